```python
import math
import jax
import jax.numpy as jnp
from jax import lax
import numpy as np

D_MODEL = 1024
BATCH = 16
SEQ = 256
DEPTH = 2
DEC_BATCH = 8
DEC_SEQ = 2048
PAST_LEN = 512

F32 = jnp.float32
GRID_W = 64
EPS = 1e-6
W_A = D_MODEL // 2
S5_H = 16
S5_G = W_A // S5_H
S5_P = 64
N_DIR = 2
W_B = D_MODEL // 2
HY_ORDER = 2
HY_SHORT = 3
HY_BANDS = 16
HY_EMB = 1 + 2 * HY_BANDS
HY_FH = 64
HY_FAST_PCT = 0.3
HY_SLOW_PCT = 1.5
HY_TARGET = 1e-2
HY_SHIFT = 0.05
W_C = D_MODEL // 2
POOL_WINDOWS = (2, 4, 8, 16)
POOL_G = W_C // 4
N_BRANCH = 3
IN_SPLIT_POINTS = (W_A, 2 * W_A, 2 * W_A + (HY_ORDER + 1) * W_B, 2 * W_A + (HY_ORDER + 2) * W_B,
                   2 * W_A + (HY_ORDER + 2) * W_B + W_C, 2 * W_A + (HY_ORDER + 2) * W_B + 2 * W_C)
IN_W = 2 * W_A + (HY_ORDER + 2) * W_B + 2 * W_C + N_BRANCH * D_MODEL

kernel_name = 'hybrid_s5_hyena_pool_diffusion_step'


def rms_norm(x, g):
    xf = x.astype(F32)
    r = lax.rsqrt(jnp.mean(xf * xf, axis=-1, keepdims=True) + EPS)
    return (xf * r * g.astype(F32)).astype(x.dtype)


def modulation(cond, w, b):
    return jax.nn.silu(cond.astype(F32)) @ w.astype(F32) + b.astype(F32)


def grid_pos_embed(L):
    rows = L // GRID_W
    r = jnp.broadcast_to(jnp.arange(rows, dtype=F32)[:, None], (rows, GRID_W)).reshape(-1)
    col = jnp.broadcast_to(jnp.arange(GRID_W, dtype=F32)[None, :], (rows, GRID_W)).reshape(-1)
    q = D_MODEL // 4
    omega = 1.0 / (10000.0 ** (jnp.arange(q, dtype=F32) / q))
    ar = r[:, None] * omega[None, :]
    ac = col[:, None] * omega[None, :]
    return jnp.concatenate([jnp.sin(ar), jnp.cos(ar), jnp.sin(ac), jnp.cos(ac)], axis=-1)


def s5_discretise(lam_re, lam_im, log_dt, b_re, b_im):
    lam_re = lam_re.astype(F32)
    lam_im = lam_im.astype(F32)
    dt = jnp.exp(log_dt.astype(F32))[:, None]
    mag = jnp.exp(lam_re * dt)
    ang = lam_im * dt
    a_re = mag * jnp.cos(ang)
    a_im = mag * jnp.sin(ang)
    n_re = a_re - 1.0
    n_im = a_im
    den = lam_re * lam_re + lam_im * lam_im
    k_re = (n_re * lam_re + n_im * lam_im) / den
    k_im = (n_im * lam_re - n_re * lam_im) / den
    br = b_re.astype(F32)
    bi = b_im.astype(F32)
    bb_re = k_re[..., None] * br - k_im[..., None] * bi
    bb_im = k_re[..., None] * bi + k_im[..., None] * br
    return a_re, a_im, bb_re, bb_im


def _complex_affine_combine(e1, e2):
    a1r, a1i, b1r, b1i = e1
    a2r, a2i, b2r, b2i = e2
    return (a2r * a1r - a2i * a1i,
            a2r * a1i + a2i * a1r,
            a2r * b1r - a2i * b1i + b2r,
            a2r * b1i + a2i * b1r + b2i)


def s5_direction(u, h0_re, h0_im, a_re, a_im, bb_re, bb_im, c_re, c_im, reverse):
    L = u.shape[1]
    bu_re = jnp.einsum('blgh,gph->blgp', u, bb_re)
    bu_im = jnp.einsum('blgh,gph->blgp', u, bb_im)
    inj_re = a_re * h0_re - a_im * h0_im
    inj_im = a_re * h0_im + a_im * h0_re
    first = L - 1 if reverse else 0
    bu_re = bu_re.at[:, first].add(inj_re)
    bu_im = bu_im.at[:, first].add(inj_im)
    ar = jnp.broadcast_to(a_re, (1, L) + a_re.shape)
    ai = jnp.broadcast_to(a_im, (1, L) + a_im.shape)
    _, _, x_re, x_im = lax.associative_scan(_complex_affine_combine, (ar, ai, bu_re, bu_im),
                                            reverse=reverse, axis=1)
    y = (jnp.einsum('blgp,ghp->blgh', x_re, c_re) - jnp.einsum('blgp,ghp->blgh', x_im, c_im))
    last = 0 if reverse else L - 1
    return y, x_re[:, last], x_im[:, last]


def s5_branch(u, h0_re, h0_im, lam_re, lam_im, log_dt, b_re, b_im, c_re, c_im, d_skip, w_glu, b_glu):
    Bn, L, _ = u.shape
    uf = u.astype(F32)
    ug = uf.reshape(Bn, L, S5_G, S5_H)
    y = uf * d_skip.astype(F32)
    fin_re = []
    fin_im = []
    for d in range(N_DIR):
        a_re, a_im, bb_re, bb_im = s5_discretise(lam_re[d], lam_im[d], log_dt[d], b_re[d], b_im[d])
        yd, fr, fi = s5_direction(ug, h0_re[:, d], h0_im[:, d], a_re, a_im, bb_re, bb_im,
                                  c_re[d].astype(F32), c_im[d].astype(F32), reverse=(d == 1))
        y = y + yd.reshape(Bn, L, W_A)
        fin_re.append(fr)
        fin_im.append(fi)
    y = jax.nn.gelu(y)
    y = y * jax.nn.sigmoid(y @ w_glu.astype(F32) + b_glu.astype(F32))
    return y, jnp.stack(fin_re, axis=1), jnp.stack(fin_im, axis=1)


def hyena_filters(L, w1, b1, w2, b2, freq, w3):
    t = jnp.arange(L, dtype=F32)
    tn = t / (L - 1)
    f = jnp.linspace(1e-4, HY_BANDS - 1, HY_BANDS, dtype=F32)
    ang = (2.0 * math.pi / L) * t[:, None] * f[None, :]
    z = jnp.concatenate([tn[:, None], jnp.cos(ang), -jnp.sin(ang)], axis=-1)
    fr = freq.astype(F32)
    h = jnp.sin(fr * (z @ w1.astype(F32) + b1.astype(F32)))
    h = jnp.sin(fr * (h @ w2.astype(F32) + b2.astype(F32)))
    h = (h @ w3.astype(F32)).reshape(L, HY_ORDER, W_B)
    max_decay = math.log(HY_TARGET) / HY_FAST_PCT
    min_decay = math.log(HY_TARGET) / HY_SLOW_PCT
    deltas = jnp.abs(jnp.linspace(min_decay, max_decay, W_B, dtype=F32))
    half = L // 2
    off = jnp.abs(t - half) / half
    win = jnp.exp(-off[:, None] * deltas[None, :]) + HY_SHIFT
    return h * win[:, None, :]


def short_conv(z, w, b):
    L = z.shape[1]
    pad = HY_SHORT // 2
    zp = jnp.pad(z, ((0, 0), (pad, HY_SHORT - 1 - pad), (0, 0)))
    out = b.astype(F32)
    for k in range(HY_SHORT):
        out = out + zp[:, k:k + L] * w[k].astype(F32)
    return out


def fft_long_conv(u, h, bias):
    L = u.shape[1]
    n = 2 * L
    uf = jnp.fft.rfft(u, n=n, axis=1)
    hf = jnp.fft.rfft(h, n=n, axis=0)
    y = jnp.fft.irfft(uf * hf[None], n=n, axis=1)[:, L // 2:L // 2 + L]
    return y + u * bias.astype(F32)


def hyena_branch(u, conv_w, conv_b, w1, b1, w2, b2, freq, w3, bias):
    L = u.shape[1]
    uc = short_conv(u.astype(F32), conv_w, conv_b)
    v, x1, x2 = jnp.split(uc, HY_ORDER + 1, axis=-1)
    h = hyena_filters(L, w1, b1, w2, b2, freq, w3)
    gates = (x1, x2)
    z = v
    for o in range(HY_ORDER):
        z = gates[o] * fft_long_conv(z, h[:, o], bias[o])
    return z


def pool_branch(u, w, scale):
    Bn, L, _ = u.shape
    uf = u.astype(F32)
    cs = jnp.concatenate([jnp.zeros((Bn, 1, W_C), F32), jnp.cumsum(uf, axis=1)], axis=1)
    t = jnp.arange(L)
    outs = []
    for gi, win in enumerate(POOL_WINDOWS):
        sl = slice(gi * POOL_G, (gi + 1) * POOL_G)
        lo = jnp.clip(t - win // 2, 0, L)
        hi = jnp.clip(t - win // 2 + win, 0, L)
        csg = cs[..., sl]
        mean = (csg[:, hi] - csg[:, lo]) / (hi - lo).astype(F32)[:, None]
        outs.append((mean - uf[..., sl]) @ w[gi].astype(F32))
    return jnp.concatenate(outs, axis=-1) * scale.astype(F32)


def trunk_layer(x, mod, h0_re, h0_im, p, l):
    dt = x.dtype
    shift, scale, gate = jnp.split(mod[:, None, :], 3, axis=-1)
    h = (rms_norm(x, p['norm_g'][l]).astype(F32) * (1.0 + scale) + shift).astype(dt)
    proj = h @ p['w_in'][l]
    u_a, g_a, u_b, g_b, u_c, g_c, m = jnp.split(proj, IN_SPLIT_POINTS, axis=-1)
    y_a, st_re, st_im = s5_branch(u_a, h0_re, h0_im, p['s5_lam_re'][l], p['s5_lam_im'][l], p['s5_log_dt'][l],
                                  p['s5_b_re'][l], p['s5_b_im'][l], p['s5_c_re'][l], p['s5_c_im'][l],
                                  p['s5_d'][l], p['s5_w_glu'][l], p['s5_b_glu'][l])
    y_b = hyena_branch(u_b, p['hy_conv_w'][l], p['hy_conv_b'][l], p['hy_f_w1'][l], p['hy_f_b1'][l],
                       p['hy_f_w2'][l], p['hy_f_b2'][l], p['hy_f_freq'][l], p['hy_f_w3'][l], p['hy_bias'][l])
    y_c = pool_branch(u_c, p['pool_w'][l], p['pool_scale'][l])
    y_a = (y_a * jax.nn.silu(g_a.astype(F32))).astype(dt)
    y_b = (y_b * jax.nn.silu(g_b.astype(F32))).astype(dt)
    y_c = (y_c * jax.nn.silu(g_c.astype(F32))).astype(dt)
    m_a, m_b, m_c = jnp.split(jax.nn.sigmoid(m.astype(F32)), N_BRANCH, axis=-1)
    merged = (m_a * (y_a @ p['w_br_a'][l]).astype(F32)
              + m_b * (y_b @ p['w_br_b'][l]).astype(F32)
              + m_c * (y_c @ p['w_br_c'][l]).astype(F32))
    out = (merged.astype(dt) @ p['w_out'][l]).astype(F32)
    x_new = (x.astype(F32) + gate * out).astype(dt)
    return x_new, st_re, st_im


def setup_inputs(seed: int = 0) -> dict:
    key = jax.random.key(seed)
    ks = jax.random.split(key, 40)

    def nrm(k, shape, s):
        return jax.random.normal(k, shape, F32) * s

    D = D_MODEL
    st_shape = (DEC_BATCH, DEPTH, N_DIR, S5_G, S5_P)
    pv = (DEPTH, N_DIR, S5_G, S5_P)
    return {
        'x_prompt': nrm(ks[0], (BATCH, SEQ, D), 1.0),
        'x_sample': nrm(ks[1], (DEC_BATCH, DEC_SEQ, D), 1.0),
        'c': nrm(ks[2], (DEC_BATCH, D), 1.0),
        'state_s5_re': nrm(ks[3], st_shape, 0.5),
        'state_s5_im': nrm(ks[4], st_shape, 0.5),
        'c_ctx': nrm(ks[5], (D,), 1.0),
        'norm_g': 1.0 + nrm(ks[6], (DEPTH, D), 0.02),
        'w_mod': nrm(ks[7], (DEPTH, D, 3 * D), 0.5 * D ** -0.5),
        'b_mod': nrm(ks[8], (DEPTH, 3 * D), 0.02),
        'w_in': nrm(ks[9], (DEPTH, D, IN_W), D ** -0.5),
        's5_lam_re': -0.5 + nrm(ks[10], pv, 0.01),
        's5_lam_im': jnp.pi * jnp.arange(S5_P, dtype=F32) + nrm(ks[11], pv, 0.01),
        's5_log_dt': jax.random.uniform(ks[12], (DEPTH, N_DIR, S5_G), F32, math.log(1e-3), math.log(1e-1)),
        's5_b_re': nrm(ks[13], (DEPTH, N_DIR, S5_G, S5_P, S5_H), (2 * S5_H) ** -0.5),
        's5_b_im': nrm(ks[14], (DEPTH, N_DIR, S5_G, S5_P, S5_H), (2 * S5_H) ** -0.5),
        's5_c_re': nrm(ks[15], (DEPTH, N_DIR, S5_G, S5_H, S5_P), (2 * S5_P) ** -0.5),
        's5_c_im': nrm(ks[16], (DEPTH, N_DIR, S5_G, S5_H, S5_P), (2 * S5_P) ** -0.5),
        's5_d': nrm(ks[17], (DEPTH, W_A), 1.0),
        's5_w_glu': nrm(ks[18], (DEPTH, W_A, W_A), W_A ** -0.5),
        's5_b_glu': nrm(ks[19], (DEPTH, W_A), 0.02),
        'hy_conv_w': nrm(ks[20], (DEPTH, HY_SHORT, (HY_ORDER + 1) * W_B), HY_SHORT ** -0.5),
        'hy_conv_b': nrm(ks[21], (DEPTH, (HY_ORDER + 1) * W_B), 0.02),
        'hy_f_w1': nrm(ks[22], (DEPTH, HY_EMB, HY_FH), HY_EMB ** -0.5),
        'hy_f_b1': nrm(ks[23], (DEPTH, HY_FH), 0.02),
        'hy_f_w2': nrm(ks[24], (DEPTH, HY_FH, HY_FH), HY_FH ** -0.5),
        'hy_f_b2': nrm(ks[25], (DEPTH, HY_FH), 0.02),
        'hy_f_freq': 1.0 + nrm(ks[26], (DEPTH, HY_FH), 0.1),
        'hy_f_w3': nrm(ks[27], (DEPTH, HY_FH, HY_ORDER * W_B), 0.02),
        'hy_bias': nrm(ks[28], (DEPTH, HY_ORDER, W_B), 1.0),
        'pool_w': nrm(ks[29], (DEPTH, len(POOL_WINDOWS), POOL_G, POOL_G), POOL_G ** -0.5),
        'pool_scale': 1.0 + nrm(ks[30], (DEPTH, W_C), 0.1),
        'w_br_a': nrm(ks[31], (DEPTH, W_A, D), W_A ** -0.5),
        'w_br_b': nrm(ks[32], (DEPTH, W_B, D), W_B ** -0.5),
        'w_br_c': nrm(ks[33], (DEPTH, W_C, D), W_C ** -0.5),
        'w_out': nrm(ks[34], (DEPTH, D, D), D ** -0.5),
        'final_g': 1.0 + nrm(ks[35], (D,), 0.02),
    }


def reference(x_prompt, x_sample, c, state_s5_re, state_s5_im, c_ctx, norm_g, w_mod, b_mod, w_in,
              s5_lam_re, s5_lam_im, s5_log_dt, s5_b_re, s5_b_im, s5_c_re, s5_c_im, s5_d, s5_w_glu, s5_b_glu,
              hy_conv_w, hy_conv_b, hy_f_w1, hy_f_b1, hy_f_w2, hy_f_b2, hy_f_freq, hy_f_w3, hy_bias,
              pool_w, pool_scale, w_br_a, w_br_b, w_br_c, w_out, final_g):
    p = {'norm_g': norm_g, 'w_in': w_in,
         's5_lam_re': s5_lam_re, 's5_lam_im': s5_lam_im, 's5_log_dt': s5_log_dt,
         's5_b_re': s5_b_re, 's5_b_im': s5_b_im, 's5_c_re': s5_c_re, 's5_c_im': s5_c_im,
         's5_d': s5_d, 's5_w_glu': s5_w_glu, 's5_b_glu': s5_b_glu,
         'hy_conv_w': hy_conv_w, 'hy_conv_b': hy_conv_b, 'hy_f_w1': hy_f_w1, 'hy_f_b1': hy_f_b1,
         'hy_f_w2': hy_f_w2, 'hy_f_b2': hy_f_b2, 'hy_f_freq': hy_f_freq, 'hy_f_w3': hy_f_w3,
         'hy_bias': hy_bias, 'pool_w': pool_w, 'pool_scale': pool_scale,
         'w_br_a': w_br_a, 'w_br_b': w_br_b, 'w_br_c': w_br_c, 'w_out': w_out}

    xc = x_prompt
    zeros_state = jnp.zeros((x_prompt.shape[0], N_DIR, S5_G, S5_P), F32)
    st_re_layers = []
    st_im_layers = []
    for l in range(DEPTH):
        mod_ctx = modulation(c_ctx[None, :], w_mod[l], b_mod[l])
        xc, sr, si = trunk_layer(xc, mod_ctx, zeros_state, zeros_state, p, l)
        st_re_layers.append(sr)
        st_im_layers.append(si)
    y_prompt = rms_norm(xc, final_g)
    new_s5_re = jnp.stack(st_re_layers, axis=1)
    new_s5_im = jnp.stack(st_im_layers, axis=1)

    L = x_sample.shape[1]
    xs = (x_sample.astype(F32) + grid_pos_embed(L)[None]).astype(x_sample.dtype)
    for l in range(DEPTH):
        mod_lat = modulation(c, w_mod[l], b_mod[l])
        xs, _, _ = trunk_layer(xs, mod_lat, state_s5_re[:, l].astype(F32), state_s5_im[:, l].astype(F32), p, l)
    y_sample = rms_norm(xs, final_g)

    return (y_prompt, y_sample, new_s5_re, new_s5_im)
```

```python
import functools
import math

import jax
import jax.numpy as jnp
from jax import lax
from jax.experimental import pallas as pl
from jax.experimental.pallas import tpu as pltpu

F32 = jnp.float32
BF16 = jnp.bfloat16
HIGHEST = lax.Precision.HIGHEST

D_MODEL = 1024
DEPTH = 2
GRID_W = 64
EPS = 1e-6
W_A = D_MODEL // 2
S5_H = 16
S5_G = W_A // S5_H
S5_P = 64
W_B = D_MODEL // 2
HY_BANDS = 16
HY_FAST_PCT = 0.3
HY_SLOW_PCT = 1.5
HY_TARGET = 1e-2
HY_SHIFT = 0.05
W_C = D_MODEL // 2
POOL_WINDOWS = (2, 4, 8, 16)
POOL_G = W_C // 4
U_COLS = W_A + 3 * W_B + W_C
G_COLS = W_A + W_B + W_C + 3 * D_MODEL
S5_CHUNK = 16
S5_W = S5_CHUNK * S5_H
MOD_ROWS = 16
VMEM_LIMIT = 56 * 1024 * 1024


def _cparams(n_grid):
    return pltpu.CompilerParams(dimension_semantics=("arbitrary",) * n_grid,
                                vmem_limit_bytes=VMEM_LIMIT)


def _mod_kernel(c_ref, w_ref, b_ref, o_ref):
    c = c_ref[...]
    s = c * jax.nn.sigmoid(c)
    o_ref[0] = jnp.dot(s, w_ref[0], preferred_element_type=F32, precision=HIGHEST) + b_ref[0]


def _modulation(cond, w_mod, b_mod, tn=512):
    n = 3 * D_MODEL
    return pl.pallas_call(
        _mod_kernel,
        grid=(DEPTH, n // tn),
        in_specs=[pl.BlockSpec((MOD_ROWS, D_MODEL), lambda l, j: (0, 0)),
                  pl.BlockSpec((1, D_MODEL, tn), lambda l, j: (l, 0, j)),
                  pl.BlockSpec((1, 1, tn), lambda l, j: (l, 0, j))],
        out_specs=pl.BlockSpec((1, MOD_ROWS, tn), lambda l, j: (l, 0, j)),
        out_shape=jax.ShapeDtypeStruct((DEPTH, MOD_ROWS, n), F32),
        compiler_params=_cparams(2),
        name="modulation",
    )(cond, w_mod, b_mod.reshape(DEPTH, 1, n))


def _normed(x, mod, g):
    r = lax.rsqrt(jnp.mean(x * x, axis=-1, keepdims=True) + EPS)
    shift = mod[:, :D_MODEL]
    scale = mod[:, D_MODEL:2 * D_MODEL]
    return (x * r * g) * (1.0 + scale) + shift


def _in_kernel(*refs, has_pos):
    if has_pos:
        x_ref, pos_ref, mod_ref, g_ref, w_ref, o_ref = refs
        x = x_ref[0] + pos_ref[...]
    else:
        x_ref, mod_ref, g_ref, w_ref, o_ref = refs
        x = x_ref[0]
    h = _normed(x, mod_ref[0], g_ref[...])
    o_ref[0] = jnp.dot(h.astype(BF16), w_ref[...], preferred_element_type=F32)


def _in_proj(x, pos, mod, mod_row0, mod_stride, g, w_u, tm):
    B, L, D = x.shape
    has_pos = pos is not None
    in_specs = [pl.BlockSpec((1, tm, D), lambda b, i: (b, i, 0))]
    args = [x]
    if has_pos:
        in_specs.append(pl.BlockSpec((tm, D), lambda b, i: (i, 0)))
        args.append(pos)
    in_specs += [pl.BlockSpec((1, 1, 3 * D), lambda b, i: (mod_row0 + mod_stride * b, 0, 0)),
                 pl.BlockSpec((1, D), lambda b, i: (0, 0)),
                 pl.BlockSpec((D, U_COLS), lambda b, i: (0, 0))]
    args += [mod, g, w_u]
    return pl.pallas_call(
        functools.partial(_in_kernel, has_pos=has_pos),
        grid=(B, L // tm),
        in_specs=in_specs,
        out_specs=pl.BlockSpec((1, tm, U_COLS), lambda b, i: (b, i, 0)),
        out_shape=jax.ShapeDtypeStruct((B, L, U_COLS), F32),
        compiler_params=_cparams(2),
        name="in_proj",
    )(*args)


def _s5_kernel(u_ref, m_ref, p_ref, q_ref, a_ref, h0_ref, y_ref, fin_ref,
               sloc_ref, sinf_ref, sinb_ref, *, nc, B):
    u = u_ref[0].astype(BF16)
    sloc_ref[...] = jnp.dot(u, p_ref[0], preferred_element_type=F32)
    half = S5_W // 2
    lane = lax.broadcasted_iota(jnp.int32, (B, half), 1)
    is_fwd = lane < S5_P
    a = a_ref[0]
    ar = a[:, :half]
    ai = a[:, half:]

    def body(i, carry):
        cr, ci = carry
        jf = pl.multiple_of(i * B, B)
        jb = pl.multiple_of((nc - 1 - i) * B, B)
        sinf_ref[pl.ds(jf, B), :half] = cr
        sinf_ref[pl.ds(jf, B), half:] = ci
        sinb_ref[pl.ds(jb, B), :half] = cr
        sinb_ref[pl.ds(jb, B), half:] = ci
        lf = sloc_ref[pl.ds(jf, B), :]
        lb = sloc_ref[pl.ds(jb, B), :]
        lr = jnp.where(is_fwd, lf[:, :half], lb[:, :half])
        li = jnp.where(is_fwd, lf[:, half:], lb[:, half:])
        return (ar * cr - ai * ci + lr, ar * ci + ai * cr + li)

    h0 = h0_ref[0]
    fr, fi = lax.fori_loop(0, nc, body, (h0[:, :half], h0[:, half:]))
    fin_ref[0, :, :half] = fr
    fin_ref[0, :, half:] = fi
    lane_full = lax.broadcasted_iota(jnp.int32, (nc * B, S5_W), 1)
    is_fwd_full = (lane_full % half) < S5_P
    s_in = jnp.where(is_fwd_full, sinf_ref[...], sinb_ref[...]).astype(BF16)
    y_ref[0] = (jnp.dot(u, m_ref[0], preferred_element_type=F32)
                + jnp.dot(s_in, q_ref[0], preferred_element_type=F32))


def _s5_mix(ug, ops, h0):
    m, p, q, a16 = ops
    G, rows, _ = ug.shape
    B = h0.shape[1]
    nc = rows // B
    blk = lambda r: pl.BlockSpec((1, r, S5_W), lambda g: (g, 0, 0))
    return pl.pallas_call(
        functools.partial(_s5_kernel, nc=nc, B=B),
        grid=(G,),
        in_specs=[blk(rows), blk(S5_W), blk(S5_W), blk(S5_W), blk(1), blk(B)],
        out_specs=[blk(rows), blk(B)],
        out_shape=[jax.ShapeDtypeStruct((G, rows, S5_W), F32),
                   jax.ShapeDtypeStruct((G, B, S5_W), F32)],
        scratch_shapes=[pltpu.VMEM((rows, S5_W), F32)] * 3,
        compiler_params=_cparams(1),
        name="s5_mix",
    )(ug, m, p, q, a16, h0)


def _s5_operators(lam_re, lam_im, log_dt, b_re, b_im, c_re, c_im):
    T = S5_CHUNK
    dt = jnp.exp(log_dt)[..., None]
    mag = jnp.exp(lam_re * dt)
    ang = lam_im * dt
    a_re = mag * jnp.cos(ang)
    a_im = mag * jnp.sin(ang)
    n_re = a_re - 1.0
    n_im = a_im
    den = lam_re * lam_re + lam_im * lam_im
    k_re = (n_re * lam_re + n_im * lam_im) / den
    k_im = (n_im * lam_re - n_re * lam_im) / den
    bb_re = k_re[..., None] * b_re - k_im[..., None] * b_im
    bb_im = k_re[..., None] * b_im + k_im[..., None] * b_re

    def step(carry, _):
        pr, pi = carry
        return (pr * a_re - pi * a_im, pr * a_im + pi * a_re), (pr, pi)

    _, (ap_re, ap_im) = lax.scan(step, (jnp.ones_like(a_re), jnp.zeros_like(a_re)), None, length=T + 1)
    ca_re = c_re[None] * ap_re[:, :, :, None, :] - c_im[None] * ap_im[:, :, :, None, :]
    ca_im = c_re[None] * ap_im[:, :, :, None, :] + c_im[None] * ap_re[:, :, :, None, :]
    ab_re = ap_re[..., None] * bb_re[None] - ap_im[..., None] * bb_im[None]
    ab_im = ap_re[..., None] * bb_im[None] + ap_im[..., None] * bb_re[None]
    kk = (jnp.einsum('kdgop,dgpi->kdgoi', ca_re[:T], bb_re, precision=HIGHEST)
          - jnp.einsum('kdgop,dgpi->kdgoi', ca_im[:T], bb_im, precision=HIGHEST))
    ti = jnp.arange(T)[:, None]
    to = jnp.arange(T)[None, :]
    mf = jnp.where((to >= ti)[:, :, None, None, None], kk[jnp.clip(to - ti, 0, T - 1), 0], 0.0)
    mb = jnp.where((ti >= to)[:, :, None, None, None], kk[jnp.clip(ti - to, 0, T - 1), 1], 0.0)
    m = (mf + mb).transpose(2, 0, 4, 1, 3).reshape(S5_G, S5_W, S5_W)
    tt = jnp.arange(T)
    pf_re, pf_im = ab_re[T - 1 - tt, 0], ab_im[T - 1 - tt, 0]
    pb_re, pb_im = ab_re[tt, 1], ab_im[tt, 1]
    p = jnp.stack([pf_re, pb_re, pf_im, pb_im], axis=0)
    p = p.transpose(2, 1, 4, 0, 3).reshape(S5_G, S5_W, 4 * S5_P)
    qf_re, qf_im = ca_re[tt + 1, 0], -ca_im[tt + 1, 0]
    qb_re, qb_im = ca_re[T - tt, 1], -ca_im[T - tt, 1]
    q = jnp.stack([qf_re, qb_re, qf_im, qb_im], axis=0)
    q = q.transpose(2, 0, 4, 1, 3).reshape(S5_G, 4 * S5_P, S5_W)
    a16 = jnp.stack([ap_re[T, 0], ap_re[T, 1], ap_im[T, 0], ap_im[T, 1]], axis=1)
    a16 = a16.reshape(S5_G, 1, 4 * S5_P)
    return m.astype(BF16), p.astype(BF16), q.astype(BF16), a16


def _dft_tables(L, kf):
    N = 2 * L
    KB = L // kf
    k = jnp.arange(L, dtype=jnp.int32)[:, None]
    t = jnp.arange(L, dtype=jnp.int32)[None, :]

    def cos_sin(m):
        m = m & (2 * N - 1)
        m = jnp.where(m >= N, m - 2 * N, m)
        ang = m.astype(F32) * (math.pi / N)
        return jnp.cos(ang), jnp.sin(ang)

    c, s = cos_sin((2 * k + 1) * t)
    fm = jnp.concatenate([c.reshape(KB, kf, L), -s.reshape(KB, kf, L)], axis=1)
    cg, sg = cos_sin((2 * k + 1) * (t - L // 2))
    gm = jnp.concatenate([cg.reshape(KB, kf, L), -sg.reshape(KB, kf, L)], axis=1) * (2.0 / N)
    return fm.astype(BF16), fm.transpose(0, 2, 1).astype(BF16), gm


def _hyena_filters(L, w1, b1, w2, b2, freq, w3):
    t = jnp.arange(L, dtype=F32)
    tn = t / (L - 1)
    f = jnp.linspace(1e-4, HY_BANDS - 1, HY_BANDS, dtype=F32)
    ang = (2.0 * math.pi / L) * t[:, None] * f[None, :]
    z = jnp.concatenate([tn[:, None], jnp.cos(ang), -jnp.sin(ang)], axis=-1)
    h = jnp.sin(freq * (jnp.dot(z, w1, precision=HIGHEST) + b1))
    h = jnp.sin(freq * (jnp.dot(h, w2, precision=HIGHEST) + b2))
    h = jnp.dot(h, w3, precision=HIGHEST).reshape(L, 2, W_B)
    max_decay = math.log(HY_TARGET) / HY_FAST_PCT
    min_decay = math.log(HY_TARGET) / HY_SLOW_PCT
    deltas = jnp.abs(jnp.linspace(min_decay, max_decay, W_B, dtype=F32))
    half = L // 2
    off = jnp.abs(t - half) / half
    win = jnp.exp(-off[:, None] * deltas[None, :]) + HY_SHIFT
    return h * win[:, None, :]


def _shift_rows(u, s, row):
    L = u.shape[0]
    r = pltpu.roll(u, (-s) % L, 0)
    return jnp.where((row + s >= 0) & (row + s < L), r, 0.0)


def _hy_kernel(u_ref, cw_ref, cb_ref, hb_ref, fm_ref, fmt_ref, hf_ref, o_ref,
               zb_ref, zf_ref, acc_ref, *, L, kf, KB):
    o = pl.program_id(1)
    kb = pl.program_id(2)

    def short_conv():
        u = u_ref[0]
        w = cw_ref[...]
        row = lax.broadcasted_iota(jnp.int32, (L, 1), 0)
        return (cb_ref[...] + _shift_rows(u, -1, row) * w[0:1] + u * w[1:2]
                + _shift_rows(u, 1, row) * w[2:3])

    @pl.when((o == 0) & (kb == 0))
    def _():
        v = short_conv()
        zf_ref[...] = v
        zb_ref[...] = v.astype(BF16)

    @pl.when(kb == 0)
    def _():
        acc_ref[...] = jnp.zeros_like(acc_ref)

    uf = jnp.dot(fm_ref[0], zb_ref[...], preferred_element_type=F32)
    hf = hf_ref[0, 0]
    ur, ui = uf[:kf], uf[kf:]
    hr, hi = hf[:kf], hf[kf:]
    y = jnp.concatenate([ur * hr - ui * hi, ur * hi + ui * hr], axis=0).astype(BF16)
    acc_ref[...] += jnp.dot(fmt_ref[0], y, preferred_element_type=F32)

    @pl.when(kb == KB - 1)
    def _():
        z = short_conv() * (acc_ref[...] + zf_ref[...] * hb_ref[0])

        @pl.when(o == 0)
        def _():
            zf_ref[...] = z
            zb_ref[...] = z.astype(BF16)

        @pl.when(o == 1)
        def _():
            o_ref[0] = z


def _hyena_mix(up, conv_w, conv_b, bias, fm, fmt, hf):
    B, L, _ = up.shape
    KB, kf2, _ = fm.shape
    kf = kf2 // 2
    assert KB >= 2
    col = lambda o, kb: jnp.where((o == 0) & (kb == 0), 0, 1 + o)
    return pl.pallas_call(
        functools.partial(_hy_kernel, L=L, kf=kf, KB=KB),
        grid=(B, 2, KB),
        in_specs=[pl.BlockSpec((1, L, W_B), lambda b, o, kb: (b, 0, 1 + col(o, kb))),
                  pl.BlockSpec((3, W_B), lambda b, o, kb: (0, col(o, kb))),
                  pl.BlockSpec((1, W_B), lambda b, o, kb: (0, col(o, kb))),
                  pl.BlockSpec((1, 1, W_B), lambda b, o, kb: (o, 0, 0)),
                  pl.BlockSpec((1, kf2, L), lambda b, o, kb: (kb, 0, 0)),
                  pl.BlockSpec((1, L, kf2), lambda b, o, kb: (kb, 0, 0)),
                  pl.BlockSpec((1, 1, kf2, W_B), lambda b, o, kb: (o, kb, 0, 0))],
        out_specs=pl.BlockSpec((1, L, W_B), lambda b, o, kb: (b, 0, 0)),
        out_shape=jax.ShapeDtypeStruct((B, L, W_B), F32),
        scratch_shapes=[pltpu.VMEM((L, W_B), BF16), pltpu.VMEM((L, W_B), F32), pltpu.VMEM((L, W_B), F32)],
        compiler_params=_cparams(3),
        name="hyena_mix",
    )(up, conv_w, conv_b.reshape(1, -1), bias.reshape(2, 1, W_B), fm, fmt, hf)


def _pool_kernel(u_ref, w_ref, sc_ref, o_ref, *, L):
    row = lax.broadcasted_iota(jnp.int32, (L, 1), 0)
    for gi, win in enumerate(POOL_WINDOWS):
        u = u_ref[0, :, gi * POOL_G:(gi + 1) * POOL_G]
        s = u
        for off in range(-(win // 2), win - win // 2):
            if off != 0:
                s = s + _shift_rows(u, off, row)
        lo = jnp.maximum(row - win // 2, 0)
        hi = jnp.minimum(row - win // 2 + win, L)
        d = s / (hi - lo).astype(F32) - u
        o_ref[0, :, gi * POOL_G:(gi + 1) * POOL_G] = (
            jnp.dot(d.astype(BF16), w_ref[gi], preferred_element_type=F32)
            * sc_ref[:, gi * POOL_G:(gi + 1) * POOL_G])


def _pool_mix(up, w, scale):
    B, L, _ = up.shape
    ucol = (W_A + 3 * W_B) // W_C
    return pl.pallas_call(
        functools.partial(_pool_kernel, L=L),
        grid=(B,),
        in_specs=[pl.BlockSpec((1, L, W_C), lambda b: (b, 0, ucol)),
                  pl.BlockSpec((len(POOL_WINDOWS), POOL_G, POOL_G), lambda b: (0, 0, 0)),
                  pl.BlockSpec((1, W_C), lambda b: (0, 0))],
        out_specs=pl.BlockSpec((1, L, W_C), lambda b: (b, 0, 0)),
        out_shape=jax.ShapeDtypeStruct((B, L, W_C), F32),
        compiler_params=_cparams(1),
        name="pool_mix",
    )(up, w, scale.reshape(1, W_C))


def _silu(x):
    return x * jax.nn.sigmoid(x)


def _gelu_tanh(x):
    return x * (0.5 * (1.0 + jnp.tanh(math.sqrt(2.0 / math.pi) * (x + 0.044715 * (x * x * x)))))


def _out_kernel(*refs, has_pos, final):
    refs = list(refs)
    x_ref = refs.pop(0)
    x = x_ref[0]
    if has_pos:
        x = x + refs.pop(0)[...]
    (mod_ref, g_ref, ua_ref, ya_ref, zb_ref, yc_ref, wg_ref, d_ref, wglu_ref, bglu_ref,
     wa_ref, wb_ref, wc_ref, wo_ref) = refs[:14]
    refs = refs[14:]
    if final:
        fg_ref, xo_ref = refs
    else:
        (xo_ref,) = refs
    mod = mod_ref[0]
    h = _normed(x, mod, g_ref[...]).astype(BF16)
    gm = jnp.dot(h, wg_ref[...], preferred_element_type=F32)
    g_a = gm[:, :W_A]
    g_b = gm[:, W_A:W_A + W_B]
    g_c = gm[:, W_A + W_B:W_A + W_B + W_C]
    mm = jax.nn.sigmoid(gm[:, W_A + W_B + W_C:])
    y = _gelu_tanh(ya_ref[0] + ua_ref[0] * d_ref[...])
    y = y * jax.nn.sigmoid(jnp.dot(y.astype(BF16), wglu_ref[...], preferred_element_type=F32) + bglu_ref[...])
    y_a = (y * _silu(g_a)).astype(BF16)
    y_b = (zb_ref[0] * _silu(g_b)).astype(BF16)
    y_c = (yc_ref[0] * _silu(g_c)).astype(BF16)
    merged = (mm[:, :D_MODEL] * jnp.dot(y_a, wa_ref[...], preferred_element_type=F32)
              + mm[:, D_MODEL:2 * D_MODEL] * jnp.dot(y_b, wb_ref[...], preferred_element_type=F32)
              + mm[:, 2 * D_MODEL:] * jnp.dot(y_c, wc_ref[...], preferred_element_type=F32))
    out = jnp.dot(merged.astype(BF16), wo_ref[...], preferred_element_type=F32)
    x_new = x + mod[:, 2 * D_MODEL:] * out
    if final:
        r = lax.rsqrt(jnp.mean(x_new * x_new, axis=-1, keepdims=True) + EPS)
        xo_ref[0] = x_new * r * fg_ref[...]
    else:
        xo_ref[0] = x_new


def _out_proj(x, pos, mod, mod_row0, mod_stride, g, up, ya, zb, yc, w_g, d, w_glu, b_glu,
              w_a, w_b, w_c, w_o, final_g, tm):
    B, L, D = x.shape
    has_pos = pos is not None
    final = final_g is not None
    tok = lambda w: pl.BlockSpec((1, tm, w), lambda b, i: (b, i, 0))
    full = lambda a: pl.BlockSpec(a.shape, lambda b, i: (0,) * a.ndim)
    in_specs = [tok(D)]
    args = [x]
    if has_pos:
        in_specs.append(pl.BlockSpec((tm, D), lambda b, i: (i, 0)))
        args.append(pos)
    weights = [w_g, d, w_glu, b_glu, w_a, w_b, w_c, w_o]
    in_specs += [pl.BlockSpec((1, 1, 3 * D), lambda b, i: (mod_row0 + mod_stride * b, 0, 0)),
                 full(g), tok(W_A), tok(W_A), tok(W_B), tok(W_C)]
    in_specs += [full(a) for a in weights]
    args += [mod, g, up, ya, zb, yc] + weights
    if final:
        in_specs.append(full(final_g))
        args.append(final_g)
    return pl.pallas_call(
        functools.partial(_out_kernel, has_pos=has_pos, final=final),
        grid=(B, L // tm),
        in_specs=in_specs,
        out_specs=tok(D),
        out_shape=jax.ShapeDtypeStruct((B, L, D), F32),
        compiler_params=_cparams(2),
        name="out_proj",
    )(*args)


def _grid_pos_embed(L):
    rows = L // GRID_W
    r = jnp.broadcast_to(jnp.arange(rows, dtype=F32)[:, None], (rows, GRID_W)).reshape(-1)
    col = jnp.broadcast_to(jnp.arange(GRID_W, dtype=F32)[None, :], (rows, GRID_W)).reshape(-1)
    q = D_MODEL // 4
    omega = 1.0 / (10000.0 ** (jnp.arange(q, dtype=F32) / q))
    ar = r[:, None] * omega[None, :]
    ac = col[:, None] * omega[None, :]
    return jnp.concatenate([jnp.sin(ar), jnp.cos(ar), jnp.sin(ac), jnp.cos(ac)], axis=-1)


def _to_chunks(u_a, B, L):
    nc = L // S5_CHUNK
    u = u_a.reshape(B, nc, S5_CHUNK, S5_G, S5_H).transpose(3, 1, 0, 2, 4)
    return u.reshape(S5_G, nc * B, S5_W)


def _from_chunks(y, B, L):
    nc = L // S5_CHUNK
    y = y.reshape(S5_G, nc, B, S5_CHUNK, S5_H).transpose(2, 1, 3, 0, 4)
    return y.reshape(B, L, W_A)


def _states_to_lanes(st_re, st_im):
    s = jnp.concatenate([st_re[:, 0], st_re[:, 1], st_im[:, 0], st_im[:, 1]], axis=-1)
    return s.transpose(1, 0, 2)


def _lanes_to_states(fin):
    f = fin.transpose(1, 0, 2).reshape(fin.shape[1], S5_G, 4, S5_P)
    return (jnp.stack([f[:, :, 0], f[:, :, 1]], axis=1), jnp.stack([f[:, :, 2], f[:, :, 3]], axis=1))


def kernel(x_prompt, x_sample, c, state_s5_re, state_s5_im, c_ctx, norm_g, w_mod, b_mod, w_in, s5_lam_re, s5_lam_im, s5_log_dt, s5_b_re, s5_b_im, s5_c_re, s5_c_im, s5_d, s5_w_glu, s5_b_glu, hy_conv_w, hy_conv_b, hy_f_w1, hy_f_b1, hy_f_w2, hy_f_b2, hy_f_freq, hy_f_w3, hy_bias, pool_w, pool_scale, w_br_a, w_br_b, w_br_c, w_out, final_g):
    Bc, Lc, D = x_prompt.shape
    Bl, Ll, _ = x_sample.shape
    assert Bl + 1 <= MOD_ROWS

    cond = jnp.zeros((MOD_ROWS, D), F32).at[0].set(c_ctx).at[1:1 + Bl].set(c)
    mod = _modulation(cond, w_mod, b_mod).reshape(DEPTH * MOD_ROWS, 1, 3 * D)

    pos = _grid_pos_embed(Ll)
    groups = {
        'ctx': dict(B=Bc, L=Lc, kf=min(128, Lc // 2), tm_in=Lc, tm_out=Lc, row0=0, stride=0),
        'lat': dict(B=Bl, L=Ll, kf=min(256, Ll // 2), tm_in=min(512, Ll), tm_out=min(256, Ll), row0=1, stride=1),
    }
    tables = {k: _dft_tables(v['L'], v['kf']) for k, v in groups.items()}

    sp = (0, W_A, 2 * W_A, 2 * W_A + 3 * W_B, 2 * W_A + 4 * W_B, 2 * W_A + 4 * W_B + W_C,
          2 * W_A + 4 * W_B + 2 * W_C)
    xs = {'ctx': x_prompt, 'lat': x_sample}
    new_re, new_im = [], []
    for l in range(DEPTH):
        wl = w_in[l]
        w_u = jnp.concatenate([wl[:, sp[0]:sp[1]], wl[:, sp[2]:sp[3]], wl[:, sp[4]:sp[5]]], axis=1).astype(BF16)
        w_g = jnp.concatenate([wl[:, sp[1]:sp[2]], wl[:, sp[3]:sp[4]], wl[:, sp[5]:]], axis=1).astype(BF16)
        s5_ops = _s5_operators(s5_lam_re[l], s5_lam_im[l], s5_log_dt[l], s5_b_re[l], s5_b_im[l],
                               s5_c_re[l], s5_c_im[l])
        g = norm_g[l].reshape(1, D)
        for name, cfg in groups.items():
            B, L = cfg['B'], cfg['L']
            x = xs[name]
            p = pos if (name == 'lat' and l == 0) else None
            row0 = l * MOD_ROWS + cfg['row0']
            up = _in_proj(x, p, mod, row0, cfg['stride'], g, w_u, cfg['tm_in'])
            if name == 'ctx':
                h0g = jnp.zeros((S5_G, B, 4 * S5_P), F32)
            else:
                h0g = _states_to_lanes(state_s5_re[:, l], state_s5_im[:, l])
            ya, fin = _s5_mix(_to_chunks(up[:, :, :W_A], B, L), s5_ops, h0g)
            ya = _from_chunks(ya, B, L)
            if name == 'ctx':
                fr, fi = _lanes_to_states(fin)
                new_re.append(fr)
                new_im.append(fi)
            fm, fmt, gm = tables[name]
            filt = _hyena_filters(L, hy_f_w1[l], hy_f_b1[l], hy_f_w2[l], hy_f_b2[l], hy_f_freq[l], hy_f_w3[l])
            hf = jnp.einsum('bkd,dc->bkc', gm, filt.reshape(L, 2 * W_B), precision=HIGHEST)
            hf = hf.reshape(hf.shape[0], hf.shape[1], 2, W_B).transpose(2, 0, 1, 3)
            zb = _hyena_mix(up, hy_conv_w[l], hy_conv_b[l], hy_bias[l], fm, fmt, hf)
            yc = _pool_mix(up, pool_w[l].astype(BF16), pool_scale[l])
            fg = final_g.reshape(1, D) if l == DEPTH - 1 else None
            xs[name] = _out_proj(x, p, mod, row0, cfg['stride'], g, up, ya, zb, yc, w_g,
                                 s5_d[l].reshape(1, W_A), s5_w_glu[l].astype(BF16), s5_b_glu[l].reshape(1, W_A),
                                 w_br_a[l].astype(BF16), w_br_b[l].astype(BF16), w_br_c[l].astype(BF16),
                                 w_out[l].astype(BF16), fg, cfg['tm_out'])
    return (xs['ctx'], xs['lat'], jnp.stack(new_re, axis=1), jnp.stack(new_im, axis=1))
```

```python
import functools
import math

import jax
import jax.numpy as jnp
from jax import lax
from jax.experimental import pallas as pl
from jax.experimental.pallas import tpu as pltpu

F32 = jnp.float32
BF16 = jnp.bfloat16
HIGHEST = lax.Precision.HIGHEST

D_MODEL = 1024
DEPTH = 2
GRID_W = 64
EPS = 1e-6
W_A = D_MODEL // 2
S5_H = 16
S5_G = W_A // S5_H
S5_P = 64
W_B = D_MODEL // 2
HY_BANDS = 16
HY_FAST_PCT = 0.3
HY_SLOW_PCT = 1.5
HY_TARGET = 1e-2
HY_SHIFT = 0.05
W_C = D_MODEL // 2
POOL_WINDOWS = (2, 4, 8, 16)
POOL_G = W_C // 4
LANES = 128
CB = 512
U_BLOCKS = (0, 2, 3, 4, 6)
G_BLOCKS = (1, 5, 7)
M_BLOCK0 = 4
U_COLS = CB * len(U_BLOCKS)
S5_CHUNK = 16
S5_W = S5_CHUNK * S5_H
S5_GB = LANES // S5_H
MOD_ROWS = 16
MOD_LAT0 = 8
VMEM_LIMIT = 56 * 1024 * 1024


def _cparams(n_grid):
    return pltpu.CompilerParams(dimension_semantics=("arbitrary",) * n_grid,
                                vmem_limit_bytes=VMEM_LIMIT)


def _mod_kernel(c_ref, w_ref, b_ref, o_ref):
    c = c_ref[...]
    s = c * jax.nn.sigmoid(c)
    o_ref[0] = jnp.dot(s, w_ref[0], preferred_element_type=F32, precision=HIGHEST) + b_ref[0]


def _modulation(cond, w_mod, b_mod, tn=512):
    n = 3 * D_MODEL
    return pl.pallas_call(
        _mod_kernel,
        grid=(DEPTH, n // tn),
        in_specs=[pl.BlockSpec((MOD_ROWS, D_MODEL), lambda l, j: (0, 0)),
                  pl.BlockSpec((1, D_MODEL, tn), lambda l, j: (l, 0, j)),
                  pl.BlockSpec((1, 1, tn), lambda l, j: (l, 0, j))],
        out_specs=pl.BlockSpec((1, MOD_ROWS, tn), lambda l, j: (l, 0, j)),
        out_shape=jax.ShapeDtypeStruct((DEPTH, MOD_ROWS, n), F32),
        compiler_params=_cparams(2),
        name="modulation",
    )(cond, w_mod, b_mod.reshape(DEPTH, 1, n))


def _by_batch(x, bm):
    return x if bm == 1 else x.reshape(x.shape[0] // bm, bm, x.shape[1])


def _mod_part(mod, i, bm):
    m = mod[:bm, i * D_MODEL:(i + 1) * D_MODEL]
    return m if bm == 1 else m[None]


def _load_x(x_ref, pos_ref, bm):
    x = _by_batch(x_ref[...], bm)
    if pos_ref is not None:
        x = x + pos_ref[...][:, None, :]
    return x


def _normed(x, mod, g, bm):
    r = lax.rsqrt(jnp.mean(x * x, axis=-1, keepdims=True) + EPS)
    h = (x * r * g) * (1.0 + _mod_part(mod, 1, bm)) + _mod_part(mod, 0, bm)
    return h.reshape(-1, D_MODEL).astype(BF16)


def _in_kernel(*refs, has_pos, bm):
    refs = list(refs)
    x_ref = refs.pop(0)
    pos_ref = refs.pop(0) if has_pos else None
    mod_ref, g_ref = refs[:2]
    w_refs = refs[2:2 + len(U_BLOCKS)]
    o_ref = refs[-1]
    h = _normed(_load_x(x_ref, pos_ref, bm), mod_ref[...], g_ref[...], bm)
    for i, w_ref in enumerate(w_refs):
        o_ref[:, i * CB:(i + 1) * CB] = jnp.dot(h, w_ref[0], preferred_element_type=F32)


def _x_specs(x, pos, mod, mod_blk, bm, tm):
    D = D_MODEL
    specs = [pl.BlockSpec((tm, D), lambda i: (i, 0))]
    args = [x]
    if pos is not None:
        specs.append(pl.BlockSpec((tm // bm, D), lambda i: (i, 0)))
        args.append(pos)
    specs.append(pl.BlockSpec((MOD_LAT0, 3 * D), lambda i: (mod_blk, 0)))
    args.append(mod)
    return specs, args


def _in_proj(x, pos, mod, mod_blk, bm, g, w_in, l, tm):
    rows, D = x.shape
    in_specs, args = _x_specs(x, pos, mod, mod_blk, bm, tm)
    in_specs.append(pl.BlockSpec((1, D), lambda i: (0, 0)))
    args.append(g)
    for cb in U_BLOCKS:
        in_specs.append(pl.BlockSpec((1, D, CB), lambda i, cb=cb: (l, 0, cb)))
        args.append(w_in)
    return pl.pallas_call(
        functools.partial(_in_kernel, has_pos=pos is not None, bm=bm),
        grid=(rows // tm,),
        in_specs=in_specs,
        out_specs=pl.BlockSpec((tm, U_COLS), lambda i: (i, 0)),
        out_shape=jax.ShapeDtypeStruct((rows, U_COLS), F32),
        compiler_params=_cparams(1),
        name="in_proj",
    )(*args)


S5_RB = 64
S5_GS = 4


def _s5_kernel(u_ref, m_ref, p_ref, q_ref, a_ref, h0_ref, y_ref, fin_ref,
               x_ref, yall_ref, sloc_ref, sinf_ref, sinb_ref, *, nc, B):
    rows = nc * B
    cpb = S5_RB // B
    half = S5_W // 2
    lane_grp = lax.broadcasted_iota(jnp.int32, (S5_RB, LANES), 1) // S5_H

    lane = lax.broadcasted_iota(jnp.int32, (B, half), 1)
    is_fwd = lane < S5_P
    lane_full = lax.broadcasted_iota(jnp.int32, (rows, S5_W), 1)
    is_fwd_full = (lane_full % half) < S5_P

    def gather_step(rb, _, g0):
        c0 = pl.multiple_of(rb * cpb, cpb)
        r0 = pl.multiple_of(rb * S5_RB, S5_RB)
        slabs = [u_ref[pl.ds(c0, cpb), t * B:(t + 1) * B, :].reshape(S5_RB, LANES) for t in range(S5_CHUNK)]
        for gi in range(S5_GS):
            g8 = g0 + gi
            for hf in range(2):
                acc = None
                for k in range(S5_GB):
                    s = slabs[hf * S5_GB + k]
                    r = s if k == g8 else pltpu.roll(s, ((k - g8) * S5_H) % LANES, 1)
                    acc = r if acc is None else jnp.where(lane_grp == k, r, acc)
                x_ref[gi, pl.ds(r0, S5_RB), hf * LANES:(hf + 1) * LANES] = acc.astype(BF16)
        return 0

    def group_step(gi, _, g0):
        g8 = g0 + gi
        u = x_ref[gi]
        sloc_ref[...] = jnp.dot(u, p_ref[g8], preferred_element_type=F32)
        a = a_ref[g8]
        ar = a[:, :half]
        ai = a[:, half:]

        def body(i, carry):
            cr, ci = carry
            jf = pl.multiple_of(i * B, B)
            jb = pl.multiple_of((nc - 1 - i) * B, B)
            sinf_ref[pl.ds(jf, B), :half] = cr
            sinf_ref[pl.ds(jf, B), half:] = ci
            sinb_ref[pl.ds(jb, B), :half] = cr
            sinb_ref[pl.ds(jb, B), half:] = ci
            lf = sloc_ref[pl.ds(jf, B), :]
            lb = sloc_ref[pl.ds(jb, B), :]
            lr = jnp.where(is_fwd, lf[:, :half], lb[:, :half])
            li = jnp.where(is_fwd, lf[:, half:], lb[:, half:])
            return (ar * cr - ai * ci + lr, ar * ci + ai * cr + li)

        h0 = h0_ref[g8]
        fr, fi = lax.fori_loop(0, nc, body, (h0[:, :half], h0[:, half:]))
        fin_ref[g8, :, :half] = fr
        fin_ref[g8, :, half:] = fi
        s_in = jnp.where(is_fwd_full, sinf_ref[...], sinb_ref[...]).astype(BF16)
        yall_ref[gi] = (jnp.dot(u, m_ref[g8], preferred_element_type=F32)
                        + jnp.dot(s_in, q_ref[g8], preferred_element_type=F32))
        return 0

    def scatter_step(rb, _, g0):
        c0 = pl.multiple_of(rb * cpb, cpb)
        r0 = pl.multiple_of(rb * S5_RB, S5_RB)
        for hf in range(2):
            ys = [yall_ref[gi, pl.ds(r0, S5_RB), hf * LANES:(hf + 1) * LANES] for gi in range(S5_GS)]
            for k in range(S5_GB):
                acc = None
                for gi in range(S5_GS):
                    g8 = g0 + gi
                    r = ys[gi] if k == g8 else pltpu.roll(ys[gi], ((g8 - k) * S5_H) % LANES, 1)
                    acc = r if acc is None else jnp.where(lane_grp == g8, r, acc)
                t = hf * S5_GB + k
                dst = (pl.ds(c0, cpb), slice(t * B, (t + 1) * B), slice(None))
                if g0 > 0:
                    acc = jnp.where(lane_grp >= g0, acc, y_ref[dst].reshape(S5_RB, LANES))
                y_ref[dst] = acc.reshape(cpb, B, LANES)
        return 0

    for g0 in range(0, S5_GB, S5_GS):
        lax.fori_loop(0, rows // S5_RB, functools.partial(gather_step, g0=g0), 0)
        lax.fori_loop(0, S5_GS, functools.partial(group_step, g0=g0), 0)
        lax.fori_loop(0, rows // S5_RB, functools.partial(scatter_step, g0=g0), 0)


def _s5_mix(up, ops, h0, B):
    m, p, q, a16 = ops
    rows_all = up.shape[0]
    nc = rows_all // (S5_CHUNK * B)
    rows = nc * B
    up3 = up.reshape(nc, S5_CHUNK * B, U_COLS)
    gblk = lambda r: pl.BlockSpec((S5_GB, r, S5_W), lambda j: (j, 0, 0))
    tok = pl.BlockSpec((nc, S5_CHUNK * B, LANES), lambda j: (0, 0, j))
    y, fin = pl.pallas_call(
        functools.partial(_s5_kernel, nc=nc, B=B),
        grid=(W_A // LANES,),
        in_specs=[tok, gblk(S5_W), gblk(S5_W), gblk(S5_W), gblk(1), gblk(B)],
        out_specs=[tok, gblk(B)],
        out_shape=[jax.ShapeDtypeStruct((nc, S5_CHUNK * B, W_A), F32),
                   jax.ShapeDtypeStruct((S5_G, B, S5_W), F32)],
        scratch_shapes=[pltpu.VMEM((S5_GS, rows, S5_W), BF16), pltpu.VMEM((S5_GS, rows, S5_W), F32),
                        pltpu.VMEM((rows, S5_W), F32), pltpu.VMEM((rows, S5_W), F32),
                        pltpu.VMEM((rows, S5_W), F32)],
        compiler_params=_cparams(1),
        name="s5_mix",
    )(up3, m, p, q, a16, h0)
    return y.reshape(rows_all, W_A), fin


def _toeplitz(k):
    T = k.shape[0]
    kext = jnp.concatenate([k, jnp.zeros_like(k)], axis=0)
    tiled = jnp.tile(kext, (T,) + (1,) * (k.ndim - 1))[:T * (2 * T - 1)]
    return tiled.reshape((T, 2 * T - 1) + k.shape[1:])[:, :T]


def _s5_operators(lam_re, lam_im, log_dt, b_re, b_im, c_re, c_im):
    T = S5_CHUNK
    dt = jnp.exp(log_dt)[..., None]
    mag = jnp.exp(lam_re * dt)
    ang = lam_im * dt
    a_re = mag * jnp.cos(ang)
    a_im = mag * jnp.sin(ang)
    n_re = a_re - 1.0
    n_im = a_im
    den = lam_re * lam_re + lam_im * lam_im
    k_re = (n_re * lam_re + n_im * lam_im) / den
    k_im = (n_im * lam_re - n_re * lam_im) / den
    bb_re = k_re[..., None] * b_re - k_im[..., None] * b_im
    bb_im = k_re[..., None] * b_im + k_im[..., None] * b_re

    def step(carry, _):
        pr, pi = carry
        return (pr * a_re - pi * a_im, pr * a_im + pi * a_re), (pr, pi)

    _, (ap_re, ap_im) = lax.scan(step, (jnp.ones_like(a_re), jnp.zeros_like(a_re)), None, length=T + 1)
    ca_re = c_re[None] * ap_re[:, :, :, None, :] - c_im[None] * ap_im[:, :, :, None, :]
    ca_im = c_re[None] * ap_im[:, :, :, None, :] + c_im[None] * ap_re[:, :, :, None, :]
    ab_re = ap_re[..., None] * bb_re[None] - ap_im[..., None] * bb_im[None]
    ab_im = ap_re[..., None] * bb_im[None] + ap_im[..., None] * bb_re[None]
    kk = (jnp.einsum('kdgop,dgpi->kdgoi', ca_re[:T], bb_re, precision=HIGHEST)
          - jnp.einsum('kdgop,dgpi->kdgoi', ca_im[:T], bb_im, precision=HIGHEST))
    mf = _toeplitz(kk[:, 0])
    mb = _toeplitz(kk[:, 1]).transpose(1, 0, 2, 3, 4)
    m = (mf + mb).transpose(2, 0, 4, 1, 3).reshape(S5_G, S5_W, S5_W)
    pf_re, pf_im = ab_re[T - 1::-1, 0], ab_im[T - 1::-1, 0]
    pb_re, pb_im = ab_re[:T, 1], ab_im[:T, 1]
    p = jnp.stack([pf_re, pb_re, pf_im, pb_im], axis=0)
    p = p.transpose(2, 1, 4, 0, 3).reshape(S5_G, S5_W, 4 * S5_P)
    qf_re, qf_im = ca_re[1:T + 1, 0], -ca_im[1:T + 1, 0]
    qb_re, qb_im = ca_re[T:0:-1, 1], -ca_im[T:0:-1, 1]
    q = jnp.stack([qf_re, qb_re, qf_im, qb_im], axis=0)
    q = q.transpose(2, 0, 4, 1, 3).reshape(S5_G, 4 * S5_P, S5_W)
    a16 = jnp.stack([ap_re[T, 0], ap_re[T, 1], ap_im[T, 0], ap_im[T, 1]], axis=1)
    a16 = a16.reshape(S5_G, 1, 4 * S5_P)
    return m.astype(BF16), p.astype(BF16), q.astype(BF16), a16


def _phase_tables(u, va, vb, ncb, n):
    def trig(m):
        m = m & (2 * n - 1)
        m = jnp.where(m >= n, m - 2 * n, m)
        ang = m.astype(F32) * (math.pi / n)
        return jnp.cos(ang), jnp.sin(ang)

    lane = lax.broadcasted_iota(jnp.int32, (1, LANES), 1)
    cb, sb = trig(u * (va * lane + vb))
    ca_all, sa_all = trig(u * (va * LANES * lane))
    out = []
    for ch in range(ncb):
        ca = ca_all[:, ch:ch + 1]
        sa = sa_all[:, ch:ch + 1]
        out.append((ca * cb - sa * sb, sa * cb + ca * sb))
    return out


def _tables_kernel(fm_ref, fmt_ref, gm_ref, blk_ref, *, L, kf):
    n = 2 * L
    kb = pl.program_id(0)
    u = 2 * (kb * kf + lax.broadcasted_iota(jnp.int32, (kf, 1), 0)) + 1
    for ch, (c, s) in enumerate(_phase_tables(u, 1, 0, L // LANES, n)):
        blk_ref[:kf, ch * LANES:(ch + 1) * LANES] = c
        blk_ref[kf:, ch * LANES:(ch + 1) * LANES] = -s
    blk = blk_ref[...]
    fm_ref[0] = blk.astype(BF16)
    fmt_ref[0] = blk.T.astype(BF16)
    for ch, (c, s) in enumerate(_phase_tables(u, 1, -(L // 2), L // LANES, n)):
        gm_ref[0, :kf, ch * LANES:(ch + 1) * LANES] = (c * (2.0 / n)).astype(BF16)
        gm_ref[0, kf:, ch * LANES:(ch + 1) * LANES] = (s * (-2.0 / n)).astype(BF16)


def _dft_tables(L, kf):
    KB = L // kf
    return pl.pallas_call(
        functools.partial(_tables_kernel, L=L, kf=kf),
        grid=(KB,),
        out_specs=[pl.BlockSpec((1, 2 * kf, L), lambda kb: (kb, 0, 0)),
                   pl.BlockSpec((1, L, 2 * kf), lambda kb: (kb, 0, 0)),
                   pl.BlockSpec((1, 2 * kf, L), lambda kb: (kb, 0, 0))],
        out_shape=[jax.ShapeDtypeStruct((KB, 2 * kf, L), BF16),
                   jax.ShapeDtypeStruct((KB, L, 2 * kf), BF16),
                   jax.ShapeDtypeStruct((KB, 2 * kf, L), BF16)],
        scratch_shapes=[pltpu.VMEM((2 * kf, L), F32)],
        compiler_params=_cparams(1),
        name="dft_tables",
    )()


def _spectrum_kernel(gm_ref, f_ref, o_ref):
    r = jnp.dot(gm_ref[0], f_ref[...], preferred_element_type=F32)
    o_ref[0, 0] = r[:, :W_B]
    o_ref[1, 0] = r[:, W_B:]


def _filter_spectrum(gm, filt):
    KB, kf2, L = gm.shape
    return pl.pallas_call(
        _spectrum_kernel,
        grid=(KB,),
        in_specs=[pl.BlockSpec((1, kf2, L), lambda kb: (kb, 0, 0)),
                  pl.BlockSpec((L, 2 * W_B), lambda kb: (0, 0))],
        out_specs=pl.BlockSpec((2, 1, kf2, W_B), lambda kb: (0, kb, 0, 0)),
        out_shape=jax.ShapeDtypeStruct((2, KB, kf2, W_B), F32),
        compiler_params=_cparams(1),
        name="filter_spectrum",
    )(gm, filt)


def _hyena_filters(L, w1, b1, w2, b2, freq, w3):
    t = jnp.arange(L, dtype=F32)
    tn = t / (L - 1)
    f = jnp.linspace(1e-4, HY_BANDS - 1, HY_BANDS, dtype=F32)
    ang = (2.0 * math.pi / L) * t[:, None] * f[None, :]
    z = jnp.concatenate([tn[:, None], jnp.cos(ang), -jnp.sin(ang)], axis=-1)
    h = jnp.sin(freq * (jnp.dot(z, w1, precision=HIGHEST) + b1))
    h = jnp.sin(freq * (jnp.dot(h, w2, precision=HIGHEST) + b2))
    h = jnp.dot(h, w3, precision=HIGHEST).reshape(L, 2, W_B)
    max_decay = math.log(HY_TARGET) / HY_FAST_PCT
    min_decay = math.log(HY_TARGET) / HY_SLOW_PCT
    deltas = jnp.abs(jnp.linspace(min_decay, max_decay, W_B, dtype=F32))
    half = L // 2
    off = jnp.abs(t - half) / half
    win = jnp.exp(-off[:, None] * deltas[None, :]) + HY_SHIFT
    return h * win[:, None, :]


def _shift_rows(u, s, row):
    L = u.shape[0]
    r = pltpu.roll(u, (-s) % L, 0)
    return jnp.where((row + s >= 0) & (row + s < L), r, 0.0)


def _hy_kernel(u_ref, cw_ref, cb_ref, hb_ref, fm_ref, fmt_ref, hf_ref, o_ref,
               zb_ref, zf_ref, acc_ref, *, L, kf, KB):
    o = pl.program_id(1)
    kb = pl.program_id(2)

    def short_conv():
        u = u_ref[...]
        w = cw_ref[...]
        row = lax.broadcasted_iota(jnp.int32, (L, 1), 0)
        return (cb_ref[...] + _shift_rows(u, -1, row) * w[0:1] + u * w[1:2]
                + _shift_rows(u, 1, row) * w[2:3])

    @pl.when((o == 0) & (kb == 0))
    def _():
        v = short_conv()
        zf_ref[...] = v
        zb_ref[...] = v.astype(BF16)

    @pl.when(kb == 0)
    def _():
        acc_ref[...] = jnp.zeros_like(acc_ref)

    uf = jnp.dot(fm_ref[0], zb_ref[...], preferred_element_type=F32)
    hf = hf_ref[0, 0]
    ur, ui = uf[:kf], uf[kf:]
    hr, hi = hf[:kf], hf[kf:]
    y = jnp.concatenate([ur * hr - ui * hi, ur * hi + ui * hr], axis=0).astype(BF16)
    acc_ref[...] += jnp.dot(fmt_ref[0], y, preferred_element_type=F32)

    @pl.when(kb == KB - 1)
    def _():
        z = short_conv() * (acc_ref[...] + zf_ref[...] * hb_ref[0])

        @pl.when(o == 0)
        def _():
            zf_ref[...] = z
            zb_ref[...] = z.astype(BF16)

        @pl.when(o == 1)
        def _():
            o_ref[...] = z


def _hyena_mix(up, B, conv_w, conv_b, bias, fm, fmt, hf):
    L = up.shape[0] // B
    KB, kf2, _ = fm.shape
    kf = kf2 // 2
    assert KB >= 2
    nblk = U_COLS // W_B
    col = lambda o, kb: jnp.where((o == 0) & (kb == 0), 0, 1 + o)
    return pl.pallas_call(
        functools.partial(_hy_kernel, L=L, kf=kf, KB=KB),
        grid=(B, 2, KB),
        in_specs=[pl.BlockSpec((L, W_B), lambda b, o, kb: (0, b * nblk + 1 + col(o, kb))),
                  pl.BlockSpec((3, W_B), lambda b, o, kb: (0, col(o, kb))),
                  pl.BlockSpec((1, W_B), lambda b, o, kb: (0, col(o, kb))),
                  pl.BlockSpec((1, 1, W_B), lambda b, o, kb: (o, 0, 0)),
                  pl.BlockSpec((1, kf2, L), lambda b, o, kb: (kb, 0, 0)),
                  pl.BlockSpec((1, L, kf2), lambda b, o, kb: (kb, 0, 0)),
                  pl.BlockSpec((1, 1, kf2, W_B), lambda b, o, kb: (o, kb, 0, 0))],
        out_specs=pl.BlockSpec((L, W_B), lambda b, o, kb: (0, b)),
        out_shape=jax.ShapeDtypeStruct((L, B * W_B), F32),
        scratch_shapes=[pltpu.VMEM((L, W_B), BF16), pltpu.VMEM((L, W_B), F32), pltpu.VMEM((L, W_B), F32)],
        compiler_params=_cparams(3),
        name="hyena_mix",
    )(up.reshape(L, B * U_COLS), conv_w, conv_b.reshape(1, -1), bias.reshape(2, 1, W_B), fm, fmt, hf
      ).reshape(L * B, W_B)


def _pool_kernel(u_ref, w_ref, sc_ref, o_ref, *, L):
    row = lax.broadcasted_iota(jnp.int32, (L, 1), 0)
    for gi, win in enumerate(POOL_WINDOWS):
        u = u_ref[:, gi * POOL_G:(gi + 1) * POOL_G]
        s = u
        for off in range(-(win // 2), win - win // 2):
            if off != 0:
                s = s + _shift_rows(u, off, row)
        lo = jnp.maximum(row - win // 2, 0)
        hi = jnp.minimum(row - win // 2 + win, L)
        d = s / (hi - lo).astype(F32) - u
        o_ref[:, gi * POOL_G:(gi + 1) * POOL_G] = (
            jnp.dot(d.astype(BF16), w_ref[gi], preferred_element_type=F32)
            * sc_ref[:, gi * POOL_G:(gi + 1) * POOL_G])


def _pool_mix(up, B, w, scale):
    L = up.shape[0] // B
    nblk = U_COLS // W_C
    return pl.pallas_call(
        functools.partial(_pool_kernel, L=L),
        grid=(B,),
        in_specs=[pl.BlockSpec((L, W_C), lambda b: (0, b * nblk + nblk - 1)),
                  pl.BlockSpec((len(POOL_WINDOWS), POOL_G, POOL_G), lambda b: (0, 0, 0)),
                  pl.BlockSpec((1, W_C), lambda b: (0, 0))],
        out_specs=pl.BlockSpec((L, W_C), lambda b: (0, b)),
        out_shape=jax.ShapeDtypeStruct((L, B * W_C), F32),
        compiler_params=_cparams(1),
        name="pool_mix",
    )(up.reshape(L, B * U_COLS), w, scale.reshape(1, W_C)).reshape(L * B, W_C)


def _silu(x):
    return x * jax.nn.sigmoid(x)


def _gelu_tanh(x):
    return x * (0.5 * (1.0 + jnp.tanh(math.sqrt(2.0 / math.pi) * (x + 0.044715 * (x * x * x)))))


def _out_kernel(*refs, has_pos, final, bm):
    refs = list(refs)
    x_ref = refs.pop(0)
    pos_ref = refs.pop(0) if has_pos else None
    (mod_ref, g_ref, ua_ref, ya_ref, zb_ref, yc_ref, wga_ref, wgb_ref, wgc_ref, wma_ref, wmb_ref, wmc_ref,
     d_ref, wglu_ref, bglu_ref, wa_ref, wb_ref, wc_ref, wo_ref) = refs[:19]
    refs = refs[19:]
    fg_ref = refs.pop(0) if final else None
    (xo_ref,) = refs
    mod = mod_ref[...]
    x = _load_x(x_ref, pos_ref, bm)
    h = _normed(x, mod, g_ref[...], bm)
    proj = lambda w_ref: jnp.dot(h, w_ref[0], preferred_element_type=F32)
    y = _gelu_tanh(ya_ref[...] + ua_ref[...] * d_ref[...])
    y = y * jax.nn.sigmoid(jnp.dot(y.astype(BF16), wglu_ref[...], preferred_element_type=F32) + bglu_ref[...])
    y_a = (y * _silu(proj(wga_ref))).astype(BF16)
    y_b = (zb_ref[...] * _silu(proj(wgb_ref))).astype(BF16)
    y_c = (yc_ref[...] * _silu(proj(wgc_ref))).astype(BF16)
    merged = (jax.nn.sigmoid(proj(wma_ref)) * jnp.dot(y_a, wa_ref[...], preferred_element_type=F32)
              + jax.nn.sigmoid(proj(wmb_ref)) * jnp.dot(y_b, wb_ref[...], preferred_element_type=F32)
              + jax.nn.sigmoid(proj(wmc_ref)) * jnp.dot(y_c, wc_ref[...], preferred_element_type=F32))
    out = jnp.dot(merged.astype(BF16), wo_ref[...], preferred_element_type=F32)
    x_new = x + _mod_part(mod, 2, bm) * _by_batch(out, bm)
    if final:
        r = lax.rsqrt(jnp.mean(x_new * x_new, axis=-1, keepdims=True) + EPS)
        x_new = x_new * r * fg_ref[...]
    xo_ref[...] = x_new.reshape(-1, D_MODEL)


def _out_proj(x, pos, mod, mod_blk, bm, g, up, ya, zb, yc, w_in, l, d, w_glu, b_glu,
              w_a, w_b, w_c, w_o, final_g, tm):
    rows, D = x.shape
    final = final_g is not None
    tok = lambda w: pl.BlockSpec((tm, w), lambda i: (i, 0))
    full = lambda a: pl.BlockSpec(a.shape, lambda i: (0,) * a.ndim)
    in_specs, args = _x_specs(x, pos, mod, mod_blk, bm, tm)
    in_specs += [full(g), tok(W_A), tok(W_A), tok(W_B), tok(W_C)]
    args += [g, up, ya, zb, yc]
    for cb in G_BLOCKS:
        in_specs.append(pl.BlockSpec((1, D, CB), lambda i, cb=cb: (l, 0, cb)))
        args.append(w_in)
    for k in range(3):
        in_specs.append(pl.BlockSpec((1, D, D), lambda i, k=k: (l, 0, M_BLOCK0 + k)))
        args.append(w_in)
    weights = [d, w_glu, b_glu, w_a, w_b, w_c, w_o]
    in_specs += [full(a) for a in weights]
    args += weights
    if final:
        in_specs.append(full(final_g))
        args.append(final_g)
    return pl.pallas_call(
        functools.partial(_out_kernel, has_pos=pos is not None, final=final, bm=bm),
        grid=(rows // tm,),
        in_specs=in_specs,
        out_specs=tok(D),
        out_shape=jax.ShapeDtypeStruct((rows, D), F32),
        compiler_params=_cparams(1),
        name="out_proj",
    )(*args)


def _grid_pos_embed(L):
    rows = L // GRID_W
    r = jnp.broadcast_to(jnp.arange(rows, dtype=F32)[:, None], (rows, GRID_W)).reshape(-1)
    col = jnp.broadcast_to(jnp.arange(GRID_W, dtype=F32)[None, :], (rows, GRID_W)).reshape(-1)
    q = D_MODEL // 4
    omega = 1.0 / (10000.0 ** (jnp.arange(q, dtype=F32) / q))
    ar = r[:, None] * omega[None, :]
    ac = col[:, None] * omega[None, :]
    return jnp.concatenate([jnp.sin(ar), jnp.cos(ar), jnp.sin(ac), jnp.cos(ac)], axis=-1)


def _states_to_lanes(st_re, st_im):
    s = jnp.concatenate([st_re[:, 0], st_re[:, 1], st_im[:, 0], st_im[:, 1]], axis=-1)
    return s.transpose(1, 0, 2)


def _lanes_to_states(fin):
    f = fin.transpose(1, 0, 2).reshape(fin.shape[1], S5_G, 4, S5_P)
    return (jnp.stack([f[:, :, 0], f[:, :, 1]], axis=1), jnp.stack([f[:, :, 2], f[:, :, 3]], axis=1))


def _time_major(x):
    B, L, D = x.shape
    return x.transpose(1, 0, 2).reshape(L * B, D)


def _batch_major(x, B):
    return x.reshape(x.shape[0] // B, B, x.shape[1]).transpose(1, 0, 2)


def kernel(x_prompt, x_sample, c, state_s5_re, state_s5_im, c_ctx, norm_g, w_mod, b_mod, w_in, s5_lam_re, s5_lam_im, s5_log_dt, s5_b_re, s5_b_im, s5_c_re, s5_c_im, s5_d, s5_w_glu, s5_b_glu, hy_conv_w, hy_conv_b, hy_f_w1, hy_f_b1, hy_f_w2, hy_f_b2, hy_f_freq, hy_f_w3, hy_bias, pool_w, pool_scale, w_br_a, w_br_b, w_br_c, w_out, final_g):
    Bc, Lc, D = x_prompt.shape
    Bl, Ll, _ = x_sample.shape
    assert Bl == MOD_ROWS - MOD_LAT0

    cond = jnp.zeros((MOD_ROWS, D), F32).at[0].set(c_ctx).at[MOD_LAT0:].set(c)
    mod = _modulation(cond, w_mod, b_mod).reshape(DEPTH * MOD_ROWS, 3 * D)

    pos = _grid_pos_embed(Ll)
    groups = {
        'ctx': dict(B=Bc, L=Lc, kf=min(128, Lc // 2), bm=1, mod_blk=0),
        'lat': dict(B=Bl, L=Ll, kf=min(256, Ll // 2), bm=Bl, mod_blk=1),
    }
    tables = {k: _dft_tables(v['L'], v['kf']) for k, v in groups.items()}
    w_in_b = w_in.astype(BF16)
    xs = {'ctx': _time_major(x_prompt), 'lat': _time_major(x_sample)}
    new_re, new_im = [], []
    for l in range(DEPTH):
        s5_ops = _s5_operators(s5_lam_re[l], s5_lam_im[l], s5_log_dt[l], s5_b_re[l], s5_b_im[l],
                               s5_c_re[l], s5_c_im[l])
        g = norm_g[l].reshape(1, D)
        for name, cfg in groups.items():
            B, L, bm = cfg['B'], cfg['L'], cfg['bm']
            x = xs[name]
            p = pos if (name == 'lat' and l == 0) else None
            mod_blk = l * (MOD_ROWS // MOD_LAT0) + cfg['mod_blk']
            up = _in_proj(x, p, mod, mod_blk, bm, g, w_in_b, l, min(512, L * B))
            if name == 'ctx':
                h0g = jnp.zeros((S5_G, B, 4 * S5_P), F32)
            else:
                h0g = _states_to_lanes(state_s5_re[:, l], state_s5_im[:, l])
            ya, fin = _s5_mix(up, s5_ops, h0g, B)
            if name == 'ctx':
                fr, fi = _lanes_to_states(fin)
                new_re.append(fr)
                new_im.append(fi)
            fm, fmt, gm = tables[name]
            filt = _hyena_filters(L, hy_f_w1[l], hy_f_b1[l], hy_f_w2[l], hy_f_b2[l], hy_f_freq[l], hy_f_w3[l])
            hf = _filter_spectrum(gm, filt.reshape(L, 2 * W_B).astype(BF16))
            zb = _hyena_mix(up, B, hy_conv_w[l], hy_conv_b[l], hy_bias[l], fm, fmt, hf)
            yc = _pool_mix(up, B, pool_w[l].astype(BF16), pool_scale[l])
            fg = final_g.reshape(1, D) if l == DEPTH - 1 else None
            xs[name] = _out_proj(x, p, mod, mod_blk, bm, g, up, ya, zb, yc, w_in_b, l,
                                 s5_d[l].reshape(1, W_A), s5_w_glu[l].astype(BF16), s5_b_glu[l].reshape(1, W_A),
                                 w_br_a[l].astype(BF16), w_br_b[l].astype(BF16), w_br_c[l].astype(BF16),
                                 w_out[l].astype(BF16), fg, min(256, L * B))
    return (_batch_major(xs['ctx'], Bc), _batch_major(xs['lat'], Bl),
            jnp.stack(new_re, axis=1), jnp.stack(new_im, axis=1))
```

```python
import functools
import math

import jax
import jax.numpy as jnp
from jax import lax
from jax.experimental import pallas as pl
from jax.experimental.pallas import tpu as pltpu

F32 = jnp.float32
BF16 = jnp.bfloat16
HIGHEST = lax.Precision.HIGHEST

D_MODEL = 1024
DEPTH = 2
GRID_W = 64
EPS = 1e-6
W_A = D_MODEL // 2
S5_H = 16
S5_G = W_A // S5_H
S5_P = 64
W_B = D_MODEL // 2
HY_BANDS = 16
HY_FAST_PCT = 0.3
HY_SLOW_PCT = 1.5
HY_TARGET = 1e-2
HY_SHIFT = 0.05
W_C = D_MODEL // 2
POOL_WINDOWS = (2, 4, 8, 16)
POOL_G = W_C // 4
LANES = 128
CB = 512
U_BLOCKS = (0, 2, 3, 4, 6)
G_BLOCKS = (1, 5, 7)
M_BLOCK0 = 4
UT_COLS = CB * (len(U_BLOCKS) - 1)
S5_CHUNK = 16
S5_W = S5_CHUNK * S5_H
S5_GB = LANES // S5_H
MOD_ROWS = 16
MOD_LAT0 = 8
VMEM_LIMIT = 56 * 1024 * 1024


def _cparams(n_grid):
    return pltpu.CompilerParams(dimension_semantics=("arbitrary",) * n_grid,
                                vmem_limit_bytes=VMEM_LIMIT)


def _mod_kernel(c_ref, w_ref, b_ref, o_ref):
    c = c_ref[...]
    s = c * jax.nn.sigmoid(c)
    o_ref[0] = jnp.dot(s, w_ref[0], preferred_element_type=F32, precision=HIGHEST) + b_ref[0]


def _modulation(cond, w_mod, b_mod, tn=512):
    n = 3 * D_MODEL
    return pl.pallas_call(
        _mod_kernel,
        grid=(DEPTH, n // tn),
        in_specs=[pl.BlockSpec((MOD_ROWS, D_MODEL), lambda l, j: (0, 0)),
                  pl.BlockSpec((1, D_MODEL, tn), lambda l, j: (l, 0, j)),
                  pl.BlockSpec((1, 1, tn), lambda l, j: (l, 0, j))],
        out_specs=pl.BlockSpec((1, MOD_ROWS, tn), lambda l, j: (l, 0, j)),
        out_shape=jax.ShapeDtypeStruct((DEPTH, MOD_ROWS, n), F32),
        compiler_params=_cparams(2),
        name="modulation",
    )(cond, w_mod, b_mod.reshape(DEPTH, 1, n))


def _by_batch(x, bm):
    return x if bm == 1 else x.reshape(x.shape[0] // bm, bm, x.shape[1])


def _mod_part(mod, i, bm):
    m = mod[:bm, i * D_MODEL:(i + 1) * D_MODEL]
    return m if bm == 1 else m[None]


def _stage(rows, w):
    return pltpu.VMEM((w // LANES, rows, LANES), F32)


def _put_batch(ref, b, B, val):
    for c in range(ref.shape[0]):
        ref[c, pl.ds(b, val.shape[0], stride=B), :] = val[:, c * LANES:(c + 1) * LANES]


def _get_batch(ref, b, B):
    tt = ref.shape[1] // B
    return jnp.concatenate([ref[c, pl.ds(b, tt, stride=B), :] for c in range(ref.shape[0])], axis=1)


def _put_tile(ref, val):
    for c in range(ref.shape[0]):
        ref[c] = val[:, c * LANES:(c + 1) * LANES]


def _get_tile(ref):
    return jnp.concatenate([ref[c] for c in range(ref.shape[0])], axis=1)


def _load_x(x_ref, xs_ref, pos_ref, B, bm):
    if xs_ref is not None:
        for b in range(B):
            _put_batch(xs_ref, b, B, x_ref[b])
        x2 = _get_tile(xs_ref)
    else:
        x2 = x_ref[...]
    x = _by_batch(x2, bm)
    if pos_ref is not None:
        x = x + pos_ref[...][:, None, :]
    return x


def _normed(x, mod, g, bm):
    r = lax.rsqrt(jnp.mean(x * x, axis=-1, keepdims=True) + EPS)
    h = (x * r * g) * (1.0 + _mod_part(mod, 1, bm)) + _mod_part(mod, 0, bm)
    return h.reshape(-1, D_MODEL).astype(BF16)


def _in_kernel(*refs, has_pos, batch_major, B, bm):
    refs = list(refs)
    x_ref = refs.pop(0)
    pos_ref = refs.pop(0) if has_pos else None
    mod_ref, g_ref = refs[:2]
    w_refs = refs[2:2 + len(U_BLOCKS)]
    oa_ref, ot_ref, scr_ref = refs[2 + len(U_BLOCKS):5 + len(U_BLOCKS)]
    xs_ref = refs[-1] if batch_major else None
    h = _normed(_load_x(x_ref, xs_ref, pos_ref, B, bm), mod_ref[...], g_ref[...], bm)
    oa_ref[...] = jnp.dot(h, w_refs[0][0], preferred_element_type=F32)
    for i, w_ref in enumerate(w_refs[1:]):
        _put_tile(scr_ref, jnp.dot(h, w_ref[0], preferred_element_type=F32))
        for b in range(B):
            ot_ref[:, b * UT_COLS + i * CB:b * UT_COLS + (i + 1) * CB] = _get_batch(scr_ref, b, B)


def _x_specs(x, pos, mod, mod_blk, B, bm, tm):
    D = D_MODEL
    if x.ndim == 3:
        specs = [pl.BlockSpec((B, tm // B, D), lambda i: (0, i, 0))]
    else:
        specs = [pl.BlockSpec((tm, D), lambda i: (i, 0))]
    args = [x]
    if pos is not None:
        specs.append(pl.BlockSpec((tm // bm, D), lambda i: (i, 0)))
        args.append(pos)
    specs.append(pl.BlockSpec((MOD_LAT0, 3 * D), lambda i: (mod_blk, 0)))
    args.append(mod)
    return specs, args


def _in_proj(x, pos, mod, mod_blk, B, bm, g, w_in, l, tm):
    D = D_MODEL
    rows = x.size // D
    L = rows // B
    batch_major = x.ndim == 3
    in_specs, args = _x_specs(x, pos, mod, mod_blk, B, bm, tm)
    in_specs.append(pl.BlockSpec((1, D), lambda i: (0, 0)))
    args.append(g)
    for cb in U_BLOCKS:
        in_specs.append(pl.BlockSpec((1, D, CB), lambda i, cb=cb: (l, 0, cb)))
        args.append(w_in)
    scratch = [_stage(tm, CB)] + ([_stage(tm, D)] if batch_major else [])
    return pl.pallas_call(
        functools.partial(_in_kernel, has_pos=pos is not None, batch_major=batch_major, B=B, bm=bm),
        grid=(rows // tm,),
        in_specs=in_specs,
        out_specs=[pl.BlockSpec((tm, W_A), lambda i: (i, 0)),
                   pl.BlockSpec((tm // B, B * UT_COLS), lambda i: (i, 0))],
        out_shape=[jax.ShapeDtypeStruct((rows, W_A), F32),
                   jax.ShapeDtypeStruct((L, B * UT_COLS), F32)],
        scratch_shapes=scratch,
        compiler_params=_cparams(1),
        name="in_proj",
    )(*args)


S5_RB = 64
S5_GS = 4


def _s5_kernel(u_ref, m_ref, p_ref, qt_ref, a_ref, h0_ref, y_ref, fin_ref,
               x_ref, yall_ref, sloc_ref, sinf_ref, sinb_ref, *, nc, B):
    rows = nc * B
    cpb = S5_RB // B
    half = S5_W // 2
    lane_grp = lax.broadcasted_iota(jnp.int32, (S5_RB, LANES), 1) // S5_H

    lane = lax.broadcasted_iota(jnp.int32, (B, half), 1)
    is_fwd = lane < S5_P
    lane_full = lax.broadcasted_iota(jnp.int32, (rows, S5_W), 1)
    is_fwd_full = (lane_full % half) < S5_P

    def gather_step(rb, _, g0):
        c0 = pl.multiple_of(rb * cpb, cpb)
        r0 = pl.multiple_of(rb * S5_RB, S5_RB)
        slabs = [u_ref[pl.ds(c0, cpb), t * B:(t + 1) * B, :].reshape(S5_RB, LANES) for t in range(S5_CHUNK)]
        for gi in range(S5_GS):
            g8 = g0 + gi
            for hf in range(2):
                acc = None
                for k in range(S5_GB):
                    s = slabs[hf * S5_GB + k]
                    r = s if k == g8 else pltpu.roll(s, ((k - g8) * S5_H) % LANES, 1)
                    acc = r if acc is None else jnp.where(lane_grp == k, r, acc)
                x_ref[gi, pl.ds(r0, S5_RB), hf * LANES:(hf + 1) * LANES] = acc.astype(BF16)
        return 0

    def group_step(gi, _, g0):
        g8 = g0 + gi
        u = x_ref[gi]
        sloc_ref[...] = jnp.dot(u, p_ref[g8], preferred_element_type=F32)
        a = a_ref[g8]
        ar = a[:, :half]
        ai = a[:, half:]

        def body(i, carry):
            cr, ci = carry
            jf = pl.multiple_of(i * B, B)
            jb = pl.multiple_of((nc - 1 - i) * B, B)
            sinf_ref[pl.ds(jf, B), :half] = cr
            sinf_ref[pl.ds(jf, B), half:] = ci
            sinb_ref[pl.ds(jb, B), :half] = cr
            sinb_ref[pl.ds(jb, B), half:] = ci
            lf = sloc_ref[pl.ds(jf, B), :]
            lb = sloc_ref[pl.ds(jb, B), :]
            lr = jnp.where(is_fwd, lf[:, :half], lb[:, :half])
            li = jnp.where(is_fwd, lf[:, half:], lb[:, half:])
            return (ar * cr - ai * ci + lr, ar * ci + ai * cr + li)

        h0 = h0_ref[g8]
        fr, fi = lax.fori_loop(0, nc, body, (h0[:, :half], h0[:, half:]))
        fin_ref[g8, :, :half] = fr
        fin_ref[g8, :, half:] = fi
        s_in = jnp.where(is_fwd_full, sinf_ref[...], sinb_ref[...]).astype(BF16)
        yall_ref[gi] = (jnp.dot(u, m_ref[g8], preferred_element_type=F32)
                        + lax.dot_general(s_in, qt_ref[g8], (((1,), (1,)), ((), ())),
                                          preferred_element_type=F32))
        return 0

    def scatter_step(rb, _, g0):
        c0 = pl.multiple_of(rb * cpb, cpb)
        r0 = pl.multiple_of(rb * S5_RB, S5_RB)
        for hf in range(2):
            ys = [yall_ref[gi, pl.ds(r0, S5_RB), hf * LANES:(hf + 1) * LANES] for gi in range(S5_GS)]
            for k in range(S5_GB):
                acc = None
                for gi in range(S5_GS):
                    g8 = g0 + gi
                    r = ys[gi] if k == g8 else pltpu.roll(ys[gi], ((g8 - k) * S5_H) % LANES, 1)
                    acc = r if acc is None else jnp.where(lane_grp == g8, r, acc)
                t = hf * S5_GB + k
                dst = (pl.ds(c0, cpb), slice(t * B, (t + 1) * B), slice(None))
                if g0 > 0:
                    acc = jnp.where(lane_grp >= g0, acc, y_ref[dst].reshape(S5_RB, LANES))
                y_ref[dst] = acc.reshape(cpb, B, LANES)
        return 0

    for g0 in range(0, S5_GB, S5_GS):
        lax.fori_loop(0, rows // S5_RB, functools.partial(gather_step, g0=g0), 0)
        lax.fori_loop(0, S5_GS, functools.partial(group_step, g0=g0), 0)
        lax.fori_loop(0, rows // S5_RB, functools.partial(scatter_step, g0=g0), 0)


def _s5_mix(ua, ops, h0, B):
    m, p, q, a16 = ops
    rows_all = ua.shape[0]
    nc = rows_all // (S5_CHUNK * B)
    rows = nc * B
    up3 = ua.reshape(nc, S5_CHUNK * B, W_A)
    gblk = lambda r: pl.BlockSpec((S5_GB, r, S5_W), lambda j: (j, 0, 0))
    tok = pl.BlockSpec((nc, S5_CHUNK * B, LANES), lambda j: (0, 0, j))
    y, fin = pl.pallas_call(
        functools.partial(_s5_kernel, nc=nc, B=B),
        grid=(W_A // LANES,),
        in_specs=[tok, gblk(S5_W), gblk(S5_W), gblk(S5_W), gblk(1), gblk(B)],
        out_specs=[tok, gblk(B)],
        out_shape=[jax.ShapeDtypeStruct((nc, S5_CHUNK * B, W_A), F32),
                   jax.ShapeDtypeStruct((S5_G, B, S5_W), F32)],
        scratch_shapes=[pltpu.VMEM((S5_GS, rows, S5_W), BF16), pltpu.VMEM((S5_GS, rows, S5_W), F32),
                        pltpu.VMEM((rows, S5_W), F32), pltpu.VMEM((rows, S5_W), F32),
                        pltpu.VMEM((rows, S5_W), F32)],
        compiler_params=_cparams(1),
        name="s5_mix",
    )(up3, m, p, q, a16, h0)
    return y.reshape(rows_all, W_A), fin


def _s5_ops_kernel(lr_ref, li_ref, ldt_ref, br_ref, bi_ref, cr_ref, ci_ref, m_ref, p_ref, qt_ref, a_ref):
    T = S5_CHUNK
    H = S5_H
    lam_re = lr_ref[0]
    lam_im = li_ref[0]
    dt = jnp.exp(ldt_ref[0])
    mag = jnp.exp(lam_re * dt)
    ang = lam_im * dt
    a_re = mag * jnp.cos(ang)
    a_im = mag * jnp.sin(ang)
    n_re = a_re - 1.0
    n_im = a_im
    den = lam_re * lam_re + lam_im * lam_im
    k_re = (n_re * lam_re + n_im * lam_im) / den
    k_im = (n_im * lam_re - n_re * lam_im) / den
    b_re = br_ref[0]
    b_im = bi_ref[0]
    bb_re = k_re * b_re - k_im * b_im
    bb_im = k_re * b_im + k_im * b_re
    c_re = cr_ref[0]
    c_im = ci_ref[0]
    ap = [(jnp.ones_like(a_re), jnp.zeros_like(a_re))]
    for _ in range(T):
        pr, pi = ap[-1]
        ap.append((pr * a_re - pi * a_im, pr * a_im + pi * a_re))
    is_fwd = lax.broadcasted_iota(jnp.int32, (1, LANES), 1) < S5_P

    def powers(kf, kb):
        return jnp.where(is_fwd, ap[kf][0], ap[kb][0]), jnp.where(is_fwd, ap[kf][1], ap[kb][1])

    for t in range(T):
        rows = slice(t * H, (t + 1) * H)
        er, ei = powers(T - 1 - t, t)
        p_ref[0, rows, :LANES] = (er * bb_re - ei * bb_im).astype(BF16)
        p_ref[0, rows, LANES:] = (er * bb_im + ei * bb_re).astype(BF16)
        er, ei = powers(t + 1, T - t)
        qt_ref[0, rows, :LANES] = (c_re * er - c_im * ei).astype(BF16)
        qt_ref[0, rows, LANES:] = (-(c_re * ei + c_im * er)).astype(BF16)
    a_ref[0, :, :LANES] = ap[T][0]
    a_ref[0, :, LANES:] = ap[T][1]

    def ca(k):
        return c_re * ap[k][0] - c_im * ap[k][1], c_re * ap[k][1] + c_im * ap[k][0]

    fwd = [ca(k) for k in range(T)]
    bwd = fwd[::-1]
    nt = lambda a, b: lax.dot_general(a, b, (((1,), (1,)), ((), ())), precision=HIGHEST,
                                      preferred_element_type=F32)
    cat = lambda parts, i: jnp.concatenate([p[i] for p in parts], axis=0)
    zero = jnp.zeros_like(bb_re)
    wf = (nt(jnp.where(is_fwd, bb_re, zero), cat(fwd, 0)) - nt(jnp.where(is_fwd, bb_im, zero), cat(fwd, 1)))
    wb = (nt(jnp.where(is_fwd, zero, bb_re), cat(bwd, 0)) - nt(jnp.where(is_fwd, zero, bb_im), cat(bwd, 1)))
    pad = jnp.zeros((H, S5_W), F32)
    wf_pad = jnp.concatenate([pad, wf], axis=1)
    wb_pad = jnp.concatenate([wb, pad], axis=1)
    for t in range(T):
        m_ref[0, t * H:(t + 1) * H, :] = (wf_pad[:, S5_W - H * t:2 * S5_W - H * t]
                                          + wb_pad[:, (T - 1 - t) * H:(T - 1 - t) * H + S5_W]).astype(BF16)


def _s5_operators(lam_re, lam_im, log_dt, b_re, b_im, c_re, c_im):
    G = S5_G
    dirs = lambda x: jnp.concatenate([x[0], x[1]], axis=-1)
    lam = [dirs(x)[:, None, :] for x in (lam_re, lam_im)]
    ldt = dirs(jnp.broadcast_to(log_dt[..., None], (2, G, S5_P)))[:, None, :]
    bt = [dirs(x.transpose(0, 1, 3, 2)) for x in (b_re, b_im)]
    ct = [dirs(x) for x in (c_re, c_im)]
    vec = pl.BlockSpec((1, 1, LANES), lambda g: (g, 0, 0))
    mat = pl.BlockSpec((1, S5_H, LANES), lambda g: (g, 0, 0))
    op = pl.BlockSpec((1, S5_W, S5_W), lambda g: (g, 0, 0))
    return pl.pallas_call(
        _s5_ops_kernel,
        grid=(G,),
        in_specs=[vec, vec, vec, mat, mat, mat, mat],
        out_specs=[op, op, op, pl.BlockSpec((1, 1, S5_W), lambda g: (g, 0, 0))],
        out_shape=[jax.ShapeDtypeStruct((G, S5_W, S5_W), BF16)] * 3
        + [jax.ShapeDtypeStruct((G, 1, S5_W), F32)],
        compiler_params=_cparams(1),
        name="s5_operators",
    )(*lam, ldt, *bt, *ct)


def _phase_tables(u, va, vb, ncb, n):
    def trig(m):
        m = m & (2 * n - 1)
        m = jnp.where(m >= n, m - 2 * n, m)
        ang = m.astype(F32) * (math.pi / n)
        return jnp.cos(ang), jnp.sin(ang)

    lane = lax.broadcasted_iota(jnp.int32, (1, LANES), 1)
    cb, sb = trig(u * (va * lane + vb))
    ca_all, sa_all = trig(u * (va * LANES * lane))
    out = []
    for ch in range(ncb):
        ca = ca_all[:, ch:ch + 1]
        sa = sa_all[:, ch:ch + 1]
        out.append((ca * cb - sa * sb, sa * cb + ca * sb))
    return out


def _tables_kernel(fm_ref, fmt_ref, gm_ref, blk_ref, *, L, kf):
    n = 2 * L
    kb = pl.program_id(0)
    u = 2 * (kb * kf + lax.broadcasted_iota(jnp.int32, (kf, 1), 0)) + 1
    for ch, (c, s) in enumerate(_phase_tables(u, 1, 0, L // LANES, n)):
        blk_ref[:kf, ch * LANES:(ch + 1) * LANES] = c
        blk_ref[kf:, ch * LANES:(ch + 1) * LANES] = -s
    blk = blk_ref[...]
    fm_ref[0] = blk.astype(BF16)
    fmt_ref[0] = blk.T.astype(BF16)
    for ch, (c, s) in enumerate(_phase_tables(u, 1, -(L // 2), L // LANES, n)):
        gm_ref[0, :kf, ch * LANES:(ch + 1) * LANES] = (c * (2.0 / n)).astype(BF16)
        gm_ref[0, kf:, ch * LANES:(ch + 1) * LANES] = (s * (-2.0 / n)).astype(BF16)


def _dft_tables(L, kf):
    KB = L // kf
    return pl.pallas_call(
        functools.partial(_tables_kernel, L=L, kf=kf),
        grid=(KB,),
        out_specs=[pl.BlockSpec((1, 2 * kf, L), lambda kb: (kb, 0, 0)),
                   pl.BlockSpec((1, L, 2 * kf), lambda kb: (kb, 0, 0)),
                   pl.BlockSpec((1, 2 * kf, L), lambda kb: (kb, 0, 0))],
        out_shape=[jax.ShapeDtypeStruct((KB, 2 * kf, L), BF16),
                   jax.ShapeDtypeStruct((KB, L, 2 * kf), BF16),
                   jax.ShapeDtypeStruct((KB, 2 * kf, L), BF16)],
        scratch_shapes=[pltpu.VMEM((2 * kf, L), F32)],
        compiler_params=_cparams(1),
        name="dft_tables",
    )()


def _spectrum_kernel(gm_ref, f_ref, o_ref):
    r = jnp.dot(gm_ref[0], f_ref[...], preferred_element_type=F32)
    o_ref[0, 0] = r[:, :W_B]
    o_ref[1, 0] = r[:, W_B:]


def _filter_spectrum(gm, filt):
    KB, kf2, L = gm.shape
    return pl.pallas_call(
        _spectrum_kernel,
        grid=(KB,),
        in_specs=[pl.BlockSpec((1, kf2, L), lambda kb: (kb, 0, 0)),
                  pl.BlockSpec((L, 2 * W_B), lambda kb: (0, 0))],
        out_specs=pl.BlockSpec((2, 1, kf2, W_B), lambda kb: (0, kb, 0, 0)),
        out_shape=jax.ShapeDtypeStruct((2, KB, kf2, W_B), F32),
        compiler_params=_cparams(1),
        name="filter_spectrum",
    )(gm, filt)


def _hyena_filters(L, w1, b1, w2, b2, freq, w3):
    t = jnp.arange(L, dtype=F32)
    tn = t / (L - 1)
    f = jnp.linspace(1e-4, HY_BANDS - 1, HY_BANDS, dtype=F32)
    ang = (2.0 * math.pi / L) * t[:, None] * f[None, :]
    z = jnp.concatenate([tn[:, None], jnp.cos(ang), -jnp.sin(ang)], axis=-1)
    h = jnp.sin(freq * (jnp.dot(z, w1, precision=HIGHEST) + b1))
    h = jnp.sin(freq * (jnp.dot(h, w2, precision=HIGHEST) + b2))
    h = jnp.dot(h, w3, precision=HIGHEST).reshape(L, 2, W_B)
    max_decay = math.log(HY_TARGET) / HY_FAST_PCT
    min_decay = math.log(HY_TARGET) / HY_SLOW_PCT
    deltas = jnp.abs(jnp.linspace(min_decay, max_decay, W_B, dtype=F32))
    half = L // 2
    off = jnp.abs(t - half) / half
    win = jnp.exp(-off[:, None] * deltas[None, :]) + HY_SHIFT
    return h * win[:, None, :]


def _shift_rows(u, s, row):
    L = u.shape[0]
    r = pltpu.roll(u, (-s) % L, 0)
    return jnp.where((row + s >= 0) & (row + s < L), r, 0.0)


def _hy_kernel(u_ref, cw_ref, cb_ref, hb_ref, fm_ref, fmt_ref, hf_ref, o_ref,
               zb_ref, zf_ref, acc_ref, *, L, kf, KB):
    o = pl.program_id(1)
    kb = pl.program_id(2)

    def short_conv():
        u = u_ref[...]
        w = cw_ref[...]
        row = lax.broadcasted_iota(jnp.int32, (L, 1), 0)
        return (cb_ref[...] + _shift_rows(u, -1, row) * w[0:1] + u * w[1:2]
                + _shift_rows(u, 1, row) * w[2:3])

    @pl.when((o == 0) & (kb == 0))
    def _():
        v = short_conv()
        zf_ref[...] = v
        zb_ref[...] = v.astype(BF16)

    @pl.when(kb == 0)
    def _():
        acc_ref[...] = jnp.zeros_like(acc_ref)

    uf = jnp.dot(fm_ref[0], zb_ref[...], preferred_element_type=F32)
    hf = hf_ref[0, 0]
    ur, ui = uf[:kf], uf[kf:]
    hr, hi = hf[:kf], hf[kf:]
    y = jnp.concatenate([ur * hr - ui * hi, ur * hi + ui * hr], axis=0).astype(BF16)
    acc_ref[...] += jnp.dot(fmt_ref[0], y, preferred_element_type=F32)

    @pl.when(kb == KB - 1)
    def _():
        z = short_conv() * (acc_ref[...] + zf_ref[...] * hb_ref[0])

        @pl.when(o == 0)
        def _():
            zf_ref[...] = z
            zb_ref[...] = z.astype(BF16)

        @pl.when(o == 1)
        def _():
            o_ref[...] = z


def _hyena_mix(ut, B, conv_w, conv_b, bias, fm, fmt, hf):
    L = ut.shape[0]
    KB, kf2, _ = fm.shape
    kf = kf2 // 2
    assert KB >= 2
    nblk = UT_COLS // W_B
    col = lambda o, kb: jnp.where((o == 0) & (kb == 0), 0, 1 + o)
    return pl.pallas_call(
        functools.partial(_hy_kernel, L=L, kf=kf, KB=KB),
        grid=(B, 2, KB),
        in_specs=[pl.BlockSpec((L, W_B), lambda b, o, kb: (0, b * nblk + col(o, kb))),
                  pl.BlockSpec((3, W_B), lambda b, o, kb: (0, col(o, kb))),
                  pl.BlockSpec((1, W_B), lambda b, o, kb: (0, col(o, kb))),
                  pl.BlockSpec((1, 1, W_B), lambda b, o, kb: (o, 0, 0)),
                  pl.BlockSpec((1, kf2, L), lambda b, o, kb: (kb, 0, 0)),
                  pl.BlockSpec((1, L, kf2), lambda b, o, kb: (kb, 0, 0)),
                  pl.BlockSpec((1, 1, kf2, W_B), lambda b, o, kb: (o, kb, 0, 0))],
        out_specs=pl.BlockSpec((L, W_B), lambda b, o, kb: (0, b)),
        out_shape=jax.ShapeDtypeStruct((L, B * W_B), F32),
        scratch_shapes=[pltpu.VMEM((L, W_B), BF16), pltpu.VMEM((L, W_B), F32), pltpu.VMEM((L, W_B), F32)],
        compiler_params=_cparams(3),
        name="hyena_mix",
    )(ut, conv_w, conv_b.reshape(1, -1), bias.reshape(2, 1, W_B), fm, fmt, hf)


def _pool_kernel(u_ref, w_ref, sc_ref, o_ref, *, L):
    row = lax.broadcasted_iota(jnp.int32, (L, 1), 0)
    for gi, win in enumerate(POOL_WINDOWS):
        u = u_ref[:, gi * POOL_G:(gi + 1) * POOL_G]
        s = u
        for off in range(-(win // 2), win - win // 2):
            if off != 0:
                s = s + _shift_rows(u, off, row)
        lo = jnp.maximum(row - win // 2, 0)
        hi = jnp.minimum(row - win // 2 + win, L)
        d = s / (hi - lo).astype(F32) - u
        o_ref[:, gi * POOL_G:(gi + 1) * POOL_G] = (
            jnp.dot(d.astype(BF16), w_ref[gi], preferred_element_type=F32)
            * sc_ref[:, gi * POOL_G:(gi + 1) * POOL_G])


def _pool_mix(ut, B, w, scale):
    L = ut.shape[0]
    nblk = UT_COLS // W_C
    return pl.pallas_call(
        functools.partial(_pool_kernel, L=L),
        grid=(B,),
        in_specs=[pl.BlockSpec((L, W_C), lambda b: (0, b * nblk + nblk - 1)),
                  pl.BlockSpec((len(POOL_WINDOWS), POOL_G, POOL_G), lambda b: (0, 0, 0)),
                  pl.BlockSpec((1, W_C), lambda b: (0, 0))],
        out_specs=pl.BlockSpec((L, W_C), lambda b: (0, b)),
        out_shape=jax.ShapeDtypeStruct((L, B * W_C), F32),
        compiler_params=_cparams(1),
        name="pool_mix",
    )(ut, w, scale.reshape(1, W_C))


def _silu(x):
    return x * jax.nn.sigmoid(x)


def _gelu_tanh(x):
    return x * (0.5 * (1.0 + jnp.tanh(math.sqrt(2.0 / math.pi) * (x + 0.044715 * (x * x * x)))))


def _out_kernel(*refs, has_pos, batch_major, final, B, bm):
    refs = list(refs)
    x_ref = refs.pop(0)
    pos_ref = refs.pop(0) if has_pos else None
    (mod_ref, g_ref, ua_ref, ya_ref, zb_ref, yc_ref, wga_ref, wgb_ref, wgc_ref, wma_ref, wmb_ref, wmc_ref,
     d_ref, wglu_ref, bglu_ref, wa_ref, wb_ref, wc_ref, wo_ref) = refs[:19]
    refs = refs[19:]
    fg_ref = refs.pop(0) if final else None
    xo_ref, zs_ref, ys_ref = refs[:3]
    xs_ref = refs[3] if (batch_major or final) else None
    for b in range(B):
        _put_batch(zs_ref, b, B, zb_ref[:, b * W_B:(b + 1) * W_B])
        _put_batch(ys_ref, b, B, yc_ref[:, b * W_C:(b + 1) * W_C])
    mod = mod_ref[...]
    x = _load_x(x_ref, xs_ref if batch_major else None, pos_ref, B, bm)
    h = _normed(x, mod, g_ref[...], bm)
    proj = lambda w_ref: jnp.dot(h, w_ref[0], preferred_element_type=F32)
    y = _gelu_tanh(ya_ref[...] + ua_ref[...] * d_ref[...])
    y = y * jax.nn.sigmoid(jnp.dot(y.astype(BF16), wglu_ref[...], preferred_element_type=F32) + bglu_ref[...])
    y_a = (y * _silu(proj(wga_ref))).astype(BF16)
    y_b = (_get_tile(zs_ref) * _silu(proj(wgb_ref))).astype(BF16)
    y_c = (_get_tile(ys_ref) * _silu(proj(wgc_ref))).astype(BF16)
    merged = (jax.nn.sigmoid(proj(wma_ref)) * jnp.dot(y_a, wa_ref[...], preferred_element_type=F32)
              + jax.nn.sigmoid(proj(wmb_ref)) * jnp.dot(y_b, wb_ref[...], preferred_element_type=F32)
              + jax.nn.sigmoid(proj(wmc_ref)) * jnp.dot(y_c, wc_ref[...], preferred_element_type=F32))
    out = jnp.dot(merged.astype(BF16), wo_ref[...], preferred_element_type=F32)
    x_new = x + _mod_part(mod, 2, bm) * _by_batch(out, bm)
    if final:
        r = lax.rsqrt(jnp.mean(x_new * x_new, axis=-1, keepdims=True) + EPS)
        _put_tile(xs_ref, (x_new * r * fg_ref[...]).reshape(-1, D_MODEL))
        for b in range(B):
            xo_ref[b] = _get_batch(xs_ref, b, B)
    else:
        xo_ref[...] = x_new.reshape(-1, D_MODEL)


def _out_proj(x, pos, mod, mod_blk, B, bm, g, ua, ya, zb, yc, w_in, l, d, w_glu, b_glu,
              w_a, w_b, w_c, w_o, final_g, tm):
    D = D_MODEL
    rows = x.size // D
    L = rows // B
    batch_major = x.ndim == 3
    final = final_g is not None
    tok = lambda w: pl.BlockSpec((tm, w), lambda i: (i, 0))
    seq = lambda w: pl.BlockSpec((tm // B, B * w), lambda i: (i, 0))
    full = lambda a: pl.BlockSpec(a.shape, lambda i: (0,) * a.ndim)
    in_specs, args = _x_specs(x, pos, mod, mod_blk, B, bm, tm)
    in_specs += [full(g), tok(W_A), tok(W_A), seq(W_B), seq(W_C)]
    args += [g, ua, ya, zb, yc]
    for cb in G_BLOCKS:
        in_specs.append(pl.BlockSpec((1, D, CB), lambda i, cb=cb: (l, 0, cb)))
        args.append(w_in)
    for k in range(3):
        in_specs.append(pl.BlockSpec((1, D, D), lambda i, k=k: (l, 0, M_BLOCK0 + k)))
        args.append(w_in)
    weights = [d, w_glu, b_glu, w_a, w_b, w_c, w_o]
    in_specs += [full(a) for a in weights]
    args += weights
    if final:
        in_specs.append(full(final_g))
        args.append(final_g)
    scratch = [_stage(tm, W_B), _stage(tm, W_C)]
    if batch_major or final:
        scratch.append(_stage(tm, D))
    if final:
        out_spec = pl.BlockSpec((B, tm // B, D), lambda i: (0, i, 0))
        out_shape = jax.ShapeDtypeStruct((B, L, D), F32)
    else:
        out_spec = tok(D)
        out_shape = jax.ShapeDtypeStruct((rows, D), F32)
    return pl.pallas_call(
        functools.partial(_out_kernel, has_pos=pos is not None, batch_major=batch_major, final=final,
                          B=B, bm=bm),
        grid=(rows // tm,),
        in_specs=in_specs,
        out_specs=out_spec,
        out_shape=out_shape,
        scratch_shapes=scratch,
        compiler_params=_cparams(1),
        name="out_proj",
    )(*args)


def _grid_pos_embed(L):
    rows = L // GRID_W
    r = jnp.broadcast_to(jnp.arange(rows, dtype=F32)[:, None], (rows, GRID_W)).reshape(-1)
    col = jnp.broadcast_to(jnp.arange(GRID_W, dtype=F32)[None, :], (rows, GRID_W)).reshape(-1)
    q = D_MODEL // 4
    omega = 1.0 / (10000.0 ** (jnp.arange(q, dtype=F32) / q))
    ar = r[:, None] * omega[None, :]
    ac = col[:, None] * omega[None, :]
    return jnp.concatenate([jnp.sin(ar), jnp.cos(ar), jnp.sin(ac), jnp.cos(ac)], axis=-1)


def _states_to_lanes(st_re, st_im):
    s = jnp.concatenate([st_re[:, 0], st_re[:, 1], st_im[:, 0], st_im[:, 1]], axis=-1)
    return s.transpose(1, 0, 2)


def _lanes_to_states(fin):
    f = fin.transpose(1, 0, 2).reshape(fin.shape[1], S5_G, 4, S5_P)
    return (jnp.stack([f[:, :, 0], f[:, :, 1]], axis=1), jnp.stack([f[:, :, 2], f[:, :, 3]], axis=1))


def kernel(x_prompt, x_sample, c, state_s5_re, state_s5_im, c_ctx, norm_g, w_mod, b_mod, w_in, s5_lam_re, s5_lam_im, s5_log_dt, s5_b_re, s5_b_im, s5_c_re, s5_c_im, s5_d, s5_w_glu, s5_b_glu, hy_conv_w, hy_conv_b, hy_f_w1, hy_f_b1, hy_f_w2, hy_f_b2, hy_f_freq, hy_f_w3, hy_bias, pool_w, pool_scale, w_br_a, w_br_b, w_br_c, w_out, final_g):
    Bc, Lc, D = x_prompt.shape
    Bl, Ll, _ = x_sample.shape
    assert Bl == MOD_ROWS - MOD_LAT0

    cond = jnp.zeros((MOD_ROWS, D), F32).at[0].set(c_ctx).at[MOD_LAT0:].set(c)
    mod = _modulation(cond, w_mod, b_mod).reshape(DEPTH * MOD_ROWS, 3 * D)

    pos = _grid_pos_embed(Ll)
    groups = {
        'ctx': dict(B=Bc, L=Lc, kf=min(128, Lc // 2), bm=1, mod_blk=0),
        'lat': dict(B=Bl, L=Ll, kf=min(256, Ll // 2), bm=Bl, mod_blk=1),
    }
    tables = {k: _dft_tables(v['L'], v['kf']) for k, v in groups.items()}
    w_in_b = w_in.astype(BF16)
    xs = {'ctx': x_prompt, 'lat': x_sample}
    new_re, new_im = [], []
    for l in range(DEPTH):
        s5_ops = _s5_operators(s5_lam_re[l], s5_lam_im[l], s5_log_dt[l], s5_b_re[l], s5_b_im[l],
                               s5_c_re[l], s5_c_im[l])
        g = norm_g[l].reshape(1, D)
        for name, cfg in groups.items():
            B, L, bm = cfg['B'], cfg['L'], cfg['bm']
            x = xs[name]
            p = pos if (name == 'lat' and l == 0) else None
            mod_blk = l * (MOD_ROWS // MOD_LAT0) + cfg['mod_blk']
            ua, ut = _in_proj(x, p, mod, mod_blk, B, bm, g, w_in_b, l, min(512, L * B))
            if name == 'ctx':
                h0g = jnp.zeros((S5_G, B, 4 * S5_P), F32)
            else:
                h0g = _states_to_lanes(state_s5_re[:, l], state_s5_im[:, l])
            ya, fin = _s5_mix(ua, s5_ops, h0g, B)
            if name == 'ctx':
                fr, fi = _lanes_to_states(fin)
                new_re.append(fr)
                new_im.append(fi)
            fm, fmt, gm = tables[name]
            filt = _hyena_filters(L, hy_f_w1[l], hy_f_b1[l], hy_f_w2[l], hy_f_b2[l], hy_f_freq[l], hy_f_w3[l])
            hf = _filter_spectrum(gm, filt.reshape(L, 2 * W_B).astype(BF16))
            zb = _hyena_mix(ut, B, hy_conv_w[l], hy_conv_b[l], hy_bias[l], fm, fmt, hf)
            yc = _pool_mix(ut, B, pool_w[l].astype(BF16), pool_scale[l])
            fg = final_g.reshape(1, D) if l == DEPTH - 1 else None
            xs[name] = _out_proj(x, p, mod, mod_blk, B, bm, g, ua, ya, zb, yc, w_in_b, l,
                                 s5_d[l].reshape(1, W_A), s5_w_glu[l].astype(BF16), s5_b_glu[l].reshape(1, W_A),
                                 w_br_a[l].astype(BF16), w_br_b[l].astype(BF16), w_br_c[l].astype(BF16),
                                 w_out[l].astype(BF16), fg, min(256, L * B))
    return (xs['ctx'], xs['lat'], jnp.stack(new_re, axis=1), jnp.stack(new_im, axis=1))
```

```python
import functools
import math

import jax
import jax.numpy as jnp
from jax import lax
from jax.experimental import pallas as pl
from jax.experimental.pallas import tpu as pltpu

F32 = jnp.float32
BF16 = jnp.bfloat16
HIGHEST = lax.Precision.HIGHEST

D_MODEL = 1024
DEPTH = 2
GRID_W = 64
EPS = 1e-6
W_A = D_MODEL // 2
S5_H = 16
S5_G = W_A // S5_H
S5_P = 64
W_B = D_MODEL // 2
HY_BANDS = 16
HY_FAST_PCT = 0.3
HY_SLOW_PCT = 1.5
HY_TARGET = 1e-2
HY_SHIFT = 0.05
W_C = D_MODEL // 2
POOL_WINDOWS = (2, 4, 8, 16)
POOL_G = W_C // 4
LANES = 128
CB = 512
U_BLOCKS = (0, 2, 3, 4, 6)
G_BLOCKS = (1, 5, 7)
M_BLOCK0 = 4
UT_COLS = CB * (len(U_BLOCKS) - 1)
S5_CHUNK = 16
S5_W = S5_CHUNK * S5_H
S5_GB = LANES // S5_H
MOD_ROWS = 16
MOD_LAT0 = 8
VMEM_LIMIT = 56 * 1024 * 1024


def _cparams(n_grid):
    return pltpu.CompilerParams(dimension_semantics=("arbitrary",) * n_grid,
                                vmem_limit_bytes=VMEM_LIMIT)


def _mod_kernel(c_ref, w_ref, b_ref, o_ref):
    c = c_ref[...]
    s = c * jax.nn.sigmoid(c)
    o_ref[0] = jnp.dot(s, w_ref[0], preferred_element_type=F32, precision=HIGHEST) + b_ref[0]


def _modulation(cond, w_mod, b_mod, tn=512):
    n = 3 * D_MODEL
    return pl.pallas_call(
        _mod_kernel,
        grid=(DEPTH, n // tn),
        in_specs=[pl.BlockSpec((MOD_ROWS, D_MODEL), lambda l, j: (0, 0)),
                  pl.BlockSpec((1, D_MODEL, tn), lambda l, j: (l, 0, j)),
                  pl.BlockSpec((1, 1, tn), lambda l, j: (l, 0, j))],
        out_specs=pl.BlockSpec((1, MOD_ROWS, tn), lambda l, j: (l, 0, j)),
        out_shape=jax.ShapeDtypeStruct((DEPTH, MOD_ROWS, n), F32),
        compiler_params=_cparams(2),
        name="modulation",
    )(cond, w_mod, b_mod.reshape(DEPTH, 1, n))


def _by_batch(x, bm):
    return x if bm == 1 else x.reshape(x.shape[0] // bm, bm, x.shape[1])


def _mod_part(mod, i, bm):
    m = mod[:bm, i * D_MODEL:(i + 1) * D_MODEL]
    return m if bm == 1 else m[None]


def _stage(rows, w):
    return pltpu.VMEM((w // LANES, rows, LANES), F32)


def _put_batch(ref, b, B, val):
    for c in range(ref.shape[0]):
        ref[c, pl.ds(b, val.shape[0], stride=B), :] = val[:, c * LANES:(c + 1) * LANES]


def _get_batch(ref, b, B):
    tt = ref.shape[1] // B
    return jnp.concatenate([ref[c, pl.ds(b, tt, stride=B), :] for c in range(ref.shape[0])], axis=1)


def _put_tile(ref, val):
    for c in range(ref.shape[0]):
        ref[c] = val[:, c * LANES:(c + 1) * LANES]


def _get_tile(ref):
    return jnp.concatenate([ref[c] for c in range(ref.shape[0])], axis=1)


def _load_x(x_ref, xs_ref, pos_ref, B, bm):
    if xs_ref is not None:
        for b in range(B):
            _put_batch(xs_ref, b, B, x_ref[b])
        x2 = _get_tile(xs_ref)
    else:
        x2 = x_ref[...]
    x = _by_batch(x2, bm)
    if pos_ref is not None:
        x = x + pos_ref[...][:, None, :]
    return x


def _normed(x, mod, g, bm):
    r = lax.rsqrt(jnp.mean(x * x, axis=-1, keepdims=True) + EPS)
    h = (x * r * g) * (1.0 + _mod_part(mod, 1, bm)) + _mod_part(mod, 0, bm)
    return h.reshape(-1, D_MODEL).astype(BF16)


def _in_kernel(*refs, has_pos, batch_major, B, bm):
    refs = list(refs)
    x_ref = refs.pop(0)
    pos_ref = refs.pop(0) if has_pos else None
    mod_ref, g_ref = refs[:2]
    w_refs = refs[2:2 + len(U_BLOCKS)]
    oa_ref, ot_ref, scr_ref = refs[2 + len(U_BLOCKS):5 + len(U_BLOCKS)]
    xs_ref = refs[-1] if batch_major else None
    h = _normed(_load_x(x_ref, xs_ref, pos_ref, B, bm), mod_ref[...], g_ref[...], bm)
    oa_ref[...] = jnp.dot(h, w_refs[0][0], preferred_element_type=F32)
    for i, w_ref in enumerate(w_refs[1:]):
        _put_tile(scr_ref, jnp.dot(h, w_ref[0], preferred_element_type=F32))
        for b in range(B):
            ot_ref[:, b * UT_COLS + i * CB:b * UT_COLS + (i + 1) * CB] = _get_batch(scr_ref, b, B)


def _x_specs(x, pos, mod, mod_blk, B, bm, tm):
    D = D_MODEL
    if x.ndim == 3:
        specs = [pl.BlockSpec((B, tm // B, D), lambda i: (0, i, 0))]
    else:
        specs = [pl.BlockSpec((tm, D), lambda i: (i, 0))]
    args = [x]
    if pos is not None:
        specs.append(pl.BlockSpec((tm // bm, D), lambda i: (i, 0)))
        args.append(pos)
    specs.append(pl.BlockSpec((MOD_LAT0, 3 * D), lambda i: (mod_blk, 0)))
    args.append(mod)
    return specs, args


def _in_proj(x, pos, mod, mod_blk, B, bm, g, w_in, l, tm):
    D = D_MODEL
    rows = x.size // D
    L = rows // B
    batch_major = x.ndim == 3
    in_specs, args = _x_specs(x, pos, mod, mod_blk, B, bm, tm)
    in_specs.append(pl.BlockSpec((1, D), lambda i: (0, 0)))
    args.append(g)
    for cb in U_BLOCKS:
        in_specs.append(pl.BlockSpec((1, D, CB), lambda i, cb=cb: (l, 0, cb)))
        args.append(w_in)
    scratch = [_stage(tm, CB)] + ([_stage(tm, D)] if batch_major else [])
    return pl.pallas_call(
        functools.partial(_in_kernel, has_pos=pos is not None, batch_major=batch_major, B=B, bm=bm),
        grid=(rows // tm,),
        in_specs=in_specs,
        out_specs=[pl.BlockSpec((tm, W_A), lambda i: (i, 0)),
                   pl.BlockSpec((tm // B, B * UT_COLS), lambda i: (i, 0))],
        out_shape=[jax.ShapeDtypeStruct((rows, W_A), F32),
                   jax.ShapeDtypeStruct((L, B * UT_COLS), F32)],
        scratch_shapes=scratch,
        compiler_params=_cparams(1),
        name="in_proj",
    )(*args)


S5_RB = 64
S5_GS = 4


def _s5_kernel(u_ref, m_ref, p_ref, qt_ref, a_ref, h0_ref, y_ref, fin_ref,
               x_ref, yall_ref, sloc_ref, sinf_ref, sinb_ref, *, nc, B):
    rows = nc * B
    cpb = S5_RB // B
    half = S5_W // 2
    lane_grp = lax.broadcasted_iota(jnp.int32, (S5_RB, LANES), 1) // S5_H

    lane = lax.broadcasted_iota(jnp.int32, (B, half), 1)
    is_fwd = lane < S5_P
    lane_full = lax.broadcasted_iota(jnp.int32, (rows, S5_W), 1)
    is_fwd_full = (lane_full % half) < S5_P

    def gather_step(rb, _, g0):
        c0 = pl.multiple_of(rb * cpb, cpb)
        r0 = pl.multiple_of(rb * S5_RB, S5_RB)
        slabs = [u_ref[pl.ds(c0, cpb), t * B:(t + 1) * B, :].reshape(S5_RB, LANES) for t in range(S5_CHUNK)]
        for gi in range(S5_GS):
            g8 = g0 + gi
            for hf in range(2):
                acc = None
                for k in range(S5_GB):
                    s = slabs[hf * S5_GB + k]
                    r = s if k == g8 else pltpu.roll(s, ((k - g8) * S5_H) % LANES, 1)
                    acc = r if acc is None else jnp.where(lane_grp == k, r, acc)
                x_ref[gi, pl.ds(r0, S5_RB), hf * LANES:(hf + 1) * LANES] = acc.astype(BF16)
        return 0

    def group_step(gi, _, g0):
        g8 = g0 + gi
        u = x_ref[gi]
        sloc_ref[...] = jnp.dot(u, p_ref[g8], preferred_element_type=F32)
        a = a_ref[g8]
        ar = a[:, :half]
        ai = a[:, half:]

        def body(i, carry):
            cr, ci = carry
            jf = pl.multiple_of(i * B, B)
            jb = pl.multiple_of((nc - 1 - i) * B, B)
            sinf_ref[pl.ds(jf, B), :half] = cr
            sinf_ref[pl.ds(jf, B), half:] = ci
            sinb_ref[pl.ds(jb, B), :half] = cr
            sinb_ref[pl.ds(jb, B), half:] = ci
            lf = sloc_ref[pl.ds(jf, B), :]
            lb = sloc_ref[pl.ds(jb, B), :]
            lr = jnp.where(is_fwd, lf[:, :half], lb[:, :half])
            li = jnp.where(is_fwd, lf[:, half:], lb[:, half:])
            return (ar * cr - ai * ci + lr, ar * ci + ai * cr + li)

        h0 = h0_ref[g8]
        fr, fi = lax.fori_loop(0, nc, body, (h0[:, :half], h0[:, half:]))
        fin_ref[g8, :, :half] = fr
        fin_ref[g8, :, half:] = fi
        s_in = jnp.where(is_fwd_full, sinf_ref[...], sinb_ref[...]).astype(BF16)
        yall_ref[gi] = (jnp.dot(u, m_ref[g8], preferred_element_type=F32)
                        + lax.dot_general(s_in, qt_ref[g8], (((1,), (1,)), ((), ())),
                                          preferred_element_type=F32))
        return 0

    def scatter_step(rb, _, g0):
        c0 = pl.multiple_of(rb * cpb, cpb)
        r0 = pl.multiple_of(rb * S5_RB, S5_RB)
        for hf in range(2):
            ys = [yall_ref[gi, pl.ds(r0, S5_RB), hf * LANES:(hf + 1) * LANES] for gi in range(S5_GS)]
            for k in range(S5_GB):
                acc = None
                for gi in range(S5_GS):
                    g8 = g0 + gi
                    r = ys[gi] if k == g8 else pltpu.roll(ys[gi], ((g8 - k) * S5_H) % LANES, 1)
                    acc = r if acc is None else jnp.where(lane_grp == g8, r, acc)
                t = hf * S5_GB + k
                dst = (pl.ds(c0, cpb), slice(t * B, (t + 1) * B), slice(None))
                if g0 > 0:
                    acc = jnp.where(lane_grp >= g0, acc, y_ref[dst].reshape(S5_RB, LANES))
                y_ref[dst] = acc.reshape(cpb, B, LANES)
        return 0

    for g0 in range(0, S5_GB, S5_GS):
        lax.fori_loop(0, rows // S5_RB, functools.partial(gather_step, g0=g0), 0)
        lax.fori_loop(0, S5_GS, functools.partial(group_step, g0=g0), 0)
        lax.fori_loop(0, rows // S5_RB, functools.partial(scatter_step, g0=g0), 0)


def _s5_mix(ua, ops, h0, B):
    m, p, q, a16 = ops
    rows_all = ua.shape[0]
    nc = rows_all // (S5_CHUNK * B)
    rows = nc * B
    up3 = ua.reshape(nc, S5_CHUNK * B, W_A)
    gblk = lambda r: pl.BlockSpec((S5_GB, r, S5_W), lambda j: (j, 0, 0))
    tok = pl.BlockSpec((nc, S5_CHUNK * B, LANES), lambda j: (0, 0, j))
    y, fin = pl.pallas_call(
        functools.partial(_s5_kernel, nc=nc, B=B),
        grid=(W_A // LANES,),
        in_specs=[tok, gblk(S5_W), gblk(S5_W), gblk(S5_W), gblk(1), gblk(B)],
        out_specs=[tok, gblk(B)],
        out_shape=[jax.ShapeDtypeStruct((nc, S5_CHUNK * B, W_A), F32),
                   jax.ShapeDtypeStruct((S5_G, B, S5_W), F32)],
        scratch_shapes=[pltpu.VMEM((S5_GS, rows, S5_W), BF16), pltpu.VMEM((S5_GS, rows, S5_W), F32),
                        pltpu.VMEM((rows, S5_W), F32), pltpu.VMEM((rows, S5_W), F32),
                        pltpu.VMEM((rows, S5_W), F32)],
        compiler_params=_cparams(1),
        name="s5_mix",
    )(up3, m, p, q, a16, h0)
    return y.reshape(rows_all, W_A), fin


def _s5_ops_kernel(lr_ref, li_ref, ldt_ref, br_ref, bi_ref, cr_ref, ci_ref, m_ref, p_ref, qt_ref, a_ref):
    T = S5_CHUNK
    H = S5_H
    lam_re = lr_ref[0]
    lam_im = li_ref[0]
    dt = jnp.exp(ldt_ref[0])
    mag = jnp.exp(lam_re * dt)
    ang = lam_im * dt
    a_re = mag * jnp.cos(ang)
    a_im = mag * jnp.sin(ang)
    n_re = a_re - 1.0
    n_im = a_im
    den = lam_re * lam_re + lam_im * lam_im
    k_re = (n_re * lam_re + n_im * lam_im) / den
    k_im = (n_im * lam_re - n_re * lam_im) / den
    b_re = br_ref[0]
    b_im = bi_ref[0]
    bb_re = k_re * b_re - k_im * b_im
    bb_im = k_re * b_im + k_im * b_re
    c_re = cr_ref[0]
    c_im = ci_ref[0]
    ap = [(jnp.ones_like(a_re), jnp.zeros_like(a_re))]
    for _ in range(T):
        pr, pi = ap[-1]
        ap.append((pr * a_re - pi * a_im, pr * a_im + pi * a_re))
    is_fwd = lax.broadcasted_iota(jnp.int32, (1, LANES), 1) < S5_P

    def powers(kf, kb):
        return jnp.where(is_fwd, ap[kf][0], ap[kb][0]), jnp.where(is_fwd, ap[kf][1], ap[kb][1])

    for t in range(T):
        rows = slice(t * H, (t + 1) * H)
        er, ei = powers(T - 1 - t, t)
        p_ref[0, rows, :LANES] = (er * bb_re - ei * bb_im).astype(BF16)
        p_ref[0, rows, LANES:] = (er * bb_im + ei * bb_re).astype(BF16)
        er, ei = powers(t + 1, T - t)
        qt_ref[0, rows, :LANES] = (c_re * er - c_im * ei).astype(BF16)
        qt_ref[0, rows, LANES:] = (-(c_re * ei + c_im * er)).astype(BF16)
    a_ref[0, :, :LANES] = ap[T][0]
    a_ref[0, :, LANES:] = ap[T][1]

    def ca(k):
        return c_re * ap[k][0] - c_im * ap[k][1], c_re * ap[k][1] + c_im * ap[k][0]

    fwd = [ca(k) for k in range(T)]
    bwd = fwd[::-1]
    nt = lambda a, b: lax.dot_general(a, b, (((1,), (1,)), ((), ())), precision=HIGHEST,
                                      preferred_element_type=F32)
    cat = lambda parts, i: jnp.concatenate([p[i] for p in parts], axis=0)
    zero = jnp.zeros_like(bb_re)
    wf = (nt(jnp.where(is_fwd, bb_re, zero), cat(fwd, 0)) - nt(jnp.where(is_fwd, bb_im, zero), cat(fwd, 1)))
    wb = (nt(jnp.where(is_fwd, zero, bb_re), cat(bwd, 0)) - nt(jnp.where(is_fwd, zero, bb_im), cat(bwd, 1)))
    pad = jnp.zeros((H, S5_W), F32)
    wf_pad = jnp.concatenate([pad, wf], axis=1)
    wb_pad = jnp.concatenate([wb, pad], axis=1)
    for t in range(T):
        m_ref[0, t * H:(t + 1) * H, :] = (wf_pad[:, S5_W - H * t:2 * S5_W - H * t]
                                          + wb_pad[:, (T - 1 - t) * H:(T - 1 - t) * H + S5_W]).astype(BF16)


def _s5_operators(lam_re, lam_im, log_dt, b_re, b_im, c_re, c_im):
    G = S5_G
    dirs = lambda x: jnp.concatenate([x[0], x[1]], axis=-1)
    lam = [dirs(x)[:, None, :] for x in (lam_re, lam_im)]
    ldt = dirs(jnp.broadcast_to(log_dt[..., None], (2, G, S5_P)))[:, None, :]
    bt = [dirs(x.transpose(0, 1, 3, 2)) for x in (b_re, b_im)]
    ct = [dirs(x) for x in (c_re, c_im)]
    vec = pl.BlockSpec((1, 1, LANES), lambda g: (g, 0, 0))
    mat = pl.BlockSpec((1, S5_H, LANES), lambda g: (g, 0, 0))
    op = pl.BlockSpec((1, S5_W, S5_W), lambda g: (g, 0, 0))
    return pl.pallas_call(
        _s5_ops_kernel,
        grid=(G,),
        in_specs=[vec, vec, vec, mat, mat, mat, mat],
        out_specs=[op, op, op, pl.BlockSpec((1, 1, S5_W), lambda g: (g, 0, 0))],
        out_shape=[jax.ShapeDtypeStruct((G, S5_W, S5_W), BF16)] * 3
        + [jax.ShapeDtypeStruct((G, 1, S5_W), F32)],
        compiler_params=_cparams(1),
        name="s5_operators",
    )(*lam, ldt, *bt, *ct)


def _phase_tables(u, va, vb, ncb, n):
    def trig(m):
        m = m & (2 * n - 1)
        m = jnp.where(m >= n, m - 2 * n, m)
        ang = m.astype(F32) * (math.pi / n)
        return jnp.cos(ang), jnp.sin(ang)

    lane = lax.broadcasted_iota(jnp.int32, (1, LANES), 1)
    cb, sb = trig(u * (va * lane + vb))
    ca_all, sa_all = trig(u * (va * LANES * lane))
    out = []
    for ch in range(ncb):
        ca = ca_all[:, ch:ch + 1]
        sa = sa_all[:, ch:ch + 1]
        out.append((ca * cb - sa * sb, sa * cb + ca * sb))
    return out


def _tables_kernel(fs_ref, fst_ref, blk_ref, *, Ls):
    u = 2 * lax.broadcasted_iota(jnp.int32, (Ls, 1), 0) + 1
    for ch, (c, s) in enumerate(_phase_tables(u, 1, 0, Ls // LANES, 2 * Ls)):
        blk_ref[:Ls, ch * LANES:(ch + 1) * LANES] = c
        blk_ref[Ls:, ch * LANES:(ch + 1) * LANES] = -s
    blk = blk_ref[...]
    fs_ref[...] = blk.astype(BF16)
    fst_ref[...] = blk.T.astype(BF16)


def _dft_tables(Ls):
    return pl.pallas_call(
        functools.partial(_tables_kernel, Ls=Ls),
        out_shape=[jax.ShapeDtypeStruct((2 * Ls, Ls), BF16), jax.ShapeDtypeStruct((Ls, 2 * Ls), BF16)],
        scratch_shapes=[pltpu.VMEM((2 * Ls, Ls), F32)],
        compiler_params=pltpu.CompilerParams(vmem_limit_bytes=VMEM_LIMIT),
        name="dft_tables",
    )()


def _spectrum_kernel(fs_ref, f_ref, w_ref, *, S, Ls):
    scale = 1.0 / Ls
    k = lax.broadcasted_iota(jnp.int32, (Ls, 1), 0)
    sgn = jnp.where(k % 2 == 0, 1.0, -1.0)
    prev = None
    for i in range(S + 1):
        re = im = None
        if i < S:
            g = jnp.dot(fs_ref[...], f_ref[i * Ls:(i + 1) * Ls, :], preferred_element_type=F32) * scale
            re, im = g[:Ls], g[Ls:]
        cur = (re, im)
        if prev is not None:
            jre, jim = -sgn * prev[1], sgn * prev[0]
            re = jre if re is None else re + jre
            im = jim if im is None else im + jim
        w_ref[0, i, :Ls] = re
        w_ref[0, i, Ls:] = im
        prev = cur


def _filter_spectrum(fs, filt, S, C):
    n2, Ls = fs.shape
    L = filt.shape[0]
    nct = W_B // C
    return pl.pallas_call(
        functools.partial(_spectrum_kernel, S=S, Ls=Ls),
        grid=(2, nct),
        in_specs=[pl.BlockSpec((n2, Ls), lambda o, ct: (0, 0)),
                  pl.BlockSpec((L, C), lambda o, ct: (0, o * nct + ct))],
        out_specs=pl.BlockSpec((1, S + 1, n2, C), lambda o, ct: (o, 0, 0, ct)),
        out_shape=jax.ShapeDtypeStruct((2, S + 1, n2, W_B), F32),
        compiler_params=_cparams(2),
        name="filter_spectrum",
    )(fs, filt)


def _hyena_filters(L, w1, b1, w2, b2, freq, w3):
    t = jnp.arange(L, dtype=F32)
    tn = t / (L - 1)
    f = jnp.linspace(1e-4, HY_BANDS - 1, HY_BANDS, dtype=F32)
    ang = (2.0 * math.pi / L) * t[:, None] * f[None, :]
    z = jnp.concatenate([tn[:, None], jnp.cos(ang), -jnp.sin(ang)], axis=-1)
    h = jnp.sin(freq * (jnp.dot(z, w1, precision=HIGHEST) + b1))
    h = jnp.sin(freq * (jnp.dot(h, w2, precision=HIGHEST) + b2))
    h = jnp.dot(h, w3, precision=HIGHEST).reshape(L, 2, W_B)
    max_decay = math.log(HY_TARGET) / HY_FAST_PCT
    min_decay = math.log(HY_TARGET) / HY_SLOW_PCT
    deltas = jnp.abs(jnp.linspace(min_decay, max_decay, W_B, dtype=F32))
    half = L // 2
    off = jnp.abs(t - half) / half
    win = jnp.exp(-off[:, None] * deltas[None, :]) + HY_SHIFT
    return h * win[:, None, :]


def _shift_rows(u, s, row):
    L = u.shape[0]
    r = pltpu.roll(u, (-s) % L, 0)
    return jnp.where((row + s >= 0) & (row + s < L), r, 0.0)


HY_RF = 32
SUBLANES = 8


def _short_conv_rows(u_ref, w, bias, c, R, L):
    r0 = pl.multiple_of(c * R, R)
    cur = u_ref[pl.ds(r0, R), :]
    before = u_ref[pl.ds(pl.multiple_of(jnp.maximum(r0 - SUBLANES, 0), SUBLANES), SUBLANES), :]
    after = u_ref[pl.ds(pl.multiple_of(jnp.minimum(r0 + R, L - SUBLANES), SUBLANES), SUBLANES), :]
    before = jnp.where(c == 0, 0.0, before)
    after = jnp.where(c == L // R - 1, 0.0, after)
    ext = jnp.concatenate([before, cur, after], axis=0)
    n = R + 2 * SUBLANES
    up = pltpu.roll(ext, 1, 0)[SUBLANES:SUBLANES + R]
    dn = pltpu.roll(ext, n - 1, 0)[SUBLANES:SUBLANES + R]
    return bias + up * w[0:1] + cur * w[1:2] + dn * w[2:3]


def _hy_kernel(v_ref, x_ref, cwv_ref, cbv_ref, cwx_ref, cbx_ref, hb_ref, fs_ref, fst_ref, w_ref, o_ref,
               zf_ref, zb_ref, u_ref, d_ref, os_ref, *, L, S, R):
    Ls = L // S
    o = pl.program_id(2)

    @pl.when(o == 0)
    def _():
        def body(c, _):
            rows = pl.ds(pl.multiple_of(c * R, R), R)
            z = _short_conv_rows(v_ref, cwv_ref[...], cbv_ref[...], c, R, L)
            zf_ref[rows, :] = z
            zb_ref[rows, :] = z.astype(BF16)
            return 0
        lax.fori_loop(0, L // R, body, 0)

    for j in range(S):
        u_ref[j] = jnp.dot(fs_ref[...], zb_ref[j * Ls:(j + 1) * Ls, :], preferred_element_type=F32)

    for r in range(S):
        def prod(fc, _, r=r):
            f0 = pl.multiple_of(fc * HY_RF, HY_RF)
            re_rows = pl.ds(f0, HY_RF)
            im_rows = pl.ds(Ls + f0, HY_RF)
            dre = dim = None
            for j in range(S):
                d = r - j + S // 2
                if 0 <= d <= S:
                    ur, ui = u_ref[j, re_rows, :], u_ref[j, im_rows, :]
                    wr, wi = w_ref[0, d, re_rows, :], w_ref[0, d, im_rows, :]
                    pre, pim = ur * wr - ui * wi, ur * wi + ui * wr
                    dre = pre if dre is None else dre + pre
                    dim = pim if dim is None else dim + pim
            d_ref[r, re_rows, :] = dre.astype(BF16)
            d_ref[r, im_rows, :] = dim.astype(BF16)
            return 0
        lax.fori_loop(0, Ls // HY_RF, prod, 0)

    for r in range(S):
        os_ref[...] = jnp.dot(fst_ref[...], d_ref[r], preferred_element_type=F32)

        def gate(cc, _, r=r):
            c = r * (Ls // R) + cc
            rows = pl.ds(pl.multiple_of(c * R, R), R)
            x = _short_conv_rows(x_ref, cwx_ref[...], cbx_ref[...], c, R, L)
            z = x * (os_ref[pl.ds(pl.multiple_of(cc * R, R), R), :] + zf_ref[rows, :] * hb_ref[0])

            @pl.when(o == 0)
            def _():
                zf_ref[rows, :] = z
                zb_ref[rows, :] = z.astype(BF16)

            @pl.when(o == 1)
            def _():
                o_ref[rows, :] = z
            return 0
        lax.fori_loop(0, Ls // R, gate, 0)


def _hyena_mix(ut, B, conv_w, conv_b, bias, fs, fst, w, C):
    L = ut.shape[0]
    n2, Ls = fs.shape
    S = L // Ls
    nct = W_B // C
    ncol = UT_COLS // C
    xcol = lambda o, ct: (1 + o) * nct + ct
    conv_b = conv_b.reshape(1, -1)
    R = min(128, Ls)
    return pl.pallas_call(
        functools.partial(_hy_kernel, L=L, S=S, R=R),
        grid=(B, nct, 2),
        in_specs=[pl.BlockSpec((L, C), lambda b, ct, o: (0, b * ncol + ct)),
                  pl.BlockSpec((L, C), lambda b, ct, o: (0, b * ncol + xcol(o, ct))),
                  pl.BlockSpec((3, C), lambda b, ct, o: (0, ct)),
                  pl.BlockSpec((1, C), lambda b, ct, o: (0, ct)),
                  pl.BlockSpec((3, C), lambda b, ct, o: (0, xcol(o, ct))),
                  pl.BlockSpec((1, C), lambda b, ct, o: (0, xcol(o, ct))),
                  pl.BlockSpec((1, 1, C), lambda b, ct, o: (o, 0, ct)),
                  pl.BlockSpec((n2, Ls), lambda b, ct, o: (0, 0)),
                  pl.BlockSpec((Ls, n2), lambda b, ct, o: (0, 0)),
                  pl.BlockSpec((1, S + 1, n2, C), lambda b, ct, o: (o, 0, 0, ct))],
        out_specs=pl.BlockSpec((L, C), lambda b, ct, o: (0, b * nct + ct)),
        out_shape=jax.ShapeDtypeStruct((L, B * W_B), F32),
        scratch_shapes=[pltpu.VMEM((L, C), F32), pltpu.VMEM((L, C), BF16), pltpu.VMEM((S, n2, C), F32),
                        pltpu.VMEM((S, n2, C), BF16), pltpu.VMEM((Ls, C), F32)],
        compiler_params=_cparams(3),
        name="hyena_mix",
    )(ut, ut, conv_w, conv_b, conv_w, conv_b, bias.reshape(2, 1, W_B), fs, fst, w)


def _pool_kernel(u_ref, w_ref, sc_ref, o_ref, *, L):
    row = lax.broadcasted_iota(jnp.int32, (L, 1), 0)
    for gi, win in enumerate(POOL_WINDOWS):
        u = u_ref[:, gi * POOL_G:(gi + 1) * POOL_G]
        s = u
        for off in range(-(win // 2), win - win // 2):
            if off != 0:
                s = s + _shift_rows(u, off, row)
        lo = jnp.maximum(row - win // 2, 0)
        hi = jnp.minimum(row - win // 2 + win, L)
        d = s / (hi - lo).astype(F32) - u
        o_ref[:, gi * POOL_G:(gi + 1) * POOL_G] = (
            jnp.dot(d.astype(BF16), w_ref[gi], preferred_element_type=F32)
            * sc_ref[:, gi * POOL_G:(gi + 1) * POOL_G])


def _pool_mix(ut, B, w, scale):
    L = ut.shape[0]
    nblk = UT_COLS // W_C
    return pl.pallas_call(
        functools.partial(_pool_kernel, L=L),
        grid=(B,),
        in_specs=[pl.BlockSpec((L, W_C), lambda b: (0, b * nblk + nblk - 1)),
                  pl.BlockSpec((len(POOL_WINDOWS), POOL_G, POOL_G), lambda b: (0, 0, 0)),
                  pl.BlockSpec((1, W_C), lambda b: (0, 0))],
        out_specs=pl.BlockSpec((L, W_C), lambda b: (0, b)),
        out_shape=jax.ShapeDtypeStruct((L, B * W_C), F32),
        compiler_params=_cparams(1),
        name="pool_mix",
    )(ut, w, scale.reshape(1, W_C))


def _silu(x):
    return x * jax.nn.sigmoid(x)


def _gelu_tanh(x):
    return x * (0.5 * (1.0 + jnp.tanh(math.sqrt(2.0 / math.pi) * (x + 0.044715 * (x * x * x)))))


def _out_kernel(*refs, has_pos, batch_major, final, B, bm):
    refs = list(refs)
    x_ref = refs.pop(0)
    pos_ref = refs.pop(0) if has_pos else None
    (mod_ref, g_ref, ua_ref, ya_ref, zb_ref, yc_ref, wga_ref, wgb_ref, wgc_ref, wma_ref, wmb_ref, wmc_ref,
     d_ref, wglu_ref, bglu_ref, wa_ref, wb_ref, wc_ref, wo_ref) = refs[:19]
    refs = refs[19:]
    fg_ref = refs.pop(0) if final else None
    xo_ref, zs_ref, ys_ref = refs[:3]
    xs_ref = refs[3] if (batch_major or final) else None
    for b in range(B):
        _put_batch(zs_ref, b, B, zb_ref[:, b * W_B:(b + 1) * W_B])
        _put_batch(ys_ref, b, B, yc_ref[:, b * W_C:(b + 1) * W_C])
    mod = mod_ref[...]
    x = _load_x(x_ref, xs_ref if batch_major else None, pos_ref, B, bm)
    h = _normed(x, mod, g_ref[...], bm)
    proj = lambda w_ref: jnp.dot(h, w_ref[0], preferred_element_type=F32)
    y = _gelu_tanh(ya_ref[...] + ua_ref[...] * d_ref[...])
    y = y * jax.nn.sigmoid(jnp.dot(y.astype(BF16), wglu_ref[...], preferred_element_type=F32) + bglu_ref[...])
    y_a = (y * _silu(proj(wga_ref))).astype(BF16)
    y_b = (_get_tile(zs_ref) * _silu(proj(wgb_ref))).astype(BF16)
    y_c = (_get_tile(ys_ref) * _silu(proj(wgc_ref))).astype(BF16)
    merged = (jax.nn.sigmoid(proj(wma_ref)) * jnp.dot(y_a, wa_ref[...], preferred_element_type=F32)
              + jax.nn.sigmoid(proj(wmb_ref)) * jnp.dot(y_b, wb_ref[...], preferred_element_type=F32)
              + jax.nn.sigmoid(proj(wmc_ref)) * jnp.dot(y_c, wc_ref[...], preferred_element_type=F32))
    out = jnp.dot(merged.astype(BF16), wo_ref[...], preferred_element_type=F32)
    x_new = x + _mod_part(mod, 2, bm) * _by_batch(out, bm)
    if final:
        r = lax.rsqrt(jnp.mean(x_new * x_new, axis=-1, keepdims=True) + EPS)
        _put_tile(xs_ref, (x_new * r * fg_ref[...]).reshape(-1, D_MODEL))
        for b in range(B):
            xo_ref[b] = _get_batch(xs_ref, b, B)
    else:
        xo_ref[...] = x_new.reshape(-1, D_MODEL)


def _out_proj(x, pos, mod, mod_blk, B, bm, g, ua, ya, zb, yc, w_in, l, d, w_glu, b_glu,
              w_a, w_b, w_c, w_o, final_g, tm):
    D = D_MODEL
    rows = x.size // D
    L = rows // B
    batch_major = x.ndim == 3
    final = final_g is not None
    tok = lambda w: pl.BlockSpec((tm, w), lambda i: (i, 0))
    seq = lambda w: pl.BlockSpec((tm // B, B * w), lambda i: (i, 0))
    full = lambda a: pl.BlockSpec(a.shape, lambda i: (0,) * a.ndim)
    in_specs, args = _x_specs(x, pos, mod, mod_blk, B, bm, tm)
    in_specs += [full(g), tok(W_A), tok(W_A), seq(W_B), seq(W_C)]
    args += [g, ua, ya, zb, yc]
    for cb in G_BLOCKS:
        in_specs.append(pl.BlockSpec((1, D, CB), lambda i, cb=cb: (l, 0, cb)))
        args.append(w_in)
    for k in range(3):
        in_specs.append(pl.BlockSpec((1, D, D), lambda i, k=k: (l, 0, M_BLOCK0 + k)))
        args.append(w_in)
    weights = [d, w_glu, b_glu, w_a, w_b, w_c, w_o]
    in_specs += [full(a) for a in weights]
    args += weights
    if final:
        in_specs.append(full(final_g))
        args.append(final_g)
    scratch = [_stage(tm, W_B), _stage(tm, W_C)]
    if batch_major or final:
        scratch.append(_stage(tm, D))
    if final:
        out_spec = pl.BlockSpec((B, tm // B, D), lambda i: (0, i, 0))
        out_shape = jax.ShapeDtypeStruct((B, L, D), F32)
    else:
        out_spec = tok(D)
        out_shape = jax.ShapeDtypeStruct((rows, D), F32)
    return pl.pallas_call(
        functools.partial(_out_kernel, has_pos=pos is not None, batch_major=batch_major, final=final,
                          B=B, bm=bm),
        grid=(rows // tm,),
        in_specs=in_specs,
        out_specs=out_spec,
        out_shape=out_shape,
        scratch_shapes=scratch,
        compiler_params=_cparams(1),
        name="out_proj",
    )(*args)


def _grid_pos_embed(L):
    rows = L // GRID_W
    r = jnp.broadcast_to(jnp.arange(rows, dtype=F32)[:, None], (rows, GRID_W)).reshape(-1)
    col = jnp.broadcast_to(jnp.arange(GRID_W, dtype=F32)[None, :], (rows, GRID_W)).reshape(-1)
    q = D_MODEL // 4
    omega = 1.0 / (10000.0 ** (jnp.arange(q, dtype=F32) / q))
    ar = r[:, None] * omega[None, :]
    ac = col[:, None] * omega[None, :]
    return jnp.concatenate([jnp.sin(ar), jnp.cos(ar), jnp.sin(ac), jnp.cos(ac)], axis=-1)


def _states_to_lanes(st_re, st_im):
    s = jnp.concatenate([st_re[:, 0], st_re[:, 1], st_im[:, 0], st_im[:, 1]], axis=-1)
    return s.transpose(1, 0, 2)


def _lanes_to_states(fin):
    f = fin.transpose(1, 0, 2).reshape(fin.shape[1], S5_G, 4, S5_P)
    return (jnp.stack([f[:, :, 0], f[:, :, 1]], axis=1), jnp.stack([f[:, :, 2], f[:, :, 3]], axis=1))


def kernel(x_prompt, x_sample, c, state_s5_re, state_s5_im, c_ctx, norm_g, w_mod, b_mod, w_in, s5_lam_re, s5_lam_im, s5_log_dt, s5_b_re, s5_b_im, s5_c_re, s5_c_im, s5_d, s5_w_glu, s5_b_glu, hy_conv_w, hy_conv_b, hy_f_w1, hy_f_b1, hy_f_w2, hy_f_b2, hy_f_freq, hy_f_w3, hy_bias, pool_w, pool_scale, w_br_a, w_br_b, w_br_c, w_out, final_g):
    Bc, Lc, D = x_prompt.shape
    Bl, Ll, _ = x_sample.shape
    assert Bl == MOD_ROWS - MOD_LAT0

    cond = jnp.zeros((MOD_ROWS, D), F32).at[0].set(c_ctx).at[MOD_LAT0:].set(c)
    mod = _modulation(cond, w_mod, b_mod).reshape(DEPTH * MOD_ROWS, 3 * D)

    pos = _grid_pos_embed(Ll)
    groups = {
        'ctx': dict(B=Bc, L=Lc, S=2, C=W_B, bm=1, mod_blk=0),
        'lat': dict(B=Bl, L=Ll, S=4, C=W_B // 2, bm=Bl, mod_blk=1),
    }
    tables = {k: _dft_tables(v['L'] // v['S']) for k, v in groups.items()}
    w_in_b = w_in.astype(BF16)
    xs = {'ctx': x_prompt, 'lat': x_sample}
    new_re, new_im = [], []
    for l in range(DEPTH):
        s5_ops = _s5_operators(s5_lam_re[l], s5_lam_im[l], s5_log_dt[l], s5_b_re[l], s5_b_im[l],
                               s5_c_re[l], s5_c_im[l])
        g = norm_g[l].reshape(1, D)
        for name, cfg in groups.items():
            B, L, bm = cfg['B'], cfg['L'], cfg['bm']
            x = xs[name]
            p = pos if (name == 'lat' and l == 0) else None
            mod_blk = l * (MOD_ROWS // MOD_LAT0) + cfg['mod_blk']
            ua, ut = _in_proj(x, p, mod, mod_blk, B, bm, g, w_in_b, l, min(512, L * B))
            if name == 'ctx':
                h0g = jnp.zeros((S5_G, B, 4 * S5_P), F32)
            else:
                h0g = _states_to_lanes(state_s5_re[:, l], state_s5_im[:, l])
            ya, fin = _s5_mix(ua, s5_ops, h0g, B)
            if name == 'ctx':
                fr, fi = _lanes_to_states(fin)
                new_re.append(fr)
                new_im.append(fi)
            fs, fst = tables[name]
            filt = _hyena_filters(L, hy_f_w1[l], hy_f_b1[l], hy_f_w2[l], hy_f_b2[l], hy_f_freq[l], hy_f_w3[l])
            hw = _filter_spectrum(fs, filt.reshape(L, 2 * W_B).astype(BF16), cfg['S'], cfg['C'])
            zb = _hyena_mix(ut, B, hy_conv_w[l], hy_conv_b[l], hy_bias[l], fs, fst, hw, cfg['C'])
            yc = _pool_mix(ut, B, pool_w[l].astype(BF16), pool_scale[l])
            fg = final_g.reshape(1, D) if l == DEPTH - 1 else None
            xs[name] = _out_proj(x, p, mod, mod_blk, B, bm, g, ua, ya, zb, yc, w_in_b, l,
                                 s5_d[l].reshape(1, W_A), s5_w_glu[l].astype(BF16), s5_b_glu[l].reshape(1, W_A),
                                 w_br_a[l].astype(BF16), w_br_b[l].astype(BF16), w_br_c[l].astype(BF16),
                                 w_out[l].astype(BF16), fg, min(256, L * B))
    return (xs['ctx'], xs['lat'], jnp.stack(new_re, axis=1), jnp.stack(new_im, axis=1))
```

```python
import functools
import math

import jax
import jax.numpy as jnp
from jax import lax
from jax.experimental import pallas as pl
from jax.experimental.pallas import tpu as pltpu

F32 = jnp.float32
BF16 = jnp.bfloat16
HIGHEST = lax.Precision.HIGHEST

D_MODEL = 1024
DEPTH = 2
GRID_W = 64
EPS = 1e-6
W_A = D_MODEL // 2
S5_H = 16
S5_G = W_A // S5_H
S5_P = 64
W_B = D_MODEL // 2
HY_BANDS = 16
HY_FAST_PCT = 0.3
HY_SLOW_PCT = 1.5
HY_TARGET = 1e-2
HY_SHIFT = 0.05
W_C = D_MODEL // 2
POOL_WINDOWS = (2, 4, 8, 16)
POOL_G = W_C // 4
LANES = 128
CB = 512
U_BLOCKS = (0, 2, 3, 4, 6)
G_BLOCKS = (1, 5, 7)
M_BLOCK0 = 4
UT_COLS = CB * (len(U_BLOCKS) - 1)
S5_CHUNK = 16
S5_W = S5_CHUNK * S5_H
S5_GB = LANES // S5_H
MOD_ROWS = 16
MOD_LAT0 = 8
VMEM_LIMIT = 56 * 1024 * 1024


def _cparams(n_grid):
    return pltpu.CompilerParams(dimension_semantics=("arbitrary",) * n_grid,
                                vmem_limit_bytes=VMEM_LIMIT)


def _mod_kernel(c_ref, w_ref, b_ref, o_ref):
    c = c_ref[...]
    s = c * jax.nn.sigmoid(c)
    o_ref[0] = jnp.dot(s, w_ref[0], preferred_element_type=F32, precision=HIGHEST) + b_ref[0]


def _modulation(cond, w_mod, b_mod, tn=512):
    n = 3 * D_MODEL
    return pl.pallas_call(
        _mod_kernel,
        grid=(DEPTH, n // tn),
        in_specs=[pl.BlockSpec((MOD_ROWS, D_MODEL), lambda l, j: (0, 0)),
                  pl.BlockSpec((1, D_MODEL, tn), lambda l, j: (l, 0, j)),
                  pl.BlockSpec((1, 1, tn), lambda l, j: (l, 0, j))],
        out_specs=pl.BlockSpec((1, MOD_ROWS, tn), lambda l, j: (l, 0, j)),
        out_shape=jax.ShapeDtypeStruct((DEPTH, MOD_ROWS, n), F32),
        compiler_params=_cparams(2),
        name="modulation",
    )(cond, w_mod, b_mod.reshape(DEPTH, 1, n))


def _by_batch(x, bm):
    return x if bm == 1 else x.reshape(x.shape[0] // bm, bm, x.shape[1])


def _mod_part(mod, i, bm):
    m = mod[:bm, i * D_MODEL:(i + 1) * D_MODEL]
    return m if bm == 1 else m[None]


def _stage(rows, w):
    return pltpu.VMEM((w // LANES, rows, LANES), F32)


def _put_batch(ref, b, B, val):
    for c in range(ref.shape[0]):
        ref[c, pl.ds(b, val.shape[0], stride=B), :] = val[:, c * LANES:(c + 1) * LANES]


def _get_batch(ref, b, B):
    tt = ref.shape[1] // B
    return jnp.concatenate([ref[c, pl.ds(b, tt, stride=B), :] for c in range(ref.shape[0])], axis=1)


def _put_tile(ref, val):
    for c in range(ref.shape[0]):
        ref[c] = val[:, c * LANES:(c + 1) * LANES]


def _get_tile(ref):
    return jnp.concatenate([ref[c] for c in range(ref.shape[0])], axis=1)


def _load_x(x_ref, xs_ref, pos_ref, B, bm):
    if xs_ref is not None:
        for b in range(B):
            _put_batch(xs_ref, b, B, x_ref[b])
        x2 = _get_tile(xs_ref)
    else:
        x2 = x_ref[...]
    x = _by_batch(x2, bm)
    if pos_ref is not None:
        x = x + pos_ref[...][:, None, :]
    return x


def _normed(x, mod, g, bm):
    r = lax.rsqrt(jnp.mean(x * x, axis=-1, keepdims=True) + EPS)
    h = (x * r * g) * (1.0 + _mod_part(mod, 1, bm)) + _mod_part(mod, 0, bm)
    return h.reshape(-1, D_MODEL).astype(BF16)


def _in_kernel(*refs, has_pos, batch_major, B, bm):
    refs = list(refs)
    x_ref = refs.pop(0)
    pos_ref = refs.pop(0) if has_pos else None
    mod_ref, g_ref = refs[:2]
    w_refs = refs[2:2 + len(U_BLOCKS)]
    oa_ref, ot_ref, scr_ref = refs[2 + len(U_BLOCKS):5 + len(U_BLOCKS)]
    xs_ref = refs[-1] if batch_major else None
    h = _normed(_load_x(x_ref, xs_ref, pos_ref, B, bm), mod_ref[...], g_ref[...], bm)
    oa_ref[...] = jnp.dot(h, w_refs[0][0], preferred_element_type=F32)
    for i, w_ref in enumerate(w_refs[1:]):
        _put_tile(scr_ref, jnp.dot(h, w_ref[0], preferred_element_type=F32))
        for b in range(B):
            ot_ref[:, b * UT_COLS + i * CB:b * UT_COLS + (i + 1) * CB] = _get_batch(scr_ref, b, B)


def _x_specs(x, pos, mod, mod_blk, B, bm, tm):
    D = D_MODEL
    if x.ndim == 3:
        specs = [pl.BlockSpec((B, tm // B, D), lambda i: (0, i, 0))]
    else:
        specs = [pl.BlockSpec((tm, D), lambda i: (i, 0))]
    args = [x]
    if pos is not None:
        specs.append(pl.BlockSpec((tm // bm, D), lambda i: (i, 0)))
        args.append(pos)
    specs.append(pl.BlockSpec((MOD_LAT0, 3 * D), lambda i: (mod_blk, 0)))
    args.append(mod)
    return specs, args


def _in_proj(x, pos, mod, mod_blk, B, bm, g, w_in, l, tm):
    D = D_MODEL
    rows = x.size // D
    L = rows // B
    batch_major = x.ndim == 3
    in_specs, args = _x_specs(x, pos, mod, mod_blk, B, bm, tm)
    in_specs.append(pl.BlockSpec((1, D), lambda i: (0, 0)))
    args.append(g)
    for cb in U_BLOCKS:
        in_specs.append(pl.BlockSpec((1, D, CB), lambda i, cb=cb: (l, 0, cb)))
        args.append(w_in)
    scratch = [_stage(tm, CB)] + ([_stage(tm, D)] if batch_major else [])
    return pl.pallas_call(
        functools.partial(_in_kernel, has_pos=pos is not None, batch_major=batch_major, B=B, bm=bm),
        grid=(rows // tm,),
        in_specs=in_specs,
        out_specs=[pl.BlockSpec((tm, W_A), lambda i: (i, 0)),
                   pl.BlockSpec((tm // B, B * UT_COLS), lambda i: (i, 0))],
        out_shape=[jax.ShapeDtypeStruct((rows, W_A), F32),
                   jax.ShapeDtypeStruct((L, B * UT_COLS), F32)],
        scratch_shapes=scratch,
        compiler_params=_cparams(1),
        name="in_proj",
    )(*args)


S5_RB = 64
S5_GS = 4


def _s5_kernel(u_ref, m_ref, p_ref, qt_ref, a_ref, h0_ref, y_ref, fin_ref,
               x_ref, yall_ref, sloc_ref, sinf_ref, sinb_ref, *, nc, B):
    rows = nc * B
    cpb = S5_RB // B
    half = S5_W // 2
    lane_grp = lax.broadcasted_iota(jnp.int32, (S5_RB, LANES), 1) // S5_H

    lane = lax.broadcasted_iota(jnp.int32, (B, half), 1)
    is_fwd = lane < S5_P
    lane_full = lax.broadcasted_iota(jnp.int32, (rows, S5_W), 1)
    is_fwd_full = (lane_full % half) < S5_P

    def gather_step(rb, _, g0):
        c0 = pl.multiple_of(rb * cpb, cpb)
        r0 = pl.multiple_of(rb * S5_RB, S5_RB)
        slabs = [u_ref[pl.ds(c0, cpb), t * B:(t + 1) * B, :].reshape(S5_RB, LANES) for t in range(S5_CHUNK)]
        for gi in range(S5_GS):
            g8 = g0 + gi
            for hf in range(2):
                acc = None
                for k in range(S5_GB):
                    s = slabs[hf * S5_GB + k]
                    r = s if k == g8 else pltpu.roll(s, ((k - g8) * S5_H) % LANES, 1)
                    acc = r if acc is None else jnp.where(lane_grp == k, r, acc)
                x_ref[gi, pl.ds(r0, S5_RB), hf * LANES:(hf + 1) * LANES] = acc.astype(BF16)
        return 0

    def group_step(gi, _, g0):
        g8 = g0 + gi
        u = x_ref[gi]
        sloc_ref[...] = jnp.dot(u, p_ref[g8], preferred_element_type=F32)
        a = a_ref[g8]
        ar = a[:, :half]
        ai = a[:, half:]

        def body(i, carry):
            cr, ci = carry
            jf = pl.multiple_of(i * B, B)
            jb = pl.multiple_of((nc - 1 - i) * B, B)
            sinf_ref[pl.ds(jf, B), :half] = cr
            sinf_ref[pl.ds(jf, B), half:] = ci
            sinb_ref[pl.ds(jb, B), :half] = cr
            sinb_ref[pl.ds(jb, B), half:] = ci
            lf = sloc_ref[pl.ds(jf, B), :]
            lb = sloc_ref[pl.ds(jb, B), :]
            lr = jnp.where(is_fwd, lf[:, :half], lb[:, :half])
            li = jnp.where(is_fwd, lf[:, half:], lb[:, half:])
            return (ar * cr - ai * ci + lr, ar * ci + ai * cr + li)

        h0 = h0_ref[g8]
        fr, fi = lax.fori_loop(0, nc, body, (h0[:, :half], h0[:, half:]))
        fin_ref[g8, :, :half] = fr
        fin_ref[g8, :, half:] = fi
        s_in = jnp.where(is_fwd_full, sinf_ref[...], sinb_ref[...]).astype(BF16)
        yall_ref[gi] = (jnp.dot(u, m_ref[g8], preferred_element_type=F32)
                        + lax.dot_general(s_in, qt_ref[g8], (((1,), (1,)), ((), ())),
                                          preferred_element_type=F32))
        return 0

    def scatter_step(rb, _, g0):
        c0 = pl.multiple_of(rb * cpb, cpb)
        r0 = pl.multiple_of(rb * S5_RB, S5_RB)
        for hf in range(2):
            ys = [yall_ref[gi, pl.ds(r0, S5_RB), hf * LANES:(hf + 1) * LANES] for gi in range(S5_GS)]
            for k in range(S5_GB):
                acc = None
                for gi in range(S5_GS):
                    g8 = g0 + gi
                    r = ys[gi] if k == g8 else pltpu.roll(ys[gi], ((g8 - k) * S5_H) % LANES, 1)
                    acc = r if acc is None else jnp.where(lane_grp == g8, r, acc)
                t = hf * S5_GB + k
                dst = (pl.ds(c0, cpb), slice(t * B, (t + 1) * B), slice(None))
                if g0 > 0:
                    acc = jnp.where(lane_grp >= g0, acc, y_ref[dst].reshape(S5_RB, LANES))
                y_ref[dst] = acc.reshape(cpb, B, LANES)
        return 0

    for g0 in range(0, S5_GB, S5_GS):
        lax.fori_loop(0, rows // S5_RB, functools.partial(gather_step, g0=g0), 0)
        lax.fori_loop(0, S5_GS, functools.partial(group_step, g0=g0), 0)
        lax.fori_loop(0, rows // S5_RB, functools.partial(scatter_step, g0=g0), 0)


def _s5_mix(ua, ops, h0, B):
    m, p, q, a16 = ops
    rows_all = ua.shape[0]
    nc = rows_all // (S5_CHUNK * B)
    rows = nc * B
    up3 = ua.reshape(nc, S5_CHUNK * B, W_A)
    gblk = lambda r: pl.BlockSpec((S5_GB, r, S5_W), lambda j: (j, 0, 0))
    tok = pl.BlockSpec((nc, S5_CHUNK * B, LANES), lambda j: (0, 0, j))
    y, fin = pl.pallas_call(
        functools.partial(_s5_kernel, nc=nc, B=B),
        grid=(W_A // LANES,),
        in_specs=[tok, gblk(S5_W), gblk(S5_W), gblk(S5_W), gblk(1), gblk(B)],
        out_specs=[tok, gblk(B)],
        out_shape=[jax.ShapeDtypeStruct((nc, S5_CHUNK * B, W_A), F32),
                   jax.ShapeDtypeStruct((S5_G, B, S5_W), F32)],
        scratch_shapes=[pltpu.VMEM((S5_GS, rows, S5_W), BF16), pltpu.VMEM((S5_GS, rows, S5_W), F32),
                        pltpu.VMEM((rows, S5_W), F32), pltpu.VMEM((rows, S5_W), F32),
                        pltpu.VMEM((rows, S5_W), F32)],
        compiler_params=_cparams(1),
        name="s5_mix",
    )(up3, m, p, q, a16, h0)
    return y.reshape(rows_all, W_A), fin


def _s5_ops_kernel(lr_ref, li_ref, ldt_ref, br_ref, bi_ref, cr_ref, ci_ref, m_ref, p_ref, qt_ref, a_ref):
    T = S5_CHUNK
    H = S5_H
    lam_re = lr_ref[0]
    lam_im = li_ref[0]
    dt = jnp.exp(ldt_ref[0])
    mag = jnp.exp(lam_re * dt)
    ang = lam_im * dt
    a_re = mag * jnp.cos(ang)
    a_im = mag * jnp.sin(ang)
    n_re = a_re - 1.0
    n_im = a_im
    den = lam_re * lam_re + lam_im * lam_im
    k_re = (n_re * lam_re + n_im * lam_im) / den
    k_im = (n_im * lam_re - n_re * lam_im) / den
    b_re = br_ref[0]
    b_im = bi_ref[0]
    bb_re = k_re * b_re - k_im * b_im
    bb_im = k_re * b_im + k_im * b_re
    c_re = cr_ref[0]
    c_im = ci_ref[0]
    ap = [(jnp.ones_like(a_re), jnp.zeros_like(a_re))]
    for _ in range(T):
        pr, pi = ap[-1]
        ap.append((pr * a_re - pi * a_im, pr * a_im + pi * a_re))
    is_fwd = lax.broadcasted_iota(jnp.int32, (1, LANES), 1) < S5_P

    def powers(kf, kb):
        return jnp.where(is_fwd, ap[kf][0], ap[kb][0]), jnp.where(is_fwd, ap[kf][1], ap[kb][1])

    for t in range(T):
        rows = slice(t * H, (t + 1) * H)
        er, ei = powers(T - 1 - t, t)
        p_ref[0, rows, :LANES] = (er * bb_re - ei * bb_im).astype(BF16)
        p_ref[0, rows, LANES:] = (er * bb_im + ei * bb_re).astype(BF16)
        er, ei = powers(t + 1, T - t)
        qt_ref[0, rows, :LANES] = (c_re * er - c_im * ei).astype(BF16)
        qt_ref[0, rows, LANES:] = (-(c_re * ei + c_im * er)).astype(BF16)
    a_ref[0, :, :LANES] = ap[T][0]
    a_ref[0, :, LANES:] = ap[T][1]

    def ca(k):
        return c_re * ap[k][0] - c_im * ap[k][1], c_re * ap[k][1] + c_im * ap[k][0]

    fwd = [ca(k) for k in range(T)]
    bwd = fwd[::-1]
    nt = lambda a, b: lax.dot_general(a, b, (((1,), (1,)), ((), ())), precision=HIGHEST,
                                      preferred_element_type=F32)
    cat = lambda parts, i: jnp.concatenate([p[i] for p in parts], axis=0)
    zero = jnp.zeros_like(bb_re)
    wf = (nt(jnp.where(is_fwd, bb_re, zero), cat(fwd, 0)) - nt(jnp.where(is_fwd, bb_im, zero), cat(fwd, 1)))
    wb = (nt(jnp.where(is_fwd, zero, bb_re), cat(bwd, 0)) - nt(jnp.where(is_fwd, zero, bb_im), cat(bwd, 1)))
    pad = jnp.zeros((H, S5_W), F32)
    wf_pad = jnp.concatenate([pad, wf], axis=1)
    wb_pad = jnp.concatenate([wb, pad], axis=1)
    for t in range(T):
        m_ref[0, t * H:(t + 1) * H, :] = (wf_pad[:, S5_W - H * t:2 * S5_W - H * t]
                                          + wb_pad[:, (T - 1 - t) * H:(T - 1 - t) * H + S5_W]).astype(BF16)


def _s5_operators(lam_re, lam_im, log_dt, b_re, b_im, c_re, c_im):
    G = S5_G
    dirs = lambda x: jnp.concatenate([x[0], x[1]], axis=-1)
    lam = [dirs(x)[:, None, :] for x in (lam_re, lam_im)]
    ldt = dirs(jnp.broadcast_to(log_dt[..., None], (2, G, S5_P)))[:, None, :]
    bt = [dirs(x.transpose(0, 1, 3, 2)) for x in (b_re, b_im)]
    ct = [dirs(x) for x in (c_re, c_im)]
    vec = pl.BlockSpec((1, 1, LANES), lambda g: (g, 0, 0))
    mat = pl.BlockSpec((1, S5_H, LANES), lambda g: (g, 0, 0))
    op = pl.BlockSpec((1, S5_W, S5_W), lambda g: (g, 0, 0))
    return pl.pallas_call(
        _s5_ops_kernel,
        grid=(G,),
        in_specs=[vec, vec, vec, mat, mat, mat, mat],
        out_specs=[op, op, op, pl.BlockSpec((1, 1, S5_W), lambda g: (g, 0, 0))],
        out_shape=[jax.ShapeDtypeStruct((G, S5_W, S5_W), BF16)] * 3
        + [jax.ShapeDtypeStruct((G, 1, S5_W), F32)],
        compiler_params=_cparams(1),
        name="s5_operators",
    )(*lam, ldt, *bt, *ct)


def _phase_tables(u, va, vb, ncb, n):
    def trig(m):
        m = m & (2 * n - 1)
        m = jnp.where(m >= n, m - 2 * n, m)
        ang = m.astype(F32) * (math.pi / n)
        return jnp.cos(ang), jnp.sin(ang)

    lane = lax.broadcasted_iota(jnp.int32, (1, LANES), 1)
    cb, sb = trig(u * (va * lane + vb))
    ca_all, sa_all = trig(u * (va * LANES * lane))
    out = []
    for ch in range(ncb):
        ca = ca_all[:, ch:ch + 1]
        sa = sa_all[:, ch:ch + 1]
        out.append((ca * cb - sa * sb, sa * cb + ca * sb))
    return out


def _tables_kernel(fs_ref, fst_ref, blk_ref, *, Ls):
    u = 2 * lax.broadcasted_iota(jnp.int32, (Ls, 1), 0) + 1
    for ch, (c, s) in enumerate(_phase_tables(u, 1, 0, Ls // LANES, 2 * Ls)):
        blk_ref[:Ls, ch * LANES:(ch + 1) * LANES] = c
        blk_ref[Ls:, ch * LANES:(ch + 1) * LANES] = -s
    blk = blk_ref[...]
    fs_ref[...] = blk.astype(BF16)
    fst_ref[...] = blk.T.astype(BF16)


def _dft_tables(Ls):
    return pl.pallas_call(
        functools.partial(_tables_kernel, Ls=Ls),
        out_shape=[jax.ShapeDtypeStruct((2 * Ls, Ls), BF16), jax.ShapeDtypeStruct((Ls, 2 * Ls), BF16)],
        scratch_shapes=[pltpu.VMEM((2 * Ls, Ls), F32)],
        compiler_params=pltpu.CompilerParams(vmem_limit_bytes=VMEM_LIMIT),
        name="dft_tables",
    )()


def _spectrum_kernel(fs_ref, f_ref, w_ref, *, S, Ls):
    scale = 1.0 / Ls
    k = lax.broadcasted_iota(jnp.int32, (Ls, 1), 0)
    sgn = jnp.where(k % 2 == 0, 1.0, -1.0)
    prev = None
    for i in range(S + 1):
        re = im = None
        if i < S:
            g = jnp.dot(fs_ref[...], f_ref[i * Ls:(i + 1) * Ls, :], preferred_element_type=F32) * scale
            re, im = g[:Ls], g[Ls:]
        cur = (re, im)
        if prev is not None:
            jre, jim = -sgn * prev[1], sgn * prev[0]
            re = jre if re is None else re + jre
            im = jim if im is None else im + jim
        w_ref[0, i, :Ls] = re
        w_ref[0, i, Ls:] = im
        prev = cur


def _filter_spectrum(fs, filt, S, C):
    n2, Ls = fs.shape
    L = filt.shape[0]
    nct = W_B // C
    return pl.pallas_call(
        functools.partial(_spectrum_kernel, S=S, Ls=Ls),
        grid=(2, nct),
        in_specs=[pl.BlockSpec((n2, Ls), lambda o, ct: (0, 0)),
                  pl.BlockSpec((L, C), lambda o, ct: (0, o * nct + ct))],
        out_specs=pl.BlockSpec((1, S + 1, n2, C), lambda o, ct: (o, 0, 0, ct)),
        out_shape=jax.ShapeDtypeStruct((2, S + 1, n2, W_B), F32),
        compiler_params=_cparams(2),
        name="filter_spectrum",
    )(fs, filt)


def _hyena_filters(L, w1, b1, w2, b2, freq, w3):
    t = jnp.arange(L, dtype=F32)
    tn = t / (L - 1)
    f = jnp.linspace(1e-4, HY_BANDS - 1, HY_BANDS, dtype=F32)
    ang = (2.0 * math.pi / L) * t[:, None] * f[None, :]
    z = jnp.concatenate([tn[:, None], jnp.cos(ang), -jnp.sin(ang)], axis=-1)
    h = jnp.sin(freq * (jnp.dot(z, w1, precision=HIGHEST) + b1))
    h = jnp.sin(freq * (jnp.dot(h, w2, precision=HIGHEST) + b2))
    h = jnp.dot(h, w3, precision=HIGHEST).reshape(L, 2, W_B)
    max_decay = math.log(HY_TARGET) / HY_FAST_PCT
    min_decay = math.log(HY_TARGET) / HY_SLOW_PCT
    deltas = jnp.abs(jnp.linspace(min_decay, max_decay, W_B, dtype=F32))
    half = L // 2
    off = jnp.abs(t - half) / half
    win = jnp.exp(-off[:, None] * deltas[None, :]) + HY_SHIFT
    return h * win[:, None, :]


def _shift_rows(u, s, row):
    L = u.shape[0]
    r = pltpu.roll(u, (-s) % L, 0)
    return jnp.where((row + s >= 0) & (row + s < L), r, 0.0)


HY_RF = 32
SUBLANES = 8


def _short_conv_rows(u_ref, w, bias, c, R, L):
    r0 = pl.multiple_of(c * R, R)
    cur = u_ref[pl.ds(r0, R), :]
    before = u_ref[pl.ds(pl.multiple_of(jnp.maximum(r0 - SUBLANES, 0), SUBLANES), SUBLANES), :]
    after = u_ref[pl.ds(pl.multiple_of(jnp.minimum(r0 + R, L - SUBLANES), SUBLANES), SUBLANES), :]
    before = jnp.where(c == 0, 0.0, before)
    after = jnp.where(c == L // R - 1, 0.0, after)
    ext = jnp.concatenate([before, cur, after], axis=0)
    n = R + 2 * SUBLANES
    up = pltpu.roll(ext, 1, 0)[SUBLANES:SUBLANES + R]
    dn = pltpu.roll(ext, n - 1, 0)[SUBLANES:SUBLANES + R]
    return bias + up * w[0:1] + cur * w[1:2] + dn * w[2:3]


def _hy_kernel(v_ref, x_ref, cwv_ref, cbv_ref, cwx_ref, cbx_ref, hb_ref, fs_ref, fst_ref, w_ref, o_ref,
               zf_ref, zb_ref, u_ref, d_ref, os_ref, *, L, S, R):
    Ls = L // S
    C = zf_ref.shape[1]
    o = pl.program_id(2)
    seg = lambda j: slice(j * C, (j + 1) * C)

    @pl.when(o == 0)
    def _():
        for j in range(S):
            def body(cc, _, j=j):
                c = j * (Ls // R) + cc
                z = _short_conv_rows(v_ref, cwv_ref[...], cbv_ref[...], c, R, L)
                zf_ref[pl.ds(pl.multiple_of(c * R, R), R), :] = z
                zb_ref[pl.ds(pl.multiple_of(cc * R, R), R), seg(j)] = z.astype(BF16)
                return 0
            lax.fori_loop(0, Ls // R, body, 0)

    u_ref[...] = jnp.dot(fs_ref[...], zb_ref[...], preferred_element_type=F32)

    def prod(fc, _):
        f0 = pl.multiple_of(fc * HY_RF, HY_RF)
        re_rows = pl.ds(f0, HY_RF)
        im_rows = pl.ds(Ls + f0, HY_RF)
        for r in range(S):
            dre = dim = None
            for j in range(S):
                d = r - j + S // 2
                if 0 <= d <= S:
                    ur, ui = u_ref[re_rows, seg(j)], u_ref[im_rows, seg(j)]
                    wr, wi = w_ref[0, d, re_rows, :], w_ref[0, d, im_rows, :]
                    pre, pim = ur * wr - ui * wi, ur * wi + ui * wr
                    dre = pre if dre is None else dre + pre
                    dim = pim if dim is None else dim + pim
            d_ref[re_rows, seg(r)] = dre.astype(BF16)
            d_ref[im_rows, seg(r)] = dim.astype(BF16)
        return 0
    lax.fori_loop(0, Ls // HY_RF, prod, 0)

    os_ref[...] = jnp.dot(fst_ref[...], d_ref[...], preferred_element_type=F32)

    for r in range(S):
        def gate(cc, _, r=r):
            c = r * (Ls // R) + cc
            rows = pl.ds(pl.multiple_of(c * R, R), R)
            local = pl.ds(pl.multiple_of(cc * R, R), R)
            x = _short_conv_rows(x_ref, cwx_ref[...], cbx_ref[...], c, R, L)
            z = x * (os_ref[local, seg(r)] + zf_ref[rows, :] * hb_ref[0])

            @pl.when(o == 0)
            def _():
                zf_ref[rows, :] = z
                zb_ref[local, seg(r)] = z.astype(BF16)

            @pl.when(o == 1)
            def _():
                o_ref[rows, :] = z
            return 0
        lax.fori_loop(0, Ls // R, gate, 0)


def _hyena_mix(ut, B, conv_w, conv_b, bias, fs, fst, w, C):
    L = ut.shape[0]
    n2, Ls = fs.shape
    S = L // Ls
    nct = W_B // C
    ncol = UT_COLS // C
    xcol = lambda o, ct: (1 + o) * nct + ct
    conv_b = conv_b.reshape(1, -1)
    R = min(128, Ls)
    return pl.pallas_call(
        functools.partial(_hy_kernel, L=L, S=S, R=R),
        grid=(B, nct, 2),
        in_specs=[pl.BlockSpec((L, C), lambda b, ct, o: (0, b * ncol + ct)),
                  pl.BlockSpec((L, C), lambda b, ct, o: (0, b * ncol + xcol(o, ct))),
                  pl.BlockSpec((3, C), lambda b, ct, o: (0, ct)),
                  pl.BlockSpec((1, C), lambda b, ct, o: (0, ct)),
                  pl.BlockSpec((3, C), lambda b, ct, o: (0, xcol(o, ct))),
                  pl.BlockSpec((1, C), lambda b, ct, o: (0, xcol(o, ct))),
                  pl.BlockSpec((1, 1, C), lambda b, ct, o: (o, 0, ct)),
                  pl.BlockSpec((n2, Ls), lambda b, ct, o: (0, 0)),
                  pl.BlockSpec((Ls, n2), lambda b, ct, o: (0, 0)),
                  pl.BlockSpec((1, S + 1, n2, C), lambda b, ct, o: (o, 0, 0, ct))],
        out_specs=pl.BlockSpec((L, C), lambda b, ct, o: (0, b * nct + ct)),
        out_shape=jax.ShapeDtypeStruct((L, B * W_B), F32),
        scratch_shapes=[pltpu.VMEM((L, C), F32), pltpu.VMEM((Ls, S * C), BF16), pltpu.VMEM((n2, S * C), F32),
                        pltpu.VMEM((n2, S * C), BF16), pltpu.VMEM((Ls, S * C), F32)],
        compiler_params=_cparams(3),
        name="hyena_mix",
    )(ut, ut, conv_w, conv_b, conv_w, conv_b, bias.reshape(2, 1, W_B), fs, fst, w)


def _pool_kernel(u_ref, w_ref, sc_ref, o_ref, *, L):
    row = lax.broadcasted_iota(jnp.int32, (L, 1), 0)
    for gi, win in enumerate(POOL_WINDOWS):
        u = u_ref[:, gi * POOL_G:(gi + 1) * POOL_G]
        s = u
        for off in range(-(win // 2), win - win // 2):
            if off != 0:
                s = s + _shift_rows(u, off, row)
        lo = jnp.maximum(row - win // 2, 0)
        hi = jnp.minimum(row - win // 2 + win, L)
        d = s / (hi - lo).astype(F32) - u
        o_ref[:, gi * POOL_G:(gi + 1) * POOL_G] = (
            jnp.dot(d.astype(BF16), w_ref[gi], preferred_element_type=F32)
            * sc_ref[:, gi * POOL_G:(gi + 1) * POOL_G])


def _pool_mix(ut, B, w, scale):
    L = ut.shape[0]
    nblk = UT_COLS // W_C
    return pl.pallas_call(
        functools.partial(_pool_kernel, L=L),
        grid=(B,),
        in_specs=[pl.BlockSpec((L, W_C), lambda b: (0, b * nblk + nblk - 1)),
                  pl.BlockSpec((len(POOL_WINDOWS), POOL_G, POOL_G), lambda b: (0, 0, 0)),
                  pl.BlockSpec((1, W_C), lambda b: (0, 0))],
        out_specs=pl.BlockSpec((L, W_C), lambda b: (0, b)),
        out_shape=jax.ShapeDtypeStruct((L, B * W_C), F32),
        compiler_params=_cparams(1),
        name="pool_mix",
    )(ut, w, scale.reshape(1, W_C))


def _silu(x):
    return x * jax.nn.sigmoid(x)


def _gelu_tanh(x):
    return x * (0.5 * (1.0 + jnp.tanh(math.sqrt(2.0 / math.pi) * (x + 0.044715 * (x * x * x)))))


def _out_kernel(*refs, has_pos, batch_major, final, B, bm):
    refs = list(refs)
    x_ref = refs.pop(0)
    pos_ref = refs.pop(0) if has_pos else None
    (mod_ref, g_ref, ua_ref, ya_ref, zb_ref, yc_ref, wga_ref, wgb_ref, wgc_ref, wma_ref, wmb_ref, wmc_ref,
     d_ref, wglu_ref, bglu_ref, wa_ref, wb_ref, wc_ref, wo_ref) = refs[:19]
    refs = refs[19:]
    fg_ref = refs.pop(0) if final else None
    xo_ref, zs_ref, ys_ref = refs[:3]
    xs_ref = refs[3] if (batch_major or final) else None
    for b in range(B):
        _put_batch(zs_ref, b, B, zb_ref[:, b * W_B:(b + 1) * W_B])
        _put_batch(ys_ref, b, B, yc_ref[:, b * W_C:(b + 1) * W_C])
    mod = mod_ref[...]
    x = _load_x(x_ref, xs_ref if batch_major else None, pos_ref, B, bm)
    h = _normed(x, mod, g_ref[...], bm)
    proj = lambda w_ref: jnp.dot(h, w_ref[0], preferred_element_type=F32)
    y = _gelu_tanh(ya_ref[...] + ua_ref[...] * d_ref[...])
    y = y * jax.nn.sigmoid(jnp.dot(y.astype(BF16), wglu_ref[...], preferred_element_type=F32) + bglu_ref[...])
    y_a = (y * _silu(proj(wga_ref))).astype(BF16)
    y_b = (_get_tile(zs_ref) * _silu(proj(wgb_ref))).astype(BF16)
    y_c = (_get_tile(ys_ref) * _silu(proj(wgc_ref))).astype(BF16)
    merged = (jax.nn.sigmoid(proj(wma_ref)) * jnp.dot(y_a, wa_ref[...], preferred_element_type=F32)
              + jax.nn.sigmoid(proj(wmb_ref)) * jnp.dot(y_b, wb_ref[...], preferred_element_type=F32)
              + jax.nn.sigmoid(proj(wmc_ref)) * jnp.dot(y_c, wc_ref[...], preferred_element_type=F32))
    out = jnp.dot(merged.astype(BF16), wo_ref[...], preferred_element_type=F32)
    x_new = x + _mod_part(mod, 2, bm) * _by_batch(out, bm)
    if final:
        r = lax.rsqrt(jnp.mean(x_new * x_new, axis=-1, keepdims=True) + EPS)
        _put_tile(xs_ref, (x_new * r * fg_ref[...]).reshape(-1, D_MODEL))
        for b in range(B):
            xo_ref[b] = _get_batch(xs_ref, b, B)
    else:
        xo_ref[...] = x_new.reshape(-1, D_MODEL)


def _out_proj(x, pos, mod, mod_blk, B, bm, g, ua, ya, zb, yc, w_in, l, d, w_glu, b_glu,
              w_a, w_b, w_c, w_o, final_g, tm):
    D = D_MODEL
    rows = x.size // D
    L = rows // B
    batch_major = x.ndim == 3
    final = final_g is not None
    tok = lambda w: pl.BlockSpec((tm, w), lambda i: (i, 0))
    seq = lambda w: pl.BlockSpec((tm // B, B * w), lambda i: (i, 0))
    full = lambda a: pl.BlockSpec(a.shape, lambda i: (0,) * a.ndim)
    in_specs, args = _x_specs(x, pos, mod, mod_blk, B, bm, tm)
    in_specs += [full(g), tok(W_A), tok(W_A), seq(W_B), seq(W_C)]
    args += [g, ua, ya, zb, yc]
    for cb in G_BLOCKS:
        in_specs.append(pl.BlockSpec((1, D, CB), lambda i, cb=cb: (l, 0, cb)))
        args.append(w_in)
    for k in range(3):
        in_specs.append(pl.BlockSpec((1, D, D), lambda i, k=k: (l, 0, M_BLOCK0 + k)))
        args.append(w_in)
    weights = [d, w_glu, b_glu, w_a, w_b, w_c, w_o]
    in_specs += [full(a) for a in weights]
    args += weights
    if final:
        in_specs.append(full(final_g))
        args.append(final_g)
    scratch = [_stage(tm, W_B), _stage(tm, W_C)]
    if batch_major or final:
        scratch.append(_stage(tm, D))
    if final:
        out_spec = pl.BlockSpec((B, tm // B, D), lambda i: (0, i, 0))
        out_shape = jax.ShapeDtypeStruct((B, L, D), F32)
    else:
        out_spec = tok(D)
        out_shape = jax.ShapeDtypeStruct((rows, D), F32)
    return pl.pallas_call(
        functools.partial(_out_kernel, has_pos=pos is not None, batch_major=batch_major, final=final,
                          B=B, bm=bm),
        grid=(rows // tm,),
        in_specs=in_specs,
        out_specs=out_spec,
        out_shape=out_shape,
        scratch_shapes=scratch,
        compiler_params=_cparams(1),
        name="out_proj",
    )(*args)


def _grid_pos_embed(L):
    rows = L // GRID_W
    r = jnp.broadcast_to(jnp.arange(rows, dtype=F32)[:, None], (rows, GRID_W)).reshape(-1)
    col = jnp.broadcast_to(jnp.arange(GRID_W, dtype=F32)[None, :], (rows, GRID_W)).reshape(-1)
    q = D_MODEL // 4
    omega = 1.0 / (10000.0 ** (jnp.arange(q, dtype=F32) / q))
    ar = r[:, None] * omega[None, :]
    ac = col[:, None] * omega[None, :]
    return jnp.concatenate([jnp.sin(ar), jnp.cos(ar), jnp.sin(ac), jnp.cos(ac)], axis=-1)


def _states_to_lanes(st_re, st_im):
    s = jnp.concatenate([st_re[:, 0], st_re[:, 1], st_im[:, 0], st_im[:, 1]], axis=-1)
    return s.transpose(1, 0, 2)


def _lanes_to_states(fin):
    f = fin.transpose(1, 0, 2).reshape(fin.shape[1], S5_G, 4, S5_P)
    return (jnp.stack([f[:, :, 0], f[:, :, 1]], axis=1), jnp.stack([f[:, :, 2], f[:, :, 3]], axis=1))


def kernel(x_prompt, x_sample, c, state_s5_re, state_s5_im, c_ctx, norm_g, w_mod, b_mod, w_in, s5_lam_re, s5_lam_im, s5_log_dt, s5_b_re, s5_b_im, s5_c_re, s5_c_im, s5_d, s5_w_glu, s5_b_glu, hy_conv_w, hy_conv_b, hy_f_w1, hy_f_b1, hy_f_w2, hy_f_b2, hy_f_freq, hy_f_w3, hy_bias, pool_w, pool_scale, w_br_a, w_br_b, w_br_c, w_out, final_g):
    Bc, Lc, D = x_prompt.shape
    Bl, Ll, _ = x_sample.shape
    assert Bl == MOD_ROWS - MOD_LAT0

    cond = jnp.zeros((MOD_ROWS, D), F32).at[0].set(c_ctx).at[MOD_LAT0:].set(c)
    mod = _modulation(cond, w_mod, b_mod).reshape(DEPTH * MOD_ROWS, 3 * D)

    pos = _grid_pos_embed(Ll)
    groups = {
        'ctx': dict(B=Bc, L=Lc, S=2, C=W_B, bm=1, mod_blk=0),
        'lat': dict(B=Bl, L=Ll, S=4, C=W_B // 2, bm=Bl, mod_blk=1),
    }
    tables = {k: _dft_tables(v['L'] // v['S']) for k, v in groups.items()}
    w_in_b = w_in.astype(BF16)
    xs = {'ctx': x_prompt, 'lat': x_sample}
    new_re, new_im = [], []
    for l in range(DEPTH):
        s5_ops = _s5_operators(s5_lam_re[l], s5_lam_im[l], s5_log_dt[l], s5_b_re[l], s5_b_im[l],
                               s5_c_re[l], s5_c_im[l])
        g = norm_g[l].reshape(1, D)
        for name, cfg in groups.items():
            B, L, bm = cfg['B'], cfg['L'], cfg['bm']
            x = xs[name]
            p = pos if (name == 'lat' and l == 0) else None
            mod_blk = l * (MOD_ROWS // MOD_LAT0) + cfg['mod_blk']
            ua, ut = _in_proj(x, p, mod, mod_blk, B, bm, g, w_in_b, l, min(512, L * B))
            if name == 'ctx':
                h0g = jnp.zeros((S5_G, B, 4 * S5_P), F32)
            else:
                h0g = _states_to_lanes(state_s5_re[:, l], state_s5_im[:, l])
            ya, fin = _s5_mix(ua, s5_ops, h0g, B)
            if name == 'ctx':
                fr, fi = _lanes_to_states(fin)
                new_re.append(fr)
                new_im.append(fi)
            fs, fst = tables[name]
            filt = _hyena_filters(L, hy_f_w1[l], hy_f_b1[l], hy_f_w2[l], hy_f_b2[l], hy_f_freq[l], hy_f_w3[l])
            hw = _filter_spectrum(fs, filt.reshape(L, 2 * W_B).astype(BF16), cfg['S'], cfg['C'])
            zb = _hyena_mix(ut, B, hy_conv_w[l], hy_conv_b[l], hy_bias[l], fs, fst, hw, cfg['C'])
            yc = _pool_mix(ut, B, pool_w[l].astype(BF16), pool_scale[l])
            fg = final_g.reshape(1, D) if l == DEPTH - 1 else None
            xs[name] = _out_proj(x, p, mod, mod_blk, B, bm, g, ua, ya, zb, yc, w_in_b, l,
                                 s5_d[l].reshape(1, W_A), s5_w_glu[l].astype(BF16), s5_b_glu[l].reshape(1, W_A),
                                 w_br_a[l].astype(BF16), w_br_b[l].astype(BF16), w_br_c[l].astype(BF16),
                                 w_out[l].astype(BF16), fg, min(256, L * B))
    return (xs['ctx'], xs['lat'], jnp.stack(new_re, axis=1), jnp.stack(new_im, axis=1))
```

```python
import functools
import math

import jax
import jax.numpy as jnp
from jax import lax
from jax.experimental import pallas as pl
from jax.experimental.pallas import tpu as pltpu

F32 = jnp.float32
BF16 = jnp.bfloat16
HIGHEST = lax.Precision.HIGHEST

D_MODEL = 1024
DEPTH = 2
GRID_W = 64
EPS = 1e-6
W_A = D_MODEL // 2
S5_H = 16
S5_G = W_A // S5_H
S5_P = 64
W_B = D_MODEL // 2
HY_BANDS = 16
HY_FAST_PCT = 0.3
HY_SLOW_PCT = 1.5
HY_TARGET = 1e-2
HY_SHIFT = 0.05
W_C = D_MODEL // 2
POOL_WINDOWS = (2, 4, 8, 16)
POOL_G = W_C // 4
LANES = 128
CB = 512
U_BLOCKS = (0, 2, 3, 4, 6)
G_BLOCKS = (1, 5, 7)
M_BLOCK0 = 4
UT_COLS = CB * (len(U_BLOCKS) - 1)
S5_CHUNK = 16
S5_W = S5_CHUNK * S5_H
S5_GB = LANES // S5_H
MOD_ROWS = 16
MOD_LAT0 = 8
VMEM_LIMIT = 56 * 1024 * 1024
TM_IN = 1024
TM_OUT = 256


def _const_spec(block_shape, index):
    return pl.BlockSpec(block_shape, lambda *_: index, pipeline_mode=pl.Buffered(1))


def _cparams(n_grid):
    return pltpu.CompilerParams(dimension_semantics=("arbitrary",) * n_grid,
                                vmem_limit_bytes=VMEM_LIMIT)


def _mod_kernel(c_ref, w_ref, b_ref, o_ref):
    c = c_ref[...]
    s = c * jax.nn.sigmoid(c)
    o_ref[0] = jnp.dot(s, w_ref[0], preferred_element_type=F32, precision=HIGHEST) + b_ref[0]


def _modulation(cond, w_mod, b_mod, tn=512):
    n = 3 * D_MODEL
    return pl.pallas_call(
        _mod_kernel,
        grid=(DEPTH, n // tn),
        in_specs=[pl.BlockSpec((MOD_ROWS, D_MODEL), lambda l, j: (0, 0)),
                  pl.BlockSpec((1, D_MODEL, tn), lambda l, j: (l, 0, j)),
                  pl.BlockSpec((1, 1, tn), lambda l, j: (l, 0, j))],
        out_specs=pl.BlockSpec((1, MOD_ROWS, tn), lambda l, j: (l, 0, j)),
        out_shape=jax.ShapeDtypeStruct((DEPTH, MOD_ROWS, n), F32),
        compiler_params=_cparams(2),
        name="modulation",
    )(cond, w_mod, b_mod.reshape(DEPTH, 1, n))


def _by_batch(x, bm):
    return x if bm == 1 else x.reshape(x.shape[0] // bm, bm, x.shape[1])


def _mod_part(mod, i, bm):
    m = mod[:bm, i * D_MODEL:(i + 1) * D_MODEL]
    return m if bm == 1 else m[None]


def _stage(rows, w):
    return pltpu.VMEM((w // LANES, rows, LANES), F32)


def _put_batch(ref, b, B, val):
    for c in range(ref.shape[0]):
        ref[c, pl.ds(b, val.shape[0], stride=B), :] = val[:, c * LANES:(c + 1) * LANES]


def _get_batch(ref, b, B):
    tt = ref.shape[1] // B
    return jnp.concatenate([ref[c, pl.ds(b, tt, stride=B), :] for c in range(ref.shape[0])], axis=1)


def _put_tile(ref, val):
    for c in range(ref.shape[0]):
        ref[c] = val[:, c * LANES:(c + 1) * LANES]


def _get_tile(ref):
    return jnp.concatenate([ref[c] for c in range(ref.shape[0])], axis=1)


def _load_x(x_ref, xs_ref, pos_ref, B, bm):
    if xs_ref is not None:
        for b in range(B):
            _put_batch(xs_ref, b, B, x_ref[b])
        x2 = _get_tile(xs_ref)
    else:
        x2 = x_ref[...]
    x = _by_batch(x2, bm)
    if pos_ref is not None:
        x = x + pos_ref[...][:, None, :]
    return x


def _normed(x, mod, g, bm):
    r = lax.rsqrt(jnp.mean(x * x, axis=-1, keepdims=True) + EPS)
    h = (x * r * g) * (1.0 + _mod_part(mod, 1, bm)) + _mod_part(mod, 0, bm)
    return h.reshape(-1, D_MODEL).astype(BF16)


def _in_kernel(*refs, has_pos, batch_major, B, bm):
    refs = list(refs)
    x_ref = refs.pop(0)
    pos_ref = refs.pop(0) if has_pos else None
    mod_ref, g_ref = refs[:2]
    w_refs = refs[2:2 + len(U_BLOCKS)]
    oa_ref, ot_ref, scr_ref = refs[2 + len(U_BLOCKS):5 + len(U_BLOCKS)]
    xs_ref = refs[-1] if batch_major else None
    h = _normed(_load_x(x_ref, xs_ref, pos_ref, B, bm), mod_ref[...], g_ref[...], bm)
    oa_ref[...] = jnp.dot(h, w_refs[0][0], preferred_element_type=F32)
    for i, w_ref in enumerate(w_refs[1:]):
        _put_tile(scr_ref, jnp.dot(h, w_ref[0], preferred_element_type=F32))
        for b in range(B):
            ot_ref[:, b * UT_COLS + i * CB:b * UT_COLS + (i + 1) * CB] = _get_batch(scr_ref, b, B)


def _x_specs(x, pos, mod, mod_blk, B, bm, tm):
    D = D_MODEL
    if x.ndim == 3:
        specs = [pl.BlockSpec((B, tm // B, D), lambda i: (0, i, 0))]
    else:
        specs = [pl.BlockSpec((tm, D), lambda i: (i, 0))]
    args = [x]
    if pos is not None:
        specs.append(pl.BlockSpec((tm // bm, D), lambda i: (i, 0)))
        args.append(pos)
    specs.append(pl.BlockSpec((MOD_LAT0, 3 * D), lambda i: (mod_blk, 0)))
    args.append(mod)
    return specs, args


def _in_proj(x, pos, mod, mod_blk, B, bm, g, w_in, l, tm):
    D = D_MODEL
    rows = x.size // D
    L = rows // B
    batch_major = x.ndim == 3
    in_specs, args = _x_specs(x, pos, mod, mod_blk, B, bm, tm)
    in_specs.append(_const_spec((1, D), (0, 0)))
    args.append(g)
    for cb in U_BLOCKS:
        in_specs.append(_const_spec((1, D, CB), (l, 0, cb)))
        args.append(w_in)
    scratch = [_stage(tm, CB)] + ([_stage(tm, D)] if batch_major else [])
    return pl.pallas_call(
        functools.partial(_in_kernel, has_pos=pos is not None, batch_major=batch_major, B=B, bm=bm),
        grid=(rows // tm,),
        in_specs=in_specs,
        out_specs=[pl.BlockSpec((tm, W_A), lambda i: (i, 0)),
                   pl.BlockSpec((tm // B, B * UT_COLS), lambda i: (i, 0))],
        out_shape=[jax.ShapeDtypeStruct((rows, W_A), F32),
                   jax.ShapeDtypeStruct((L, B * UT_COLS), F32)],
        scratch_shapes=scratch,
        compiler_params=_cparams(1),
        name="in_proj",
    )(*args)


S5_RB = 64
S5_GS = 4


def _s5_kernel(u_ref, m_ref, p_ref, qt_ref, a_ref, h0_ref, y_ref, fin_ref,
               x_ref, yall_ref, sloc_ref, sinf_ref, sinb_ref, *, nc, B):
    rows = nc * B
    cpb = S5_RB // B
    half = S5_W // 2
    lane_grp = lax.broadcasted_iota(jnp.int32, (S5_RB, LANES), 1) // S5_H

    lane = lax.broadcasted_iota(jnp.int32, (B, half), 1)
    is_fwd = lane < S5_P
    lane_full = lax.broadcasted_iota(jnp.int32, (rows, S5_W), 1)
    is_fwd_full = (lane_full % half) < S5_P

    def gather_step(rb, _, g0):
        c0 = pl.multiple_of(rb * cpb, cpb)
        r0 = pl.multiple_of(rb * S5_RB, S5_RB)
        slabs = [u_ref[pl.ds(c0, cpb), t * B:(t + 1) * B, :].reshape(S5_RB, LANES) for t in range(S5_CHUNK)]
        rolled = [s if t % S5_GB == 0 else pltpu.roll(s, (t % S5_GB) * S5_H, 1) for t, s in enumerate(slabs)]
        for gi in range(S5_GS):
            g8 = g0 + gi
            for hf in range(2):
                acc = rolled[hf * S5_GB]
                for k in range(1, S5_GB):
                    acc = jnp.where(lane_grp == (k + g8) % S5_GB, rolled[hf * S5_GB + k], acc)
                x_ref[gi, pl.ds(r0, S5_RB), hf * LANES:(hf + 1) * LANES] = acc.astype(BF16)
        return 0

    def group_step(gi, _, g0):
        g8 = g0 + gi
        u = x_ref[gi]
        sloc_ref[...] = jnp.dot(u, p_ref[g8], preferred_element_type=F32)
        a = a_ref[g8]
        ar = a[:, :half]
        ai = a[:, half:]

        def body(i, carry):
            cr, ci = carry
            jf = pl.multiple_of(i * B, B)
            jb = pl.multiple_of((nc - 1 - i) * B, B)
            sinf_ref[pl.ds(jf, B), :half] = cr
            sinf_ref[pl.ds(jf, B), half:] = ci
            sinb_ref[pl.ds(jb, B), :half] = cr
            sinb_ref[pl.ds(jb, B), half:] = ci
            lf = sloc_ref[pl.ds(jf, B), :]
            lb = sloc_ref[pl.ds(jb, B), :]
            lr = jnp.where(is_fwd, lf[:, :half], lb[:, :half])
            li = jnp.where(is_fwd, lf[:, half:], lb[:, half:])
            return (ar * cr - ai * ci + lr, ar * ci + ai * cr + li)

        h0 = h0_ref[g8]
        fr, fi = lax.fori_loop(0, nc, body, (h0[:, :half], h0[:, half:]))
        fin_ref[g8, :, :half] = fr
        fin_ref[g8, :, half:] = fi
        s_in = jnp.where(is_fwd_full, sinf_ref[...], sinb_ref[...]).astype(BF16)
        yall_ref[gi] = (jnp.dot(u, m_ref[g8], preferred_element_type=F32)
                        + lax.dot_general(s_in, qt_ref[g8], (((1,), (1,)), ((), ())),
                                          preferred_element_type=F32))
        return 0

    def scatter_step(rb, _, g0):
        c0 = pl.multiple_of(rb * cpb, cpb)
        r0 = pl.multiple_of(rb * S5_RB, S5_RB)
        for hf in range(2):
            ys = [yall_ref[gi, pl.ds(r0, S5_RB), hf * LANES:(hf + 1) * LANES] for gi in range(S5_GS)]
            for k in range(S5_GB):
                acc = ys[0]
                for gi in range(1, S5_GS):
                    acc = jnp.where(lane_grp == (k + g0 + gi) % S5_GB, ys[gi], acc)
                if k:
                    acc = pltpu.roll(acc, (S5_GB - k) * S5_H, 1)
                t = hf * S5_GB + k
                dst = (pl.ds(c0, cpb), slice(t * B, (t + 1) * B), slice(None))
                if g0 > 0:
                    acc = jnp.where(lane_grp >= g0, acc, y_ref[dst].reshape(S5_RB, LANES))
                y_ref[dst] = acc.reshape(cpb, B, LANES)
        return 0

    for g0 in range(0, S5_GB, S5_GS):
        lax.fori_loop(0, rows // S5_RB, functools.partial(gather_step, g0=g0), 0)
        lax.fori_loop(0, S5_GS, functools.partial(group_step, g0=g0), 0)
        lax.fori_loop(0, rows // S5_RB, functools.partial(scatter_step, g0=g0), 0)


def _s5_mix(ua, ops, h0, B):
    m, p, q, a16 = ops
    rows_all = ua.shape[0]
    nc = rows_all // (S5_CHUNK * B)
    rows = nc * B
    up3 = ua.reshape(nc, S5_CHUNK * B, W_A)
    gblk = lambda r: pl.BlockSpec((S5_GB, r, S5_W), lambda j: (j, 0, 0))
    tok = pl.BlockSpec((nc, S5_CHUNK * B, LANES), lambda j: (0, 0, j))
    y, fin = pl.pallas_call(
        functools.partial(_s5_kernel, nc=nc, B=B),
        grid=(W_A // LANES,),
        in_specs=[tok, gblk(S5_W), gblk(S5_W), gblk(S5_W), gblk(1), gblk(B)],
        out_specs=[tok, gblk(B)],
        out_shape=[jax.ShapeDtypeStruct((nc, S5_CHUNK * B, W_A), F32),
                   jax.ShapeDtypeStruct((S5_G, B, S5_W), F32)],
        scratch_shapes=[pltpu.VMEM((S5_GS, rows, S5_W), BF16), pltpu.VMEM((S5_GS, rows, S5_W), F32),
                        pltpu.VMEM((rows, S5_W), F32), pltpu.VMEM((rows, S5_W), F32),
                        pltpu.VMEM((rows, S5_W), F32)],
        compiler_params=_cparams(1),
        name="s5_mix",
    )(up3, m, p, q, a16, h0)
    return y.reshape(rows_all, W_A), fin


def _s5_ops_kernel(lr_ref, li_ref, ldt_ref, br_ref, bi_ref, cr_ref, ci_ref, m_ref, p_ref, qt_ref, a_ref):
    T = S5_CHUNK
    H = S5_H
    lam_re = lr_ref[0]
    lam_im = li_ref[0]
    dt = jnp.exp(ldt_ref[0])
    mag = jnp.exp(lam_re * dt)
    ang = lam_im * dt
    a_re = mag * jnp.cos(ang)
    a_im = mag * jnp.sin(ang)
    n_re = a_re - 1.0
    n_im = a_im
    den = lam_re * lam_re + lam_im * lam_im
    k_re = (n_re * lam_re + n_im * lam_im) / den
    k_im = (n_im * lam_re - n_re * lam_im) / den
    b_re = br_ref[0]
    b_im = bi_ref[0]
    bb_re = k_re * b_re - k_im * b_im
    bb_im = k_re * b_im + k_im * b_re
    c_re = cr_ref[0]
    c_im = ci_ref[0]
    ap = [(jnp.ones_like(a_re), jnp.zeros_like(a_re))]
    for _ in range(T):
        pr, pi = ap[-1]
        ap.append((pr * a_re - pi * a_im, pr * a_im + pi * a_re))
    is_fwd = lax.broadcasted_iota(jnp.int32, (1, LANES), 1) < S5_P

    def powers(kf, kb):
        return jnp.where(is_fwd, ap[kf][0], ap[kb][0]), jnp.where(is_fwd, ap[kf][1], ap[kb][1])

    g8 = pl.program_id(0) % S5_GB

    def pos_rows(t):
        p = (t // S5_GB) * S5_GB + (t % S5_GB + g8) % S5_GB
        return pl.ds(pl.multiple_of(p * H, H), H)

    for t in range(T):
        rows = pos_rows(t)
        er, ei = powers(T - 1 - t, t)
        p_ref[0, rows, :LANES] = (er * bb_re - ei * bb_im).astype(BF16)
        p_ref[0, rows, LANES:] = (er * bb_im + ei * bb_re).astype(BF16)
        er, ei = powers(t + 1, T - t)
        qt_ref[0, rows, :LANES] = (c_re * er - c_im * ei).astype(BF16)
        qt_ref[0, rows, LANES:] = (-(c_re * ei + c_im * er)).astype(BF16)
    a_ref[0, :, :LANES] = ap[T][0]
    a_ref[0, :, LANES:] = ap[T][1]

    def ca(k):
        return c_re * ap[k][0] - c_im * ap[k][1], c_re * ap[k][1] + c_im * ap[k][0]

    fwd = [ca(k) for k in range(T)]
    bwd = fwd[::-1]
    nt = lambda a, b: lax.dot_general(a, b, (((1,), (1,)), ((), ())), precision=HIGHEST,
                                      preferred_element_type=F32)
    cat = lambda parts, i: jnp.concatenate([p[i] for p in parts], axis=0)
    zero = jnp.zeros_like(bb_re)
    wf = (nt(jnp.where(is_fwd, bb_re, zero), cat(fwd, 0)) - nt(jnp.where(is_fwd, bb_im, zero), cat(fwd, 1)))
    wb = (nt(jnp.where(is_fwd, zero, bb_re), cat(bwd, 0)) - nt(jnp.where(is_fwd, zero, bb_im), cat(bwd, 1)))
    pad = jnp.zeros((H, S5_W), F32)
    wf_pad = jnp.concatenate([pad, wf], axis=1)
    wb_pad = jnp.concatenate([wb, pad], axis=1)
    for t in range(T):
        row = (wf_pad[:, S5_W - H * t:2 * S5_W - H * t] + wb_pad[:, (T - 1 - t) * H:(T - 1 - t) * H + S5_W])
        row = jnp.concatenate([pltpu.roll(row[:, :LANES], g8 * H, 1), pltpu.roll(row[:, LANES:], g8 * H, 1)],
                              axis=1)
        m_ref[0, pos_rows(t), :] = row.astype(BF16)


def _s5_operators(lam_re, lam_im, log_dt, b_re, b_im, c_re, c_im):
    G = S5_G
    dirs = lambda x: jnp.concatenate([x[0], x[1]], axis=-1)
    lam = [dirs(x)[:, None, :] for x in (lam_re, lam_im)]
    ldt = dirs(jnp.broadcast_to(log_dt[..., None], (2, G, S5_P)))[:, None, :]
    bt = [dirs(x.transpose(0, 1, 3, 2)) for x in (b_re, b_im)]
    ct = [dirs(x) for x in (c_re, c_im)]
    vec = pl.BlockSpec((1, 1, LANES), lambda g: (g, 0, 0))
    mat = pl.BlockSpec((1, S5_H, LANES), lambda g: (g, 0, 0))
    op = pl.BlockSpec((1, S5_W, S5_W), lambda g: (g, 0, 0))
    return pl.pallas_call(
        _s5_ops_kernel,
        grid=(G,),
        in_specs=[vec, vec, vec, mat, mat, mat, mat],
        out_specs=[op, op, op, pl.BlockSpec((1, 1, S5_W), lambda g: (g, 0, 0))],
        out_shape=[jax.ShapeDtypeStruct((G, S5_W, S5_W), BF16)] * 3
        + [jax.ShapeDtypeStruct((G, 1, S5_W), F32)],
        compiler_params=_cparams(1),
        name="s5_operators",
    )(*lam, ldt, *bt, *ct)


def _phase_tables(u, va, vb, ncb, n):
    def trig(m):
        m = m & (2 * n - 1)
        m = jnp.where(m >= n, m - 2 * n, m)
        ang = m.astype(F32) * (math.pi / n)
        return jnp.cos(ang), jnp.sin(ang)

    lane = lax.broadcasted_iota(jnp.int32, (1, LANES), 1)
    cb, sb = trig(u * (va * lane + vb))
    ca_all, sa_all = trig(u * (va * LANES * lane))
    out = []
    for ch in range(ncb):
        ca = ca_all[:, ch:ch + 1]
        sa = sa_all[:, ch:ch + 1]
        out.append((ca * cb - sa * sb, sa * cb + ca * sb))
    return out


def _tables_kernel(fs_ref, fst_ref, blk_ref, *, Ls):
    u = 2 * lax.broadcasted_iota(jnp.int32, (Ls, 1), 0) + 1
    for ch, (c, s) in enumerate(_phase_tables(u, 1, 0, Ls // LANES, 2 * Ls)):
        blk_ref[:Ls, ch * LANES:(ch + 1) * LANES] = c
        blk_ref[Ls:, ch * LANES:(ch + 1) * LANES] = -s
    blk = blk_ref[...]
    fs_ref[...] = blk.astype(BF16)
    fst_ref[...] = blk.T.astype(BF16)


def _dft_tables(Ls):
    return pl.pallas_call(
        functools.partial(_tables_kernel, Ls=Ls),
        out_shape=[jax.ShapeDtypeStruct((2 * Ls, Ls), BF16), jax.ShapeDtypeStruct((Ls, 2 * Ls), BF16)],
        scratch_shapes=[pltpu.VMEM((2 * Ls, Ls), F32)],
        compiler_params=pltpu.CompilerParams(vmem_limit_bytes=VMEM_LIMIT),
        name="dft_tables",
    )()


def _spectrum_kernel(fs_ref, f_ref, w_ref, *, S, Ls):
    scale = 1.0 / Ls
    k = lax.broadcasted_iota(jnp.int32, (Ls, 1), 0)
    sgn = jnp.where(k % 2 == 0, 1.0, -1.0)
    prev = None
    for i in range(S + 1):
        re = im = None
        if i < S:
            g = jnp.dot(fs_ref[...], f_ref[i * Ls:(i + 1) * Ls, :], preferred_element_type=F32) * scale
            re, im = g[:Ls], g[Ls:]
        cur = (re, im)
        if prev is not None:
            jre, jim = -sgn * prev[1], sgn * prev[0]
            re = jre if re is None else re + jre
            im = jim if im is None else im + jim
        w_ref[0, i, :Ls] = re
        w_ref[0, i, Ls:] = im
        prev = cur


def _filter_spectrum(fs, filt, S, C):
    n2, Ls = fs.shape
    L = filt.shape[0]
    nct = W_B // C
    return pl.pallas_call(
        functools.partial(_spectrum_kernel, S=S, Ls=Ls),
        grid=(2, nct),
        in_specs=[pl.BlockSpec((n2, Ls), lambda o, ct: (0, 0)),
                  pl.BlockSpec((L, C), lambda o, ct: (0, o * nct + ct))],
        out_specs=pl.BlockSpec((1, S + 1, n2, C), lambda o, ct: (o, 0, 0, ct)),
        out_shape=jax.ShapeDtypeStruct((2, S + 1, n2, W_B), F32),
        compiler_params=_cparams(2),
        name="filter_spectrum",
    )(fs, filt)


def _hyena_filters(L, w1, b1, w2, b2, freq, w3):
    t = jnp.arange(L, dtype=F32)
    tn = t / (L - 1)
    f = jnp.linspace(1e-4, HY_BANDS - 1, HY_BANDS, dtype=F32)
    ang = (2.0 * math.pi / L) * t[:, None] * f[None, :]
    z = jnp.concatenate([tn[:, None], jnp.cos(ang), -jnp.sin(ang)], axis=-1)
    h = jnp.sin(freq * (jnp.dot(z, w1, precision=HIGHEST) + b1))
    h = jnp.sin(freq * (jnp.dot(h, w2, precision=HIGHEST) + b2))
    h = jnp.dot(h, w3, precision=HIGHEST).reshape(L, 2, W_B)
    max_decay = math.log(HY_TARGET) / HY_FAST_PCT
    min_decay = math.log(HY_TARGET) / HY_SLOW_PCT
    deltas = jnp.abs(jnp.linspace(min_decay, max_decay, W_B, dtype=F32))
    half = L // 2
    off = jnp.abs(t - half) / half
    win = jnp.exp(-off[:, None] * deltas[None, :]) + HY_SHIFT
    return h * win[:, None, :]


def _shift_rows(u, s, row):
    L = u.shape[0]
    r = pltpu.roll(u, (-s) % L, 0)
    return jnp.where((row + s >= 0) & (row + s < L), r, 0.0)


HY_RF = 32
SUBLANES = 8


def _short_conv_rows(u_ref, w, bias, c, R, L):
    r0 = pl.multiple_of(c * R, R)
    cur = u_ref[pl.ds(r0, R), :]
    before = u_ref[pl.ds(pl.multiple_of(jnp.maximum(r0 - SUBLANES, 0), SUBLANES), SUBLANES), :]
    after = u_ref[pl.ds(pl.multiple_of(jnp.minimum(r0 + R, L - SUBLANES), SUBLANES), SUBLANES), :]
    before = jnp.where(c == 0, 0.0, before)
    after = jnp.where(c == L // R - 1, 0.0, after)
    ext = jnp.concatenate([before, cur, after], axis=0)
    n = R + 2 * SUBLANES
    up = pltpu.roll(ext, 1, 0)[SUBLANES:SUBLANES + R]
    dn = pltpu.roll(ext, n - 1, 0)[SUBLANES:SUBLANES + R]
    return bias + up * w[0:1] + cur * w[1:2] + dn * w[2:3]


def _hy_kernel(v_ref, x_ref, cwv_ref, cbv_ref, cwx_ref, cbx_ref, hb_ref, fs_ref, fst_ref, w_ref, o_ref,
               zf_ref, zb_ref, u_ref, d_ref, os_ref, *, L, S, R):
    Ls = L // S
    C = zf_ref.shape[1]
    o = pl.program_id(2)
    seg = lambda j: slice(j * C, (j + 1) * C)

    @pl.when(o == 0)
    def _():
        for j in range(S):
            def body(cc, _, j=j):
                c = j * (Ls // R) + cc
                z = _short_conv_rows(v_ref, cwv_ref[...], cbv_ref[...], c, R, L)
                zf_ref[pl.ds(pl.multiple_of(c * R, R), R), :] = z
                zb_ref[pl.ds(pl.multiple_of(cc * R, R), R), seg(j)] = z.astype(BF16)
                return 0
            lax.fori_loop(0, Ls // R, body, 0)

    u_ref[...] = jnp.dot(fs_ref[...], zb_ref[...], preferred_element_type=F32)

    def prod(fc, _):
        f0 = pl.multiple_of(fc * HY_RF, HY_RF)
        re_rows = pl.ds(f0, HY_RF)
        im_rows = pl.ds(Ls + f0, HY_RF)
        for r in range(S):
            dre = dim = None
            for j in range(S):
                d = r - j + S // 2
                if 0 <= d <= S:
                    ur, ui = u_ref[re_rows, seg(j)], u_ref[im_rows, seg(j)]
                    wr, wi = w_ref[0, d, re_rows, :], w_ref[0, d, im_rows, :]
                    pre, pim = ur * wr - ui * wi, ur * wi + ui * wr
                    dre = pre if dre is None else dre + pre
                    dim = pim if dim is None else dim + pim
            d_ref[re_rows, seg(r)] = dre.astype(BF16)
            d_ref[im_rows, seg(r)] = dim.astype(BF16)
        return 0
    lax.fori_loop(0, Ls // HY_RF, prod, 0)

    os_ref[...] = jnp.dot(fst_ref[...], d_ref[...], preferred_element_type=F32)

    for r in range(S):
        def gate(cc, _, r=r):
            c = r * (Ls // R) + cc
            rows = pl.ds(pl.multiple_of(c * R, R), R)
            local = pl.ds(pl.multiple_of(cc * R, R), R)
            x = _short_conv_rows(x_ref, cwx_ref[...], cbx_ref[...], c, R, L)
            z = x * (os_ref[local, seg(r)] + zf_ref[rows, :] * hb_ref[0])

            @pl.when(o == 0)
            def _():
                zf_ref[rows, :] = z
                zb_ref[local, seg(r)] = z.astype(BF16)

            @pl.when(o == 1)
            def _():
                o_ref[rows, :] = z
            return 0
        lax.fori_loop(0, Ls // R, gate, 0)


def _hyena_mix(ut, B, conv_w, conv_b, bias, fs, fst, w, C):
    L = ut.shape[0]
    n2, Ls = fs.shape
    S = L // Ls
    nct = W_B // C
    ncol = UT_COLS // C
    xcol = lambda o, ct: (1 + o) * nct + ct
    conv_b = conv_b.reshape(1, -1)
    R = min(128, Ls)
    return pl.pallas_call(
        functools.partial(_hy_kernel, L=L, S=S, R=R),
        grid=(B, nct, 2),
        in_specs=[pl.BlockSpec((L, C), lambda b, ct, o: (0, b * ncol + ct)),
                  pl.BlockSpec((L, C), lambda b, ct, o: (0, b * ncol + xcol(o, ct))),
                  pl.BlockSpec((3, C), lambda b, ct, o: (0, ct)),
                  pl.BlockSpec((1, C), lambda b, ct, o: (0, ct)),
                  pl.BlockSpec((3, C), lambda b, ct, o: (0, xcol(o, ct))),
                  pl.BlockSpec((1, C), lambda b, ct, o: (0, xcol(o, ct))),
                  pl.BlockSpec((1, 1, C), lambda b, ct, o: (o, 0, ct)),
                  pl.BlockSpec((n2, Ls), lambda b, ct, o: (0, 0)),
                  pl.BlockSpec((Ls, n2), lambda b, ct, o: (0, 0)),
                  pl.BlockSpec((1, S + 1, n2, C), lambda b, ct, o: (o, 0, 0, ct))],
        out_specs=pl.BlockSpec((L, C), lambda b, ct, o: (0, b * nct + ct)),
        out_shape=jax.ShapeDtypeStruct((L, B * W_B), F32),
        scratch_shapes=[pltpu.VMEM((L, C), F32), pltpu.VMEM((Ls, S * C), BF16), pltpu.VMEM((n2, S * C), F32),
                        pltpu.VMEM((n2, S * C), BF16), pltpu.VMEM((Ls, S * C), F32)],
        compiler_params=_cparams(3),
        name="hyena_mix",
    )(ut, ut, conv_w, conv_b, conv_w, conv_b, bias.reshape(2, 1, W_B), fs, fst, w)


def _pool_kernel(u_ref, w_ref, sc_ref, o_ref, *, L):
    row = lax.broadcasted_iota(jnp.int32, (L, 1), 0)
    for gi, win in enumerate(POOL_WINDOWS):
        u = u_ref[:, gi * POOL_G:(gi + 1) * POOL_G]
        s = u
        for off in range(-(win // 2), win - win // 2):
            if off != 0:
                s = s + _shift_rows(u, off, row)
        lo = jnp.maximum(row - win // 2, 0)
        hi = jnp.minimum(row - win // 2 + win, L)
        d = s / (hi - lo).astype(F32) - u
        o_ref[:, gi * POOL_G:(gi + 1) * POOL_G] = (
            jnp.dot(d.astype(BF16), w_ref[gi], preferred_element_type=F32)
            * sc_ref[:, gi * POOL_G:(gi + 1) * POOL_G])


def _pool_mix(ut, B, w, scale):
    L = ut.shape[0]
    nblk = UT_COLS // W_C
    return pl.pallas_call(
        functools.partial(_pool_kernel, L=L),
        grid=(B,),
        in_specs=[pl.BlockSpec((L, W_C), lambda b: (0, b * nblk + nblk - 1)),
                  pl.BlockSpec((len(POOL_WINDOWS), POOL_G, POOL_G), lambda b: (0, 0, 0)),
                  pl.BlockSpec((1, W_C), lambda b: (0, 0))],
        out_specs=pl.BlockSpec((L, W_C), lambda b: (0, b)),
        out_shape=jax.ShapeDtypeStruct((L, B * W_C), F32),
        compiler_params=_cparams(1),
        name="pool_mix",
    )(ut, w, scale.reshape(1, W_C))


def _silu(x):
    return x * jax.nn.sigmoid(x)


def _gelu_tanh(x):
    return x * (0.5 * (1.0 + jnp.tanh(math.sqrt(2.0 / math.pi) * (x + 0.044715 * (x * x * x)))))


def _out_kernel(*refs, has_pos, batch_major, final, B, bm):
    refs = list(refs)
    x_ref = refs.pop(0)
    pos_ref = refs.pop(0) if has_pos else None
    (mod_ref, g_ref, ua_ref, ya_ref, zb_ref, yc_ref, wga_ref, wgb_ref, wgc_ref, wma_ref, wmb_ref, wmc_ref,
     d_ref, wglu_ref, bglu_ref, wa_ref, wb_ref, wc_ref, wo_ref) = refs[:19]
    refs = refs[19:]
    fg_ref = refs.pop(0) if final else None
    xo_ref, zs_ref, ys_ref = refs[:3]
    xs_ref = refs[3] if (batch_major or final) else None
    for b in range(B):
        _put_batch(zs_ref, b, B, zb_ref[:, b * W_B:(b + 1) * W_B])
        _put_batch(ys_ref, b, B, yc_ref[:, b * W_C:(b + 1) * W_C])
    mod = mod_ref[...]
    x = _load_x(x_ref, xs_ref if batch_major else None, pos_ref, B, bm)
    h = _normed(x, mod, g_ref[...], bm)
    proj = lambda w_ref: jnp.dot(h, w_ref[0], preferred_element_type=F32)
    y = _gelu_tanh(ya_ref[...] + ua_ref[...] * d_ref[...])
    y = y * jax.nn.sigmoid(jnp.dot(y.astype(BF16), wglu_ref[...], preferred_element_type=F32) + bglu_ref[...])
    y_a = (y * _silu(proj(wga_ref))).astype(BF16)
    y_b = (_get_tile(zs_ref) * _silu(proj(wgb_ref))).astype(BF16)
    y_c = (_get_tile(ys_ref) * _silu(proj(wgc_ref))).astype(BF16)
    merged = (jax.nn.sigmoid(proj(wma_ref)) * jnp.dot(y_a, wa_ref[...], preferred_element_type=F32)
              + jax.nn.sigmoid(proj(wmb_ref)) * jnp.dot(y_b, wb_ref[...], preferred_element_type=F32)
              + jax.nn.sigmoid(proj(wmc_ref)) * jnp.dot(y_c, wc_ref[...], preferred_element_type=F32))
    out = jnp.dot(merged.astype(BF16), wo_ref[...], preferred_element_type=F32)
    x_new = x + _mod_part(mod, 2, bm) * _by_batch(out, bm)
    if final:
        r = lax.rsqrt(jnp.mean(x_new * x_new, axis=-1, keepdims=True) + EPS)
        _put_tile(xs_ref, (x_new * r * fg_ref[...]).reshape(-1, D_MODEL))
        for b in range(B):
            xo_ref[b] = _get_batch(xs_ref, b, B)
    else:
        xo_ref[...] = x_new.reshape(-1, D_MODEL)


def _out_proj(x, pos, mod, mod_blk, B, bm, g, ua, ya, zb, yc, w_in, l, d, w_glu, b_glu,
              w_a, w_b, w_c, w_o, final_g, tm):
    D = D_MODEL
    rows = x.size // D
    L = rows // B
    batch_major = x.ndim == 3
    final = final_g is not None
    tok = lambda w: pl.BlockSpec((tm, w), lambda i: (i, 0))
    seq = lambda w: pl.BlockSpec((tm // B, B * w), lambda i: (i, 0))
    full = lambda a: _const_spec(a.shape, (0,) * a.ndim)
    in_specs, args = _x_specs(x, pos, mod, mod_blk, B, bm, tm)
    in_specs += [full(g), tok(W_A), tok(W_A), seq(W_B), seq(W_C)]
    args += [g, ua, ya, zb, yc]
    for cb in G_BLOCKS:
        in_specs.append(_const_spec((1, D, CB), (l, 0, cb)))
        args.append(w_in)
    for k in range(3):
        in_specs.append(_const_spec((1, D, D), (l, 0, M_BLOCK0 + k)))
        args.append(w_in)
    weights = [d, w_glu, b_glu, w_a, w_b, w_c, w_o]
    in_specs += [full(a) for a in weights]
    args += weights
    if final:
        in_specs.append(full(final_g))
        args.append(final_g)
    scratch = [_stage(tm, W_B), _stage(tm, W_C)]
    if batch_major or final:
        scratch.append(_stage(tm, D))
    if final:
        out_spec = pl.BlockSpec((B, tm // B, D), lambda i: (0, i, 0))
        out_shape = jax.ShapeDtypeStruct((B, L, D), F32)
    else:
        out_spec = tok(D)
        out_shape = jax.ShapeDtypeStruct((rows, D), F32)
    return pl.pallas_call(
        functools.partial(_out_kernel, has_pos=pos is not None, batch_major=batch_major, final=final,
                          B=B, bm=bm),
        grid=(rows // tm,),
        in_specs=in_specs,
        out_specs=out_spec,
        out_shape=out_shape,
        scratch_shapes=scratch,
        compiler_params=_cparams(1),
        name="out_proj",
    )(*args)


def _grid_pos_embed(L):
    rows = L // GRID_W
    r = jnp.broadcast_to(jnp.arange(rows, dtype=F32)[:, None], (rows, GRID_W)).reshape(-1)
    col = jnp.broadcast_to(jnp.arange(GRID_W, dtype=F32)[None, :], (rows, GRID_W)).reshape(-1)
    q = D_MODEL // 4
    omega = 1.0 / (10000.0 ** (jnp.arange(q, dtype=F32) / q))
    ar = r[:, None] * omega[None, :]
    ac = col[:, None] * omega[None, :]
    return jnp.concatenate([jnp.sin(ar), jnp.cos(ar), jnp.sin(ac), jnp.cos(ac)], axis=-1)


def _states_to_lanes(st_re, st_im):
    s = jnp.concatenate([st_re[:, 0], st_re[:, 1], st_im[:, 0], st_im[:, 1]], axis=-1)
    return s.transpose(1, 0, 2)


def _lanes_to_states(fin):
    f = fin.transpose(1, 0, 2).reshape(fin.shape[1], S5_G, 4, S5_P)
    return (jnp.stack([f[:, :, 0], f[:, :, 1]], axis=1), jnp.stack([f[:, :, 2], f[:, :, 3]], axis=1))


def kernel(x_prompt, x_sample, c, state_s5_re, state_s5_im, c_ctx, norm_g, w_mod, b_mod, w_in, s5_lam_re, s5_lam_im, s5_log_dt, s5_b_re, s5_b_im, s5_c_re, s5_c_im, s5_d, s5_w_glu, s5_b_glu, hy_conv_w, hy_conv_b, hy_f_w1, hy_f_b1, hy_f_w2, hy_f_b2, hy_f_freq, hy_f_w3, hy_bias, pool_w, pool_scale, w_br_a, w_br_b, w_br_c, w_out, final_g):
    Bc, Lc, D = x_prompt.shape
    Bl, Ll, _ = x_sample.shape
    assert Bl == MOD_ROWS - MOD_LAT0

    cond = jnp.zeros((MOD_ROWS, D), F32).at[0].set(c_ctx).at[MOD_LAT0:].set(c)
    mod = _modulation(cond, w_mod, b_mod).reshape(DEPTH * MOD_ROWS, 3 * D)

    pos = _grid_pos_embed(Ll)
    groups = {
        'ctx': dict(B=Bc, L=Lc, S=2, C=W_B, bm=1, mod_blk=0),
        'lat': dict(B=Bl, L=Ll, S=4, C=W_B // 2, bm=Bl, mod_blk=1),
    }
    tables = {k: _dft_tables(v['L'] // v['S']) for k, v in groups.items()}
    w_in_b = w_in.astype(BF16)
    xs = {'ctx': x_prompt, 'lat': x_sample}
    new_re, new_im = [], []
    for l in range(DEPTH):
        s5_ops = _s5_operators(s5_lam_re[l], s5_lam_im[l], s5_log_dt[l], s5_b_re[l], s5_b_im[l],
                               s5_c_re[l], s5_c_im[l])
        g = norm_g[l].reshape(1, D)
        for name, cfg in groups.items():
            B, L, bm = cfg['B'], cfg['L'], cfg['bm']
            x = xs[name]
            p = pos if (name == 'lat' and l == 0) else None
            mod_blk = l * (MOD_ROWS // MOD_LAT0) + cfg['mod_blk']
            ua, ut = _in_proj(x, p, mod, mod_blk, B, bm, g, w_in_b, l, min(TM_IN, L * B))
            if name == 'ctx':
                h0g = jnp.zeros((S5_G, B, 4 * S5_P), F32)
            else:
                h0g = _states_to_lanes(state_s5_re[:, l], state_s5_im[:, l])
            ya, fin = _s5_mix(ua, s5_ops, h0g, B)
            if name == 'ctx':
                fr, fi = _lanes_to_states(fin)
                new_re.append(fr)
                new_im.append(fi)
            fs, fst = tables[name]
            filt = _hyena_filters(L, hy_f_w1[l], hy_f_b1[l], hy_f_w2[l], hy_f_b2[l], hy_f_freq[l], hy_f_w3[l])
            hw = _filter_spectrum(fs, filt.reshape(L, 2 * W_B).astype(BF16), cfg['S'], cfg['C'])
            zb = _hyena_mix(ut, B, hy_conv_w[l], hy_conv_b[l], hy_bias[l], fs, fst, hw, cfg['C'])
            yc = _pool_mix(ut, B, pool_w[l].astype(BF16), pool_scale[l])
            fg = final_g.reshape(1, D) if l == DEPTH - 1 else None
            xs[name] = _out_proj(x, p, mod, mod_blk, B, bm, g, ua, ya, zb, yc, w_in_b, l,
                                 s5_d[l].reshape(1, W_A), s5_w_glu[l].astype(BF16), s5_b_glu[l].reshape(1, W_A),
                                 w_br_a[l].astype(BF16), w_br_b[l].astype(BF16), w_br_c[l].astype(BF16),
                                 w_out[l].astype(BF16), fg, min(TM_OUT, L * B))
    return (xs['ctx'], xs['lat'], jnp.stack(new_re, axis=1), jnp.stack(new_im, axis=1))
```

```python
import functools
import math

import jax
import jax.numpy as jnp
from jax import lax
from jax.experimental import pallas as pl
from jax.experimental.pallas import tpu as pltpu

F32 = jnp.float32
BF16 = jnp.bfloat16
HIGHEST = lax.Precision.HIGHEST

D_MODEL = 1024
DEPTH = 2
GRID_W = 64
EPS = 1e-6
W_A = D_MODEL // 2
S5_H = 16
S5_G = W_A // S5_H
S5_P = 64
W_B = D_MODEL // 2
HY_BANDS = 16
HY_FAST_PCT = 0.3
HY_SLOW_PCT = 1.5
HY_TARGET = 1e-2
HY_SHIFT = 0.05
W_C = D_MODEL // 2
POOL_WINDOWS = (2, 4, 8, 16)
POOL_G = W_C // 4
LANES = 128
CB = 512
U_BLOCKS = (0, 2, 3, 4, 6)
G_BLOCKS = (1, 5, 7)
M_BLOCK0 = 4
UT_COLS = CB * (len(U_BLOCKS) - 1)
S5_CHUNK = 16
S5_W = S5_CHUNK * S5_H
S5_GB = LANES // S5_H
MOD_ROWS = 16
MOD_LAT0 = 8
VMEM_LIMIT = 56 * 1024 * 1024
TM_IN = 1024
TM_OUT = 256


def _const_spec(block_shape, index):
    return pl.BlockSpec(block_shape, lambda *_: index, pipeline_mode=pl.Buffered(1))


def _cparams(n_grid):
    return pltpu.CompilerParams(dimension_semantics=("arbitrary",) * n_grid,
                                vmem_limit_bytes=VMEM_LIMIT)


def _mod_kernel(c_ref, w_ref, b_ref, o_ref):
    c = c_ref[...]
    s = c * jax.nn.sigmoid(c)
    o_ref[0] = jnp.dot(s, w_ref[0], preferred_element_type=F32, precision=HIGHEST) + b_ref[0]


def _modulation(cond, w_mod, b_mod, tn=512):
    n = 3 * D_MODEL
    return pl.pallas_call(
        _mod_kernel,
        grid=(DEPTH, n // tn),
        in_specs=[pl.BlockSpec((MOD_ROWS, D_MODEL), lambda l, j: (0, 0)),
                  pl.BlockSpec((1, D_MODEL, tn), lambda l, j: (l, 0, j)),
                  pl.BlockSpec((1, 1, tn), lambda l, j: (l, 0, j))],
        out_specs=pl.BlockSpec((1, MOD_ROWS, tn), lambda l, j: (l, 0, j)),
        out_shape=jax.ShapeDtypeStruct((DEPTH, MOD_ROWS, n), F32),
        compiler_params=_cparams(2),
        name="modulation",
    )(cond, w_mod, b_mod.reshape(DEPTH, 1, n))


def _by_batch(x, bm):
    return x if bm == 1 else x.reshape(x.shape[0] // bm, bm, x.shape[1])


def _mod_part(mod, i, bm):
    m = mod[:bm, i * D_MODEL:(i + 1) * D_MODEL]
    return m if bm == 1 else m[None]


def _stage(rows, w):
    return pltpu.VMEM((w // LANES, rows, LANES), F32)


def _put_batch(ref, b, B, val):
    for c in range(ref.shape[0]):
        ref[c, pl.ds(b, val.shape[0], stride=B), :] = val[:, c * LANES:(c + 1) * LANES]


def _get_batch(ref, b, B):
    tt = ref.shape[1] // B
    return jnp.concatenate([ref[c, pl.ds(b, tt, stride=B), :] for c in range(ref.shape[0])], axis=1)


def _put_tile(ref, val):
    for c in range(ref.shape[0]):
        ref[c] = val[:, c * LANES:(c + 1) * LANES]


def _get_tile(ref):
    return jnp.concatenate([ref[c] for c in range(ref.shape[0])], axis=1)


def _load_x(x_ref, xs_ref, pos_ref, B, bm):
    if xs_ref is not None:
        for b in range(B):
            _put_batch(xs_ref, b, B, x_ref[b])
        x2 = _get_tile(xs_ref)
    else:
        x2 = x_ref[...]
    x = _by_batch(x2, bm)
    if pos_ref is not None:
        x = x + pos_ref[...][:, None, :]
    return x


def _normed(x, mod, g, bm):
    r = lax.rsqrt(jnp.mean(x * x, axis=-1, keepdims=True) + EPS)
    h = (x * r * g) * (1.0 + _mod_part(mod, 1, bm)) + _mod_part(mod, 0, bm)
    return h.reshape(-1, D_MODEL).astype(BF16)


def _in_kernel(*refs, has_pos, batch_major, B, bm):
    refs = list(refs)
    x_ref = refs.pop(0)
    pos_ref = refs.pop(0) if has_pos else None
    mod_ref, g_ref = refs[:2]
    w_refs = refs[2:2 + len(U_BLOCKS)]
    oa_ref, ot_ref, h_ref, scr_ref = refs[2 + len(U_BLOCKS):6 + len(U_BLOCKS)]
    xs_ref = refs[-1] if batch_major else None
    h = _normed(_load_x(x_ref, xs_ref, pos_ref, B, bm), mod_ref[...], g_ref[...], bm)
    h_ref[...] = h
    oa_ref[...] = jnp.dot(h, w_refs[0][0], preferred_element_type=F32)
    for i, w_ref in enumerate(w_refs[1:]):
        _put_tile(scr_ref, jnp.dot(h, w_ref[0], preferred_element_type=F32))
        for b in range(B):
            ot_ref[:, b * UT_COLS + i * CB:b * UT_COLS + (i + 1) * CB] = _get_batch(scr_ref, b, B)


def _x_specs(x, pos, mod, mod_blk, B, bm, tm):
    D = D_MODEL
    if x.ndim == 3:
        specs = [pl.BlockSpec((B, tm // B, D), lambda i: (0, i, 0))]
    else:
        specs = [pl.BlockSpec((tm, D), lambda i: (i, 0))]
    args = [x]
    if pos is not None:
        specs.append(pl.BlockSpec((tm // bm, D), lambda i: (i, 0)))
        args.append(pos)
    specs.append(_const_spec((MOD_LAT0, 3 * D), (mod_blk, 0)))
    args.append(mod)
    return specs, args


def _in_proj(x, pos, mod, mod_blk, B, bm, g, w_in, l, tm):
    D = D_MODEL
    rows = x.size // D
    L = rows // B
    batch_major = x.ndim == 3
    in_specs, args = _x_specs(x, pos, mod, mod_blk, B, bm, tm)
    in_specs.append(_const_spec((1, D), (0, 0)))
    args.append(g)
    for cb in U_BLOCKS:
        in_specs.append(_const_spec((1, D, CB), (l, 0, cb)))
        args.append(w_in)
    scratch = [_stage(tm, CB)] + ([_stage(tm, D)] if batch_major else [])
    return pl.pallas_call(
        functools.partial(_in_kernel, has_pos=pos is not None, batch_major=batch_major, B=B, bm=bm),
        grid=(rows // tm,),
        in_specs=in_specs,
        out_specs=[pl.BlockSpec((tm, W_A), lambda i: (i, 0)),
                   pl.BlockSpec((tm // B, B * UT_COLS), lambda i: (i, 0)),
                   pl.BlockSpec((tm, D), lambda i: (i, 0))],
        out_shape=[jax.ShapeDtypeStruct((rows, W_A), F32),
                   jax.ShapeDtypeStruct((L, B * UT_COLS), F32),
                   jax.ShapeDtypeStruct((rows, D), BF16)],
        scratch_shapes=scratch,
        compiler_params=_cparams(1),
        name="in_proj",
    )(*args)


S5_RB = 64
S5_GS = 4


def _s5_kernel(u_ref, m_ref, p_ref, qt_ref, a_ref, h0_ref, y_ref, fin_ref,
               x_ref, yall_ref, sloc_ref, sinf_ref, sinb_ref, *, nc, B):
    rows = nc * B
    cpb = S5_RB // B
    half = S5_W // 2
    lane_grp = lax.broadcasted_iota(jnp.int32, (S5_RB, LANES), 1) // S5_H

    lane = lax.broadcasted_iota(jnp.int32, (B, half), 1)
    is_fwd = lane < S5_P
    lane_full = lax.broadcasted_iota(jnp.int32, (rows, S5_W), 1)
    is_fwd_full = (lane_full % half) < S5_P

    def gather_step(rb, _, g0):
        c0 = pl.multiple_of(rb * cpb, cpb)
        r0 = pl.multiple_of(rb * S5_RB, S5_RB)
        slabs = [u_ref[pl.ds(c0, cpb), t * B:(t + 1) * B, :].reshape(S5_RB, LANES) for t in range(S5_CHUNK)]
        rolled = [s if t % S5_GB == 0 else pltpu.roll(s, (t % S5_GB) * S5_H, 1) for t, s in enumerate(slabs)]
        for gi in range(S5_GS):
            g8 = g0 + gi
            for hf in range(2):
                acc = rolled[hf * S5_GB]
                for k in range(1, S5_GB):
                    acc = jnp.where(lane_grp == (k + g8) % S5_GB, rolled[hf * S5_GB + k], acc)
                x_ref[gi, pl.ds(r0, S5_RB), hf * LANES:(hf + 1) * LANES] = acc.astype(BF16)
        return 0

    def group_step(gi, _, g0):
        g8 = g0 + gi
        u = x_ref[gi]
        sloc_ref[...] = jnp.dot(u, p_ref[g8], preferred_element_type=F32)
        a = a_ref[g8]
        ar = a[:, :half]
        ai = a[:, half:]

        def body(i, carry):
            cr, ci = carry
            jf = pl.multiple_of(i * B, B)
            jb = pl.multiple_of((nc - 1 - i) * B, B)
            sinf_ref[pl.ds(jf, B), :half] = cr
            sinf_ref[pl.ds(jf, B), half:] = ci
            sinb_ref[pl.ds(jb, B), :half] = cr
            sinb_ref[pl.ds(jb, B), half:] = ci
            lf = sloc_ref[pl.ds(jf, B), :]
            lb = sloc_ref[pl.ds(jb, B), :]
            lr = jnp.where(is_fwd, lf[:, :half], lb[:, :half])
            li = jnp.where(is_fwd, lf[:, half:], lb[:, half:])
            return (ar * cr - ai * ci + lr, ar * ci + ai * cr + li)

        h0 = h0_ref[g8]
        fr, fi = lax.fori_loop(0, nc, body, (h0[:, :half], h0[:, half:]))
        fin_ref[g8, :, :half] = fr
        fin_ref[g8, :, half:] = fi
        s_in = jnp.where(is_fwd_full, sinf_ref[...], sinb_ref[...]).astype(BF16)
        yall_ref[gi] = (jnp.dot(u, m_ref[g8], preferred_element_type=F32)
                        + lax.dot_general(s_in, qt_ref[g8], (((1,), (1,)), ((), ())),
                                          preferred_element_type=F32))
        return 0

    def scatter_step(rb, _, g0):
        c0 = pl.multiple_of(rb * cpb, cpb)
        r0 = pl.multiple_of(rb * S5_RB, S5_RB)
        for hf in range(2):
            ys = [yall_ref[gi, pl.ds(r0, S5_RB), hf * LANES:(hf + 1) * LANES] for gi in range(S5_GS)]
            for k in range(S5_GB):
                acc = ys[0]
                for gi in range(1, S5_GS):
                    acc = jnp.where(lane_grp == (k + g0 + gi) % S5_GB, ys[gi], acc)
                if k:
                    acc = pltpu.roll(acc, (S5_GB - k) * S5_H, 1)
                t = hf * S5_GB + k
                dst = (pl.ds(c0, cpb), slice(t * B, (t + 1) * B), slice(None))
                if g0 > 0:
                    acc = jnp.where(lane_grp >= g0, acc, y_ref[dst].reshape(S5_RB, LANES))
                y_ref[dst] = acc.reshape(cpb, B, LANES)
        return 0

    for g0 in range(0, S5_GB, S5_GS):
        lax.fori_loop(0, rows // S5_RB, functools.partial(gather_step, g0=g0), 0)
        lax.fori_loop(0, S5_GS, functools.partial(group_step, g0=g0), 0)
        lax.fori_loop(0, rows // S5_RB, functools.partial(scatter_step, g0=g0), 0)


def _s5_mix(ua, ops, h0, B):
    m, p, q, a16 = ops
    rows_all = ua.shape[0]
    nc = rows_all // (S5_CHUNK * B)
    rows = nc * B
    up3 = ua.reshape(nc, S5_CHUNK * B, W_A)
    gblk = lambda r: pl.BlockSpec((S5_GB, r, S5_W), lambda j: (j, 0, 0))
    tok = pl.BlockSpec((nc, S5_CHUNK * B, LANES), lambda j: (0, 0, j))
    y, fin = pl.pallas_call(
        functools.partial(_s5_kernel, nc=nc, B=B),
        grid=(W_A // LANES,),
        in_specs=[tok, gblk(S5_W), gblk(S5_W), gblk(S5_W), gblk(1), gblk(B)],
        out_specs=[tok, gblk(B)],
        out_shape=[jax.ShapeDtypeStruct((nc, S5_CHUNK * B, W_A), F32),
                   jax.ShapeDtypeStruct((S5_G, B, S5_W), F32)],
        scratch_shapes=[pltpu.VMEM((S5_GS, rows, S5_W), BF16), pltpu.VMEM((S5_GS, rows, S5_W), F32),
                        pltpu.VMEM((rows, S5_W), F32), pltpu.VMEM((rows, S5_W), F32),
                        pltpu.VMEM((rows, S5_W), F32)],
        compiler_params=_cparams(1),
        name="s5_mix",
    )(up3, m, p, q, a16, h0)
    return y.reshape(rows_all, W_A), fin


def _s5_ops_kernel(lr_ref, li_ref, ldt_ref, br_ref, bi_ref, cr_ref, ci_ref, m_ref, p_ref, qt_ref, a_ref):
    T = S5_CHUNK
    H = S5_H
    lam_re = lr_ref[0]
    lam_im = li_ref[0]
    dt = jnp.exp(ldt_ref[0])
    mag = jnp.exp(lam_re * dt)
    ang = lam_im * dt
    a_re = mag * jnp.cos(ang)
    a_im = mag * jnp.sin(ang)
    n_re = a_re - 1.0
    n_im = a_im
    den = lam_re * lam_re + lam_im * lam_im
    k_re = (n_re * lam_re + n_im * lam_im) / den
    k_im = (n_im * lam_re - n_re * lam_im) / den
    b_re = br_ref[0]
    b_im = bi_ref[0]
    bb_re = k_re * b_re - k_im * b_im
    bb_im = k_re * b_im + k_im * b_re
    c_re = cr_ref[0]
    c_im = ci_ref[0]
    ap = [(jnp.ones_like(a_re), jnp.zeros_like(a_re))]
    for _ in range(T):
        pr, pi = ap[-1]
        ap.append((pr * a_re - pi * a_im, pr * a_im + pi * a_re))
    is_fwd = lax.broadcasted_iota(jnp.int32, (1, LANES), 1) < S5_P

    def powers(kf, kb):
        return jnp.where(is_fwd, ap[kf][0], ap[kb][0]), jnp.where(is_fwd, ap[kf][1], ap[kb][1])

    g8 = pl.program_id(0) % S5_GB

    def pos_rows(t):
        p = (t // S5_GB) * S5_GB + (t % S5_GB + g8) % S5_GB
        return pl.ds(pl.multiple_of(p * H, H), H)

    for t in range(T):
        rows = pos_rows(t)
        er, ei = powers(T - 1 - t, t)
        p_ref[0, rows, :LANES] = (er * bb_re - ei * bb_im).astype(BF16)
        p_ref[0, rows, LANES:] = (er * bb_im + ei * bb_re).astype(BF16)
        er, ei = powers(t + 1, T - t)
        qt_ref[0, rows, :LANES] = (c_re * er - c_im * ei).astype(BF16)
        qt_ref[0, rows, LANES:] = (-(c_re * ei + c_im * er)).astype(BF16)
    a_ref[0, :, :LANES] = ap[T][0]
    a_ref[0, :, LANES:] = ap[T][1]

    def ca(k):
        return c_re * ap[k][0] - c_im * ap[k][1], c_re * ap[k][1] + c_im * ap[k][0]

    fwd = [ca(k) for k in range(T)]
    bwd = fwd[::-1]
    nt = lambda a, b: lax.dot_general(a, b, (((1,), (1,)), ((), ())), precision=HIGHEST,
                                      preferred_element_type=F32)
    cat = lambda parts, i: jnp.concatenate([p[i] for p in parts], axis=0)
    zero = jnp.zeros_like(bb_re)
    wf = (nt(jnp.where(is_fwd, bb_re, zero), cat(fwd, 0)) - nt(jnp.where(is_fwd, bb_im, zero), cat(fwd, 1)))
    wb = (nt(jnp.where(is_fwd, zero, bb_re), cat(bwd, 0)) - nt(jnp.where(is_fwd, zero, bb_im), cat(bwd, 1)))
    pad = jnp.zeros((H, S5_W), F32)
    wf_pad = jnp.concatenate([pad, wf], axis=1)
    wb_pad = jnp.concatenate([wb, pad], axis=1)
    for t in range(T):
        row = (wf_pad[:, S5_W - H * t:2 * S5_W - H * t] + wb_pad[:, (T - 1 - t) * H:(T - 1 - t) * H + S5_W])
        row = jnp.concatenate([pltpu.roll(row[:, :LANES], g8 * H, 1), pltpu.roll(row[:, LANES:], g8 * H, 1)],
                              axis=1)
        m_ref[0, pos_rows(t), :] = row.astype(BF16)


def _s5_operators(lam_re, lam_im, log_dt, b_re, b_im, c_re, c_im):
    G = S5_G
    dirs = lambda x: jnp.concatenate([x[0], x[1]], axis=-1)
    lam = [dirs(x)[:, None, :] for x in (lam_re, lam_im)]
    ldt = dirs(jnp.broadcast_to(log_dt[..., None], (2, G, S5_P)))[:, None, :]
    bt = [dirs(x.transpose(0, 1, 3, 2)) for x in (b_re, b_im)]
    ct = [dirs(x) for x in (c_re, c_im)]
    vec = pl.BlockSpec((1, 1, LANES), lambda g: (g, 0, 0))
    mat = pl.BlockSpec((1, S5_H, LANES), lambda g: (g, 0, 0))
    op = pl.BlockSpec((1, S5_W, S5_W), lambda g: (g, 0, 0))
    return pl.pallas_call(
        _s5_ops_kernel,
        grid=(G,),
        in_specs=[vec, vec, vec, mat, mat, mat, mat],
        out_specs=[op, op, op, pl.BlockSpec((1, 1, S5_W), lambda g: (g, 0, 0))],
        out_shape=[jax.ShapeDtypeStruct((G, S5_W, S5_W), BF16)] * 3
        + [jax.ShapeDtypeStruct((G, 1, S5_W), F32)],
        compiler_params=_cparams(1),
        name="s5_operators",
    )(*lam, ldt, *bt, *ct)


def _phase_tables(u, va, vb, ncb, n):
    def trig(m):
        m = m & (2 * n - 1)
        m = jnp.where(m >= n, m - 2 * n, m)
        ang = m.astype(F32) * (math.pi / n)
        return jnp.cos(ang), jnp.sin(ang)

    lane = lax.broadcasted_iota(jnp.int32, (1, LANES), 1)
    cb, sb = trig(u * (va * lane + vb))
    ca_all, sa_all = trig(u * (va * LANES * lane))
    out = []
    for ch in range(ncb):
        ca = ca_all[:, ch:ch + 1]
        sa = sa_all[:, ch:ch + 1]
        out.append((ca * cb - sa * sb, sa * cb + ca * sb))
    return out


def _tables_kernel(fs_ref, fst_ref, blk_ref, *, Ls):
    u = 2 * lax.broadcasted_iota(jnp.int32, (Ls, 1), 0) + 1
    for ch, (c, s) in enumerate(_phase_tables(u, 1, 0, Ls // LANES, 2 * Ls)):
        blk_ref[:Ls, ch * LANES:(ch + 1) * LANES] = c
        blk_ref[Ls:, ch * LANES:(ch + 1) * LANES] = -s
    blk = blk_ref[...]
    fs_ref[...] = blk.astype(BF16)
    fst_ref[...] = blk.T.astype(BF16)


def _dft_tables(Ls):
    return pl.pallas_call(
        functools.partial(_tables_kernel, Ls=Ls),
        out_shape=[jax.ShapeDtypeStruct((2 * Ls, Ls), BF16), jax.ShapeDtypeStruct((Ls, 2 * Ls), BF16)],
        scratch_shapes=[pltpu.VMEM((2 * Ls, Ls), F32)],
        compiler_params=pltpu.CompilerParams(vmem_limit_bytes=VMEM_LIMIT),
        name="dft_tables",
    )()


def _spectrum_kernel(fs_ref, f_ref, w_ref, *, S, Ls):
    scale = 1.0 / Ls
    k = lax.broadcasted_iota(jnp.int32, (Ls, 1), 0)
    sgn = jnp.where(k % 2 == 0, 1.0, -1.0)
    prev = None
    for i in range(S + 1):
        re = im = None
        if i < S:
            g = jnp.dot(fs_ref[...], f_ref[i * Ls:(i + 1) * Ls, :], preferred_element_type=F32) * scale
            re, im = g[:Ls], g[Ls:]
        cur = (re, im)
        if prev is not None:
            jre, jim = -sgn * prev[1], sgn * prev[0]
            re = jre if re is None else re + jre
            im = jim if im is None else im + jim
        w_ref[0, i, :Ls] = re.astype(w_ref.dtype)
        w_ref[0, i, Ls:] = im.astype(w_ref.dtype)
        prev = cur


def _filter_spectrum(fs, filt, S, C, dtype):
    n2, Ls = fs.shape
    L = filt.shape[0]
    nct = W_B // C
    return pl.pallas_call(
        functools.partial(_spectrum_kernel, S=S, Ls=Ls),
        grid=(2, nct),
        in_specs=[pl.BlockSpec((n2, Ls), lambda o, ct: (0, 0)),
                  pl.BlockSpec((L, C), lambda o, ct: (0, o * nct + ct))],
        out_specs=pl.BlockSpec((1, S + 1, n2, C), lambda o, ct: (o, 0, 0, ct)),
        out_shape=jax.ShapeDtypeStruct((2, S + 1, n2, W_B), dtype),
        compiler_params=_cparams(2),
        name="filter_spectrum",
    )(fs, filt)


def _hyena_filters(L, w1, b1, w2, b2, freq, w3):
    t = jnp.arange(L, dtype=F32)
    tn = t / (L - 1)
    f = jnp.linspace(1e-4, HY_BANDS - 1, HY_BANDS, dtype=F32)
    ang = (2.0 * math.pi / L) * t[:, None] * f[None, :]
    z = jnp.concatenate([tn[:, None], jnp.cos(ang), -jnp.sin(ang)], axis=-1)
    h = jnp.sin(freq * (jnp.dot(z, w1, precision=HIGHEST) + b1))
    h = jnp.sin(freq * (jnp.dot(h, w2, precision=HIGHEST) + b2))
    h = jnp.dot(h, w3, precision=HIGHEST).reshape(L, 2, W_B)
    max_decay = math.log(HY_TARGET) / HY_FAST_PCT
    min_decay = math.log(HY_TARGET) / HY_SLOW_PCT
    deltas = jnp.abs(jnp.linspace(min_decay, max_decay, W_B, dtype=F32))
    half = L // 2
    off = jnp.abs(t - half) / half
    win = jnp.exp(-off[:, None] * deltas[None, :]) + HY_SHIFT
    return h * win[:, None, :]


def _shift_rows(u, s, row):
    L = u.shape[0]
    r = pltpu.roll(u, (-s) % L, 0)
    return jnp.where((row + s >= 0) & (row + s < L), r, 0.0)


HY_RF = 32
HY_R = 128
SUBLANES = 8


def _short_conv_rows(u_ref, w, bias, c, R, L):
    r0 = pl.multiple_of(c * R, R)
    cur = u_ref[pl.ds(r0, R), :]
    before = u_ref[pl.ds(pl.multiple_of(jnp.maximum(r0 - SUBLANES, 0), SUBLANES), SUBLANES), :]
    after = u_ref[pl.ds(pl.multiple_of(jnp.minimum(r0 + R, L - SUBLANES), SUBLANES), SUBLANES), :]
    before = jnp.where(c == 0, 0.0, before[SUBLANES - 1:])
    after = jnp.where(c == L // R - 1, 0.0, after[:1])
    row = lax.broadcasted_iota(jnp.int32, (R, 1), 0)
    up = jnp.where(row == 0, before, pltpu.roll(cur, 1, 0))
    dn = jnp.where(row == R - 1, after, pltpu.roll(cur, R - 1, 0))
    return bias + up * w[0:1] + cur * w[1:2] + dn * w[2:3]


def _hy_kernel(v_ref, x_ref, cwv_ref, cbv_ref, cwx_ref, cbx_ref, hb_ref, fs_ref, fst_ref, w_ref, o_ref,
               zf_ref, zb_ref, u_ref, d_ref, os_ref, *, L, S, R):
    Ls = L // S
    C = zf_ref.shape[1]
    o = pl.program_id(2)
    seg = lambda j: slice(j * C, (j + 1) * C)

    @pl.when(o == 0)
    def _():
        for j in range(S):
            def body(cc, _, j=j):
                c = j * (Ls // R) + cc
                z = _short_conv_rows(v_ref, cwv_ref[...], cbv_ref[...], c, R, L)
                zf_ref[pl.ds(pl.multiple_of(c * R, R), R), :] = z
                zb_ref[pl.ds(pl.multiple_of(cc * R, R), R), seg(j)] = z.astype(BF16)
                return 0
            lax.fori_loop(0, Ls // R, body, 0)

    u_ref[...] = jnp.dot(fs_ref[...], zb_ref[...], preferred_element_type=F32).astype(u_ref.dtype)

    def prod(fc, _):
        f0 = pl.multiple_of(fc * HY_RF, HY_RF)
        re_rows = pl.ds(f0, HY_RF)
        im_rows = pl.ds(Ls + f0, HY_RF)
        for r in range(S):
            dre = dim = None
            for j in range(S):
                d = r - j + S // 2
                if 0 <= d <= S:
                    ur, ui = u_ref[re_rows, seg(j)], u_ref[im_rows, seg(j)]
                    wr, wi = w_ref[0, d, re_rows, :], w_ref[0, d, im_rows, :]
                    pre, pim = ur * wr - ui * wi, ur * wi + ui * wr
                    dre = pre if dre is None else dre + pre
                    dim = pim if dim is None else dim + pim
            d_ref[re_rows, seg(r)] = dre.astype(BF16)
            d_ref[im_rows, seg(r)] = dim.astype(BF16)
        return 0
    lax.fori_loop(0, Ls // HY_RF, prod, 0)

    os_ref[...] = jnp.dot(fst_ref[...], d_ref[...], preferred_element_type=F32)

    for r in range(S):
        def gate(cc, _, r=r):
            c = r * (Ls // R) + cc
            rows = pl.ds(pl.multiple_of(c * R, R), R)
            local = pl.ds(pl.multiple_of(cc * R, R), R)
            x = _short_conv_rows(x_ref, cwx_ref[...], cbx_ref[...], c, R, L)
            z = x * (os_ref[local, seg(r)] + zf_ref[rows, :] * hb_ref[0])

            @pl.when(o == 0)
            def _():
                zf_ref[rows, :] = z
                zb_ref[local, seg(r)] = z.astype(BF16)

            @pl.when(o == 1)
            def _():
                o_ref[rows, :] = z
            return 0
        lax.fori_loop(0, Ls // R, gate, 0)


def _hyena_mix(ut, B, conv_w, conv_b, bias, fs, fst, w, C):
    L = ut.shape[0]
    n2, Ls = fs.shape
    S = L // Ls
    nct = W_B // C
    ncol = UT_COLS // C
    xcol = lambda o, ct: (1 + o) * nct + ct
    conv_b = conv_b.reshape(1, -1)
    R = min(HY_R, Ls)
    return pl.pallas_call(
        functools.partial(_hy_kernel, L=L, S=S, R=R),
        grid=(B, nct, 2),
        in_specs=[pl.BlockSpec((L, C), lambda b, ct, o: (0, b * ncol + ct)),
                  pl.BlockSpec((L, C), lambda b, ct, o: (0, b * ncol + xcol(o, ct))),
                  pl.BlockSpec((3, C), lambda b, ct, o: (0, ct)),
                  pl.BlockSpec((1, C), lambda b, ct, o: (0, ct)),
                  pl.BlockSpec((3, C), lambda b, ct, o: (0, xcol(o, ct))),
                  pl.BlockSpec((1, C), lambda b, ct, o: (0, xcol(o, ct))),
                  pl.BlockSpec((1, 1, C), lambda b, ct, o: (o, 0, ct)),
                  pl.BlockSpec((n2, Ls), lambda b, ct, o: (0, 0)),
                  pl.BlockSpec((Ls, n2), lambda b, ct, o: (0, 0)),
                  pl.BlockSpec((1, S + 1, n2, C), lambda b, ct, o: (o, 0, 0, ct))],
        out_specs=pl.BlockSpec((L, C), lambda b, ct, o: (0, b * nct + ct)),
        out_shape=jax.ShapeDtypeStruct((L, B * W_B), F32),
        scratch_shapes=[pltpu.VMEM((L, C), F32), pltpu.VMEM((Ls, S * C), BF16), pltpu.VMEM((n2, S * C), w.dtype),
                        pltpu.VMEM((n2, S * C), BF16), pltpu.VMEM((Ls, S * C), F32)],
        compiler_params=_cparams(3),
        name="hyena_mix",
    )(ut, ut, conv_w, conv_b, conv_w, conv_b, bias.reshape(2, 1, W_B), fs, fst, w)


def _pool_kernel(u_ref, w_ref, sc_ref, o_ref, *, L):
    row = lax.broadcasted_iota(jnp.int32, (L, 1), 0)
    for gi, win in enumerate(POOL_WINDOWS):
        u = u_ref[:, gi * POOL_G:(gi + 1) * POOL_G]
        s = u
        for off in range(-(win // 2), win - win // 2):
            if off != 0:
                s = s + _shift_rows(u, off, row)
        lo = jnp.maximum(row - win // 2, 0)
        hi = jnp.minimum(row - win // 2 + win, L)
        d = s / (hi - lo).astype(F32) - u
        o_ref[:, gi * POOL_G:(gi + 1) * POOL_G] = (
            jnp.dot(d.astype(BF16), w_ref[gi], preferred_element_type=F32)
            * sc_ref[:, gi * POOL_G:(gi + 1) * POOL_G])


def _pool_mix(ut, B, w, scale):
    L = ut.shape[0]
    nblk = UT_COLS // W_C
    return pl.pallas_call(
        functools.partial(_pool_kernel, L=L),
        grid=(B,),
        in_specs=[pl.BlockSpec((L, W_C), lambda b: (0, b * nblk + nblk - 1)),
                  pl.BlockSpec((len(POOL_WINDOWS), POOL_G, POOL_G), lambda b: (0, 0, 0)),
                  pl.BlockSpec((1, W_C), lambda b: (0, 0))],
        out_specs=pl.BlockSpec((L, W_C), lambda b: (0, b)),
        out_shape=jax.ShapeDtypeStruct((L, B * W_C), F32),
        compiler_params=_cparams(1),
        name="pool_mix",
    )(ut, w, scale.reshape(1, W_C))


def _silu(x):
    return x * jax.nn.sigmoid(x)


def _gelu_tanh(x):
    return x * (0.5 * (1.0 + jnp.tanh(math.sqrt(2.0 / math.pi) * (x + 0.044715 * (x * x * x)))))


def _out_kernel(*refs, has_pos, batch_major, final, B, bm):
    refs = list(refs)
    x_ref = refs.pop(0)
    pos_ref = refs.pop(0) if has_pos else None
    (mod_ref, h_ref, ua_ref, ya_ref, zb_ref, yc_ref, wga_ref, wgb_ref, wgc_ref, wma_ref, wmb_ref, wmc_ref,
     d_ref, wglu_ref, bglu_ref, wa_ref, wb_ref, wc_ref, wo_ref) = refs[:19]
    refs = refs[19:]
    fg_ref = refs.pop(0) if final else None
    xo_ref, zs_ref, ys_ref = refs[:3]
    xs_ref = refs[3] if (batch_major or final) else None
    for b in range(B):
        _put_batch(zs_ref, b, B, zb_ref[:, b * W_B:(b + 1) * W_B])
        _put_batch(ys_ref, b, B, yc_ref[:, b * W_C:(b + 1) * W_C])
    mod = mod_ref[...]
    h = h_ref[...]
    proj = lambda w_ref: jnp.dot(h, w_ref[0], preferred_element_type=F32)
    y = _gelu_tanh(ya_ref[...] + ua_ref[...] * d_ref[0])
    y = y * jax.nn.sigmoid(jnp.dot(y.astype(BF16), wglu_ref[0], preferred_element_type=F32) + bglu_ref[0])
    y_a = (y * _silu(proj(wga_ref))).astype(BF16)
    y_b = (_get_tile(zs_ref) * _silu(proj(wgb_ref))).astype(BF16)
    y_c = (_get_tile(ys_ref) * _silu(proj(wgc_ref))).astype(BF16)
    merged = (jax.nn.sigmoid(proj(wma_ref)) * jnp.dot(y_a, wa_ref[0], preferred_element_type=F32)
              + jax.nn.sigmoid(proj(wmb_ref)) * jnp.dot(y_b, wb_ref[0], preferred_element_type=F32)
              + jax.nn.sigmoid(proj(wmc_ref)) * jnp.dot(y_c, wc_ref[0], preferred_element_type=F32))
    out = jnp.dot(merged.astype(BF16), wo_ref[0], preferred_element_type=F32)
    x = _load_x(x_ref, xs_ref if batch_major else None, pos_ref, B, bm)
    x_new = x + _mod_part(mod, 2, bm) * _by_batch(out, bm)
    if final:
        r = lax.rsqrt(jnp.mean(x_new * x_new, axis=-1, keepdims=True) + EPS)
        _put_tile(xs_ref, (x_new * r * fg_ref[...]).reshape(-1, D_MODEL))
        for b in range(B):
            xo_ref[b] = _get_batch(xs_ref, b, B)
    else:
        xo_ref[...] = x_new.reshape(-1, D_MODEL)


def _out_proj(x, pos, mod, mod_blk, B, bm, h, ua, ya, zb, yc, w_in, l, d, w_glu, b_glu,
              w_a, w_b, w_c, w_o, final_g, tm):
    D = D_MODEL
    rows = x.size // D
    L = rows // B
    batch_major = x.ndim == 3
    final = final_g is not None
    tok = lambda w: pl.BlockSpec((tm, w), lambda i: (i, 0))
    seq = lambda w: pl.BlockSpec((tm // B, B * w), lambda i: (i, 0))
    full = lambda a: _const_spec(a.shape, (0,) * a.ndim)
    in_specs, args = _x_specs(x, pos, mod, mod_blk, B, bm, tm)
    in_specs += [tok(D), tok(W_A), tok(W_A), seq(W_B), seq(W_C)]
    args += [h, ua, ya, zb, yc]
    for cb in G_BLOCKS:
        in_specs.append(_const_spec((1, D, CB), (l, 0, cb)))
        args.append(w_in)
    for k in range(3):
        in_specs.append(_const_spec((1, D, D), (l, 0, M_BLOCK0 + k)))
        args.append(w_in)
    weights = [d, w_glu, b_glu, w_a, w_b, w_c, w_o]
    in_specs += [_const_spec((1,) + a.shape[1:], (l, 0, 0)) for a in weights]
    args += weights
    if final:
        in_specs.append(full(final_g))
        args.append(final_g)
    scratch = [_stage(tm, W_B), _stage(tm, W_C)]
    if batch_major or final:
        scratch.append(_stage(tm, D))
    if final:
        out_spec = pl.BlockSpec((B, tm // B, D), lambda i: (0, i, 0))
        out_shape = jax.ShapeDtypeStruct((B, L, D), F32)
    else:
        out_spec = tok(D)
        out_shape = jax.ShapeDtypeStruct((rows, D), F32)
    return pl.pallas_call(
        functools.partial(_out_kernel, has_pos=pos is not None, batch_major=batch_major, final=final,
                          B=B, bm=bm),
        grid=(rows // tm,),
        in_specs=in_specs,
        out_specs=out_spec,
        out_shape=out_shape,
        scratch_shapes=scratch,
        compiler_params=_cparams(1),
        name="out_proj",
    )(*args)


def _grid_pos_embed(L):
    rows = L // GRID_W
    r = jnp.broadcast_to(jnp.arange(rows, dtype=F32)[:, None], (rows, GRID_W)).reshape(-1)
    col = jnp.broadcast_to(jnp.arange(GRID_W, dtype=F32)[None, :], (rows, GRID_W)).reshape(-1)
    q = D_MODEL // 4
    omega = 1.0 / (10000.0 ** (jnp.arange(q, dtype=F32) / q))
    ar = r[:, None] * omega[None, :]
    ac = col[:, None] * omega[None, :]
    return jnp.concatenate([jnp.sin(ar), jnp.cos(ar), jnp.sin(ac), jnp.cos(ac)], axis=-1)


def _states_to_lanes(st_re, st_im):
    s = jnp.concatenate([st_re[:, 0], st_re[:, 1], st_im[:, 0], st_im[:, 1]], axis=-1)
    return s.transpose(1, 0, 2)


def _lanes_to_states(fin):
    f = fin.transpose(1, 0, 2).reshape(fin.shape[1], S5_G, 4, S5_P)
    return (jnp.stack([f[:, :, 0], f[:, :, 1]], axis=1), jnp.stack([f[:, :, 2], f[:, :, 3]], axis=1))


def kernel(x_prompt, x_sample, c, state_s5_re, state_s5_im, c_ctx, norm_g, w_mod, b_mod, w_in, s5_lam_re, s5_lam_im, s5_log_dt, s5_b_re, s5_b_im, s5_c_re, s5_c_im, s5_d, s5_w_glu, s5_b_glu, hy_conv_w, hy_conv_b, hy_f_w1, hy_f_b1, hy_f_w2, hy_f_b2, hy_f_freq, hy_f_w3, hy_bias, pool_w, pool_scale, w_br_a, w_br_b, w_br_c, w_out, final_g):
    Bc, Lc, D = x_prompt.shape
    Bl, Ll, _ = x_sample.shape
    assert Bl == MOD_ROWS - MOD_LAT0

    cond = jnp.zeros((MOD_ROWS, D), F32).at[0].set(c_ctx).at[MOD_LAT0:].set(c)
    mod = _modulation(cond, w_mod, b_mod).reshape(DEPTH * MOD_ROWS, 3 * D)

    pos = _grid_pos_embed(Ll)
    groups = {
        'ctx': dict(B=Bc, L=Lc, S=2, C=W_B, hy_dtype=F32, bm=1, mod_blk=0),
        'lat': dict(B=Bl, L=Ll, S=4, C=W_B // 2, hy_dtype=BF16, bm=Bl, mod_blk=1),
    }
    tables = {k: _dft_tables(v['L'] // v['S']) for k, v in groups.items()}
    w_in_b = w_in.astype(BF16)
    out_weights = [s5_d.reshape(DEPTH, 1, W_A), s5_w_glu.astype(BF16), s5_b_glu.reshape(DEPTH, 1, W_A),
                   w_br_a.astype(BF16), w_br_b.astype(BF16), w_br_c.astype(BF16), w_out.astype(BF16)]
    xs = {'ctx': x_prompt, 'lat': x_sample}
    new_re, new_im = [], []
    for l in range(DEPTH):
        s5_ops = _s5_operators(s5_lam_re[l], s5_lam_im[l], s5_log_dt[l], s5_b_re[l], s5_b_im[l],
                               s5_c_re[l], s5_c_im[l])
        g = norm_g[l].reshape(1, D)
        for name, cfg in groups.items():
            B, L, bm = cfg['B'], cfg['L'], cfg['bm']
            x = xs[name]
            p = pos if (name == 'lat' and l == 0) else None
            mod_blk = l * (MOD_ROWS // MOD_LAT0) + cfg['mod_blk']
            ua, ut, hn = _in_proj(x, p, mod, mod_blk, B, bm, g, w_in_b, l, min(TM_IN, L * B))
            if name == 'ctx':
                h0g = jnp.zeros((S5_G, B, 4 * S5_P), F32)
            else:
                h0g = _states_to_lanes(state_s5_re[:, l], state_s5_im[:, l])
            ya, fin = _s5_mix(ua, s5_ops, h0g, B)
            if name == 'ctx':
                fr, fi = _lanes_to_states(fin)
                new_re.append(fr)
                new_im.append(fi)
            fs, fst = tables[name]
            filt = _hyena_filters(L, hy_f_w1[l], hy_f_b1[l], hy_f_w2[l], hy_f_b2[l], hy_f_freq[l], hy_f_w3[l])
            hw = _filter_spectrum(fs, filt.reshape(L, 2 * W_B).astype(BF16), cfg['S'], cfg['C'], cfg['hy_dtype'])
            zb = _hyena_mix(ut, B, hy_conv_w[l], hy_conv_b[l], hy_bias[l], fs, fst, hw, cfg['C'])
            yc = _pool_mix(ut, B, pool_w[l].astype(BF16), pool_scale[l])
            fg = final_g.reshape(1, D) if l == DEPTH - 1 else None
            xs[name] = _out_proj(x, p, mod, mod_blk, B, bm, hn, ua, ya, zb, yc, w_in_b, l,
                                 *out_weights, fg, min(TM_OUT, L * B))
    return (xs['ctx'], xs['lat'], jnp.stack(new_re, axis=1), jnp.stack(new_im, axis=1))
```

```python
import functools
import math

import jax
import jax.numpy as jnp
from jax import lax
from jax.experimental import pallas as pl
from jax.experimental.pallas import tpu as pltpu

F32 = jnp.float32
BF16 = jnp.bfloat16
HIGHEST = lax.Precision.HIGHEST

D_MODEL = 1024
DEPTH = 2
GRID_W = 64
EPS = 1e-6
W_A = D_MODEL // 2
S5_H = 16
S5_G = W_A // S5_H
S5_P = 64
W_B = D_MODEL // 2
HY_BANDS = 16
HY_FAST_PCT = 0.3
HY_SLOW_PCT = 1.5
HY_TARGET = 1e-2
HY_SHIFT = 0.05
W_C = D_MODEL // 2
POOL_WINDOWS = (2, 4, 8, 16)
POOL_G = W_C // 4
LANES = 128
CB = 512
U_BLOCKS = (0, 2, 3, 4, 6)
G_BLOCKS = (1, 5, 7)
M_BLOCK0 = 4
UT_COLS = CB * (len(U_BLOCKS) - 1)
S5_CHUNK = 16
S5_W = S5_CHUNK * S5_H
S5_GB = LANES // S5_H
MOD_ROWS = 16
MOD_LAT0 = 8
VMEM_LIMIT = 56 * 1024 * 1024
TM_IN = 1024
TM_OUT = 256


def _const_spec(block_shape, index):
    return pl.BlockSpec(block_shape, lambda *_: index, pipeline_mode=pl.Buffered(1))


def _cparams(n_grid):
    return pltpu.CompilerParams(dimension_semantics=("arbitrary",) * n_grid,
                                vmem_limit_bytes=VMEM_LIMIT)


def _mod_kernel(c_ref, w_ref, b_ref, o_ref):
    c = c_ref[...]
    s = c * jax.nn.sigmoid(c)
    o_ref[0] = jnp.dot(s, w_ref[0], preferred_element_type=F32, precision=HIGHEST) + b_ref[0]


def _modulation(cond, w_mod, b_mod, tn=512):
    n = 3 * D_MODEL
    return pl.pallas_call(
        _mod_kernel,
        grid=(DEPTH, n // tn),
        in_specs=[pl.BlockSpec((MOD_ROWS, D_MODEL), lambda l, j: (0, 0)),
                  pl.BlockSpec((1, D_MODEL, tn), lambda l, j: (l, 0, j)),
                  pl.BlockSpec((1, 1, tn), lambda l, j: (l, 0, j))],
        out_specs=pl.BlockSpec((1, MOD_ROWS, tn), lambda l, j: (l, 0, j)),
        out_shape=jax.ShapeDtypeStruct((DEPTH, MOD_ROWS, n), F32),
        compiler_params=_cparams(2),
        name="modulation",
    )(cond, w_mod, b_mod.reshape(DEPTH, 1, n))


def _by_batch(x, bm):
    return x if bm == 1 else x.reshape(x.shape[0] // bm, bm, x.shape[1])


def _mod_part(mod, i, bm):
    m = mod[:bm, i * D_MODEL:(i + 1) * D_MODEL]
    return m if bm == 1 else m[None]


def _stage(rows, w):
    return pltpu.VMEM((w // LANES, rows, LANES), F32)


def _put_batch(ref, b, B, val):
    for c in range(ref.shape[0]):
        ref[c, pl.ds(b, val.shape[0], stride=B), :] = val[:, c * LANES:(c + 1) * LANES]


def _get_batch(ref, b, B):
    tt = ref.shape[1] // B
    return jnp.concatenate([ref[c, pl.ds(b, tt, stride=B), :] for c in range(ref.shape[0])], axis=1)


def _put_tile(ref, val):
    for c in range(ref.shape[0]):
        ref[c] = val[:, c * LANES:(c + 1) * LANES]


def _get_tile(ref):
    return jnp.concatenate([ref[c] for c in range(ref.shape[0])], axis=1)


def _load_x(x_ref, xs_ref, pos_ref, B, bm):
    if xs_ref is not None:
        for b in range(B):
            _put_batch(xs_ref, b, B, x_ref[b])
        x2 = _get_tile(xs_ref)
    else:
        x2 = x_ref[...]
    x = _by_batch(x2, bm)
    if pos_ref is not None:
        x = x + pos_ref[...][:, None, :]
    return x


def _normed(x, mod, g, bm):
    r = lax.rsqrt(jnp.mean(x * x, axis=-1, keepdims=True) + EPS)
    h = (x * r * g) * (1.0 + _mod_part(mod, 1, bm)) + _mod_part(mod, 0, bm)
    return h.reshape(-1, D_MODEL).astype(BF16)


def _in_kernel(*refs, has_pos, batch_major, B, bm):
    refs = list(refs)
    x_ref = refs.pop(0)
    pos_ref = refs.pop(0) if has_pos else None
    mod_ref, g_ref = refs[:2]
    w_refs = refs[2:2 + len(U_BLOCKS)]
    oa_ref, ot_ref, h_ref, scr_ref = refs[2 + len(U_BLOCKS):6 + len(U_BLOCKS)]
    xs_ref = refs[-1] if batch_major else None
    h = _normed(_load_x(x_ref, xs_ref, pos_ref, B, bm), mod_ref[...], g_ref[...], bm)
    h_ref[...] = h
    oa_ref[...] = jnp.dot(h, w_refs[0][0], preferred_element_type=F32)
    for i, w_ref in enumerate(w_refs[1:]):
        _put_tile(scr_ref, jnp.dot(h, w_ref[0], preferred_element_type=F32))
        for b in range(B):
            ot_ref[:, b * UT_COLS + i * CB:b * UT_COLS + (i + 1) * CB] = _get_batch(scr_ref, b, B)


def _x_specs(x, pos, mod, mod_blk, B, bm, tm):
    D = D_MODEL
    if x.ndim == 3:
        specs = [pl.BlockSpec((B, tm // B, D), lambda i: (0, i, 0))]
    else:
        specs = [pl.BlockSpec((tm, D), lambda i: (i, 0))]
    args = [x]
    if pos is not None:
        specs.append(pl.BlockSpec((tm // bm, D), lambda i: (i, 0)))
        args.append(pos)
    specs.append(_const_spec((MOD_LAT0, 3 * D), (mod_blk, 0)))
    args.append(mod)
    return specs, args


def _in_proj(x, pos, mod, mod_blk, B, bm, g, w_in, l, tm):
    D = D_MODEL
    rows = x.size // D
    L = rows // B
    batch_major = x.ndim == 3
    in_specs, args = _x_specs(x, pos, mod, mod_blk, B, bm, tm)
    in_specs.append(_const_spec((1, D), (0, 0)))
    args.append(g)
    for cb in U_BLOCKS:
        in_specs.append(_const_spec((1, D, CB), (l, 0, cb)))
        args.append(w_in)
    scratch = [_stage(tm, CB)] + ([_stage(tm, D)] if batch_major else [])
    return pl.pallas_call(
        functools.partial(_in_kernel, has_pos=pos is not None, batch_major=batch_major, B=B, bm=bm),
        grid=(rows // tm,),
        in_specs=in_specs,
        out_specs=[pl.BlockSpec((tm, W_A), lambda i: (i, 0)),
                   pl.BlockSpec((tm // B, B * UT_COLS), lambda i: (i, 0)),
                   pl.BlockSpec((tm, D), lambda i: (i, 0))],
        out_shape=[jax.ShapeDtypeStruct((rows, W_A), F32),
                   jax.ShapeDtypeStruct((L, B * UT_COLS), F32),
                   jax.ShapeDtypeStruct((rows, D), BF16)],
        scratch_shapes=scratch,
        compiler_params=_cparams(1),
        name="in_proj",
    )(*args)


S5_RB = 64
S5_GS = 4


def _s5_kernel(u_ref, m_ref, p_ref, qt_ref, a_ref, h0_ref, y_ref, fin_ref,
               x_ref, yall_ref, sloc_ref, sinf_ref, sinb_ref, *, nc, B):
    rows = nc * B
    cpb = S5_RB // B
    half = S5_W // 2
    lane_grp = lax.broadcasted_iota(jnp.int32, (S5_RB, LANES), 1) // S5_H

    lane = lax.broadcasted_iota(jnp.int32, (B, half), 1)
    is_fwd = lane < S5_P
    lane_full = lax.broadcasted_iota(jnp.int32, (rows, S5_W), 1)
    is_fwd_full = (lane_full % half) < S5_P

    def gather_step(rb, _, g0):
        c0 = pl.multiple_of(rb * cpb, cpb)
        r0 = pl.multiple_of(rb * S5_RB, S5_RB)
        slabs = [u_ref[pl.ds(c0, cpb), t * B:(t + 1) * B, :].reshape(S5_RB, LANES) for t in range(S5_CHUNK)]
        rolled = [s if t % S5_GB == 0 else pltpu.roll(s, (t % S5_GB) * S5_H, 1) for t, s in enumerate(slabs)]
        for gi in range(S5_GS):
            g8 = g0 + gi
            for hf in range(2):
                acc = rolled[hf * S5_GB]
                for k in range(1, S5_GB):
                    acc = jnp.where(lane_grp == (k + g8) % S5_GB, rolled[hf * S5_GB + k], acc)
                x_ref[gi, pl.ds(r0, S5_RB), hf * LANES:(hf + 1) * LANES] = acc.astype(BF16)
        return 0

    def group_step(pi, _, g0):
        gis = (2 * pi, 2 * pi + 1)
        ars, ais, init = [], [], []
        for k, gi in enumerate(gis):
            sloc_ref[k] = jnp.dot(x_ref[gi], p_ref[g0 + gi], preferred_element_type=F32)
            a = a_ref[g0 + gi]
            ars.append(a[:, :half])
            ais.append(a[:, half:])
            h0 = h0_ref[g0 + gi]
            init += [h0[:, :half], h0[:, half:]]

        def body(i, carry):
            jf = pl.multiple_of(i * B, B)
            jb = pl.multiple_of((nc - 1 - i) * B, B)
            out = []
            for k in range(2):
                cr, ci = carry[2 * k], carry[2 * k + 1]
                sinf_ref[k, pl.ds(jf, B), :half] = cr
                sinf_ref[k, pl.ds(jf, B), half:] = ci
                sinb_ref[k, pl.ds(jb, B), :half] = cr
                sinb_ref[k, pl.ds(jb, B), half:] = ci
                lf = sloc_ref[k, pl.ds(jf, B), :]
                lb = sloc_ref[k, pl.ds(jb, B), :]
                lr = jnp.where(is_fwd, lf[:, :half], lb[:, :half])
                li = jnp.where(is_fwd, lf[:, half:], lb[:, half:])
                out += [ars[k] * cr - ais[k] * ci + lr, ars[k] * ci + ais[k] * cr + li]
            return tuple(out)

        fin = lax.fori_loop(0, nc, body, tuple(init))
        for k, gi in enumerate(gis):
            g8 = g0 + gi
            fin_ref[g8, :, :half] = fin[2 * k]
            fin_ref[g8, :, half:] = fin[2 * k + 1]
            s_in = jnp.where(is_fwd_full, sinf_ref[k], sinb_ref[k]).astype(BF16)
            yall_ref[gi] = (jnp.dot(x_ref[gi], m_ref[g8], preferred_element_type=F32)
                            + lax.dot_general(s_in, qt_ref[g8], (((1,), (1,)), ((), ())),
                                              preferred_element_type=F32))
        return 0

    def scatter_step(rb, _, g0):
        c0 = pl.multiple_of(rb * cpb, cpb)
        r0 = pl.multiple_of(rb * S5_RB, S5_RB)
        for hf in range(2):
            ys = [yall_ref[gi, pl.ds(r0, S5_RB), hf * LANES:(hf + 1) * LANES] for gi in range(S5_GS)]
            for k in range(S5_GB):
                acc = ys[0]
                for gi in range(1, S5_GS):
                    acc = jnp.where(lane_grp == (k + g0 + gi) % S5_GB, ys[gi], acc)
                if k:
                    acc = pltpu.roll(acc, (S5_GB - k) * S5_H, 1)
                t = hf * S5_GB + k
                dst = (pl.ds(c0, cpb), slice(t * B, (t + 1) * B), slice(None))
                if g0 > 0:
                    acc = jnp.where(lane_grp >= g0, acc, y_ref[dst].reshape(S5_RB, LANES))
                y_ref[dst] = acc.reshape(cpb, B, LANES)
        return 0

    for g0 in range(0, S5_GB, S5_GS):
        lax.fori_loop(0, rows // S5_RB, functools.partial(gather_step, g0=g0), 0)
        lax.fori_loop(0, S5_GS // 2, functools.partial(group_step, g0=g0), 0)
        lax.fori_loop(0, rows // S5_RB, functools.partial(scatter_step, g0=g0), 0)


def _s5_mix(ua, ops, h0, B):
    m, p, q, a16 = ops
    rows_all = ua.shape[0]
    nc = rows_all // (S5_CHUNK * B)
    rows = nc * B
    up3 = ua.reshape(nc, S5_CHUNK * B, W_A)
    gblk = lambda r: pl.BlockSpec((S5_GB, r, S5_W), lambda j: (j, 0, 0))
    tok = pl.BlockSpec((nc, S5_CHUNK * B, LANES), lambda j: (0, 0, j))
    y, fin = pl.pallas_call(
        functools.partial(_s5_kernel, nc=nc, B=B),
        grid=(W_A // LANES,),
        in_specs=[tok, gblk(S5_W), gblk(S5_W), gblk(S5_W), gblk(1), gblk(B)],
        out_specs=[tok, gblk(B)],
        out_shape=[jax.ShapeDtypeStruct((nc, S5_CHUNK * B, W_A), F32),
                   jax.ShapeDtypeStruct((S5_G, B, S5_W), F32)],
        scratch_shapes=[pltpu.VMEM((S5_GS, rows, S5_W), BF16), pltpu.VMEM((S5_GS, rows, S5_W), F32),
                        pltpu.VMEM((2, rows, S5_W), F32), pltpu.VMEM((2, rows, S5_W), F32),
                        pltpu.VMEM((2, rows, S5_W), F32)],
        compiler_params=_cparams(1),
        name="s5_mix",
    )(up3, m, p, q, a16, h0)
    return y.reshape(rows_all, W_A), fin


def _s5_ops_kernel(lr_ref, li_ref, ldt_ref, br_ref, bi_ref, cr_ref, ci_ref, m_ref, p_ref, qt_ref, a_ref):
    T = S5_CHUNK
    H = S5_H
    lam_re = lr_ref[0]
    lam_im = li_ref[0]
    dt = jnp.exp(ldt_ref[0])
    mag = jnp.exp(lam_re * dt)
    ang = lam_im * dt
    a_re = mag * jnp.cos(ang)
    a_im = mag * jnp.sin(ang)
    n_re = a_re - 1.0
    n_im = a_im
    den = lam_re * lam_re + lam_im * lam_im
    k_re = (n_re * lam_re + n_im * lam_im) / den
    k_im = (n_im * lam_re - n_re * lam_im) / den
    b_re = br_ref[0]
    b_im = bi_ref[0]
    bb_re = k_re * b_re - k_im * b_im
    bb_im = k_re * b_im + k_im * b_re
    c_re = cr_ref[0]
    c_im = ci_ref[0]
    ap = [(jnp.ones_like(a_re), jnp.zeros_like(a_re))]
    for _ in range(T):
        pr, pi = ap[-1]
        ap.append((pr * a_re - pi * a_im, pr * a_im + pi * a_re))
    is_fwd = lax.broadcasted_iota(jnp.int32, (1, LANES), 1) < S5_P

    def powers(kf, kb):
        return jnp.where(is_fwd, ap[kf][0], ap[kb][0]), jnp.where(is_fwd, ap[kf][1], ap[kb][1])

    g8 = pl.program_id(0) % S5_GB

    def pos_rows(t):
        p = (t // S5_GB) * S5_GB + (t % S5_GB + g8) % S5_GB
        return pl.ds(pl.multiple_of(p * H, H), H)

    for t in range(T):
        rows = pos_rows(t)
        er, ei = powers(T - 1 - t, t)
        p_ref[0, rows, :LANES] = (er * bb_re - ei * bb_im).astype(BF16)
        p_ref[0, rows, LANES:] = (er * bb_im + ei * bb_re).astype(BF16)
        er, ei = powers(t + 1, T - t)
        qt_ref[0, rows, :LANES] = (c_re * er - c_im * ei).astype(BF16)
        qt_ref[0, rows, LANES:] = (-(c_re * ei + c_im * er)).astype(BF16)
    a_ref[0, :, :LANES] = ap[T][0]
    a_ref[0, :, LANES:] = ap[T][1]

    def ca(k):
        return c_re * ap[k][0] - c_im * ap[k][1], c_re * ap[k][1] + c_im * ap[k][0]

    fwd = [ca(k) for k in range(T)]
    bwd = fwd[::-1]
    nt = lambda a, b: lax.dot_general(a, b, (((1,), (1,)), ((), ())), precision=HIGHEST,
                                      preferred_element_type=F32)
    cat = lambda parts, i: jnp.concatenate([p[i] for p in parts], axis=0)
    zero = jnp.zeros_like(bb_re)
    wf = (nt(jnp.where(is_fwd, bb_re, zero), cat(fwd, 0)) - nt(jnp.where(is_fwd, bb_im, zero), cat(fwd, 1)))
    wb = (nt(jnp.where(is_fwd, zero, bb_re), cat(bwd, 0)) - nt(jnp.where(is_fwd, zero, bb_im), cat(bwd, 1)))
    pad = jnp.zeros((H, S5_W), F32)
    wf_pad = jnp.concatenate([pad, wf], axis=1)
    wb_pad = jnp.concatenate([wb, pad], axis=1)
    for t in range(T):
        row = (wf_pad[:, S5_W - H * t:2 * S5_W - H * t] + wb_pad[:, (T - 1 - t) * H:(T - 1 - t) * H + S5_W])
        row = jnp.concatenate([pltpu.roll(row[:, :LANES], g8 * H, 1), pltpu.roll(row[:, LANES:], g8 * H, 1)],
                              axis=1)
        m_ref[0, pos_rows(t), :] = row.astype(BF16)


def _s5_operators(lam_re, lam_im, log_dt, b_re, b_im, c_re, c_im):
    G = S5_G
    dirs = lambda x: jnp.concatenate([x[0], x[1]], axis=-1)
    lam = [dirs(x)[:, None, :] for x in (lam_re, lam_im)]
    ldt = dirs(jnp.broadcast_to(log_dt[..., None], (2, G, S5_P)))[:, None, :]
    bt = [dirs(x.transpose(0, 1, 3, 2)) for x in (b_re, b_im)]
    ct = [dirs(x) for x in (c_re, c_im)]
    vec = pl.BlockSpec((1, 1, LANES), lambda g: (g, 0, 0))
    mat = pl.BlockSpec((1, S5_H, LANES), lambda g: (g, 0, 0))
    op = pl.BlockSpec((1, S5_W, S5_W), lambda g: (g, 0, 0))
    return pl.pallas_call(
        _s5_ops_kernel,
        grid=(G,),
        in_specs=[vec, vec, vec, mat, mat, mat, mat],
        out_specs=[op, op, op, pl.BlockSpec((1, 1, S5_W), lambda g: (g, 0, 0))],
        out_shape=[jax.ShapeDtypeStruct((G, S5_W, S5_W), BF16)] * 3
        + [jax.ShapeDtypeStruct((G, 1, S5_W), F32)],
        compiler_params=_cparams(1),
        name="s5_operators",
    )(*lam, ldt, *bt, *ct)


def _phase_tables(u, va, vb, ncb, n):
    def trig(m):
        m = m & (2 * n - 1)
        m = jnp.where(m >= n, m - 2 * n, m)
        ang = m.astype(F32) * (math.pi / n)
        return jnp.cos(ang), jnp.sin(ang)

    lane = lax.broadcasted_iota(jnp.int32, (1, LANES), 1)
    cb, sb = trig(u * (va * lane + vb))
    ca_all, sa_all = trig(u * (va * LANES * lane))
    out = []
    for ch in range(ncb):
        ca = ca_all[:, ch:ch + 1]
        sa = sa_all[:, ch:ch + 1]
        out.append((ca * cb - sa * sb, sa * cb + ca * sb))
    return out


def _tables_kernel(fs_ref, fst_ref, blk_ref, *, Ls):
    u = 2 * lax.broadcasted_iota(jnp.int32, (Ls, 1), 0) + 1
    for ch, (c, s) in enumerate(_phase_tables(u, 1, 0, Ls // LANES, 2 * Ls)):
        blk_ref[:Ls, ch * LANES:(ch + 1) * LANES] = c
        blk_ref[Ls:, ch * LANES:(ch + 1) * LANES] = -s
    blk = blk_ref[...]
    fs_ref[...] = blk.astype(BF16)
    fst_ref[...] = blk.T.astype(BF16)


def _dft_tables(Ls):
    return pl.pallas_call(
        functools.partial(_tables_kernel, Ls=Ls),
        out_shape=[jax.ShapeDtypeStruct((2 * Ls, Ls), BF16), jax.ShapeDtypeStruct((Ls, 2 * Ls), BF16)],
        scratch_shapes=[pltpu.VMEM((2 * Ls, Ls), F32)],
        compiler_params=pltpu.CompilerParams(vmem_limit_bytes=VMEM_LIMIT),
        name="dft_tables",
    )()


def _spectrum_kernel(fs_ref, f_ref, w_ref, *, S, Ls):
    scale = 1.0 / Ls
    k = lax.broadcasted_iota(jnp.int32, (Ls, 1), 0)
    sgn = jnp.where(k % 2 == 0, 1.0, -1.0)
    prev = None
    for i in range(S + 1):
        re = im = None
        if i < S:
            g = jnp.dot(fs_ref[...], f_ref[i * Ls:(i + 1) * Ls, :], preferred_element_type=F32) * scale
            re, im = g[:Ls], g[Ls:]
        cur = (re, im)
        if prev is not None:
            jre, jim = -sgn * prev[1], sgn * prev[0]
            re = jre if re is None else re + jre
            im = jim if im is None else im + jim
        w_ref[0, i, :Ls] = re.astype(w_ref.dtype)
        w_ref[0, i, Ls:] = im.astype(w_ref.dtype)
        prev = cur


def _filter_spectrum(fs, filt, S, C, dtype):
    n2, Ls = fs.shape
    L = filt.shape[0]
    nct = W_B // C
    return pl.pallas_call(
        functools.partial(_spectrum_kernel, S=S, Ls=Ls),
        grid=(2, nct),
        in_specs=[pl.BlockSpec((n2, Ls), lambda o, ct: (0, 0)),
                  pl.BlockSpec((L, C), lambda o, ct: (0, o * nct + ct))],
        out_specs=pl.BlockSpec((1, S + 1, n2, C), lambda o, ct: (o, 0, 0, ct)),
        out_shape=jax.ShapeDtypeStruct((2, S + 1, n2, W_B), dtype),
        compiler_params=_cparams(2),
        name="filter_spectrum",
    )(fs, filt)


def _hyena_filters(L, w1, b1, w2, b2, freq, w3):
    t = jnp.arange(L, dtype=F32)
    tn = t / (L - 1)
    f = jnp.linspace(1e-4, HY_BANDS - 1, HY_BANDS, dtype=F32)
    ang = (2.0 * math.pi / L) * t[:, None] * f[None, :]
    z = jnp.concatenate([tn[:, None], jnp.cos(ang), -jnp.sin(ang)], axis=-1)
    h = jnp.sin(freq * (jnp.dot(z, w1, precision=HIGHEST) + b1))
    h = jnp.sin(freq * (jnp.dot(h, w2, precision=HIGHEST) + b2))
    h = jnp.dot(h, w3, precision=HIGHEST).reshape(L, 2, W_B)
    max_decay = math.log(HY_TARGET) / HY_FAST_PCT
    min_decay = math.log(HY_TARGET) / HY_SLOW_PCT
    deltas = jnp.abs(jnp.linspace(min_decay, max_decay, W_B, dtype=F32))
    half = L // 2
    off = jnp.abs(t - half) / half
    win = jnp.exp(-off[:, None] * deltas[None, :]) + HY_SHIFT
    return h * win[:, None, :]


HY_RF = 32
HY_R = 128
SUBLANES = 8


def _short_conv_rows(u_ref, w, bias, c, R, L):
    r0 = pl.multiple_of(c * R, R)
    cur = u_ref[pl.ds(r0, R), :]
    before = u_ref[pl.ds(pl.multiple_of(jnp.maximum(r0 - SUBLANES, 0), SUBLANES), SUBLANES), :]
    after = u_ref[pl.ds(pl.multiple_of(jnp.minimum(r0 + R, L - SUBLANES), SUBLANES), SUBLANES), :]
    before = jnp.where(c == 0, 0.0, before[SUBLANES - 1:])
    after = jnp.where(c == L // R - 1, 0.0, after[:1])
    row = lax.broadcasted_iota(jnp.int32, (R, 1), 0)
    up = jnp.where(row == 0, before, pltpu.roll(cur, 1, 0))
    dn = jnp.where(row == R - 1, after, pltpu.roll(cur, R - 1, 0))
    return bias + up * w[0:1] + cur * w[1:2] + dn * w[2:3]


def _hy_kernel(v_ref, x_ref, cwv_ref, cbv_ref, cwx_ref, cbx_ref, hb_ref, fs_ref, fst_ref, w_ref, o_ref,
               zf_ref, zb_ref, u_ref, d_ref, os_ref, *, L, S, R):
    Ls = L // S
    C = zf_ref.shape[1]
    o = pl.program_id(2)
    seg = lambda j: slice(j * C, (j + 1) * C)

    @pl.when(o == 0)
    def _():
        for j in range(S):
            def body(cc, _, j=j):
                c = j * (Ls // R) + cc
                z = _short_conv_rows(v_ref, cwv_ref[...], cbv_ref[...], c, R, L)
                zf_ref[pl.ds(pl.multiple_of(c * R, R), R), :] = z
                zb_ref[pl.ds(pl.multiple_of(cc * R, R), R), seg(j)] = z.astype(BF16)
                return 0
            lax.fori_loop(0, Ls // R, body, 0)

    u_ref[...] = jnp.dot(fs_ref[...], zb_ref[...], preferred_element_type=F32).astype(u_ref.dtype)

    def prod(fc, _):
        f0 = pl.multiple_of(fc * HY_RF, HY_RF)
        re_rows = pl.ds(f0, HY_RF)
        im_rows = pl.ds(Ls + f0, HY_RF)
        for r in range(S):
            dre = dim = None
            for j in range(S):
                d = r - j + S // 2
                if 0 <= d <= S:
                    ur, ui = u_ref[re_rows, seg(j)], u_ref[im_rows, seg(j)]
                    wr, wi = w_ref[0, d, re_rows, :], w_ref[0, d, im_rows, :]
                    pre, pim = ur * wr - ui * wi, ur * wi + ui * wr
                    dre = pre if dre is None else dre + pre
                    dim = pim if dim is None else dim + pim
            d_ref[re_rows, seg(r)] = dre.astype(BF16)
            d_ref[im_rows, seg(r)] = dim.astype(BF16)
        return 0
    lax.fori_loop(0, Ls // HY_RF, prod, 0)

    os_ref[...] = jnp.dot(fst_ref[...], d_ref[...], preferred_element_type=F32)

    for r in range(S):
        def gate(cc, _, r=r):
            c = r * (Ls // R) + cc
            rows = pl.ds(pl.multiple_of(c * R, R), R)
            local = pl.ds(pl.multiple_of(cc * R, R), R)
            x = _short_conv_rows(x_ref, cwx_ref[...], cbx_ref[...], c, R, L)
            z = x * (os_ref[local, seg(r)] + zf_ref[rows, :] * hb_ref[0])

            @pl.when(o == 0)
            def _():
                zf_ref[rows, :] = z
                zb_ref[local, seg(r)] = z.astype(BF16)

            @pl.when(o == 1)
            def _():
                o_ref[rows, :] = z
            return 0
        lax.fori_loop(0, Ls // R, gate, 0)


def _hyena_mix(ut, B, conv_w, conv_b, bias, fs, fst, w, C):
    L = ut.shape[0]
    n2, Ls = fs.shape
    S = L // Ls
    nct = W_B // C
    ncol = UT_COLS // C
    xcol = lambda o, ct: (1 + o) * nct + ct
    conv_b = conv_b.reshape(1, -1)
    R = min(HY_R, Ls)
    return pl.pallas_call(
        functools.partial(_hy_kernel, L=L, S=S, R=R),
        grid=(B, nct, 2),
        in_specs=[pl.BlockSpec((L, C), lambda b, ct, o: (0, b * ncol + ct)),
                  pl.BlockSpec((L, C), lambda b, ct, o: (0, b * ncol + xcol(o, ct))),
                  pl.BlockSpec((3, C), lambda b, ct, o: (0, ct)),
                  pl.BlockSpec((1, C), lambda b, ct, o: (0, ct)),
                  pl.BlockSpec((3, C), lambda b, ct, o: (0, xcol(o, ct))),
                  pl.BlockSpec((1, C), lambda b, ct, o: (0, xcol(o, ct))),
                  pl.BlockSpec((1, 1, C), lambda b, ct, o: (o, 0, ct)),
                  pl.BlockSpec((n2, Ls), lambda b, ct, o: (0, 0)),
                  pl.BlockSpec((Ls, n2), lambda b, ct, o: (0, 0)),
                  pl.BlockSpec((1, S + 1, n2, C), lambda b, ct, o: (o, 0, 0, ct))],
        out_specs=pl.BlockSpec((L, C), lambda b, ct, o: (0, b * nct + ct)),
        out_shape=jax.ShapeDtypeStruct((L, B * W_B), F32),
        scratch_shapes=[pltpu.VMEM((L, C), F32), pltpu.VMEM((Ls, S * C), BF16), pltpu.VMEM((n2, S * C), w.dtype),
                        pltpu.VMEM((n2, S * C), BF16), pltpu.VMEM((Ls, S * C), F32)],
        compiler_params=_cparams(3),
        name="hyena_mix",
    )(ut, ut, conv_w, conv_b, conv_w, conv_b, bias.reshape(2, 1, W_B), fs, fst, w)


def _shift_rows(u, s, row):
    L = u.shape[0]
    r = pltpu.roll(u, (-s) % L, 0)
    return jnp.where((row + s >= 0) & (row + s < L), r, 0.0)


def _pool_kernel(u_ref, w_ref, sc_ref, o_ref, *, L):
    row = lax.broadcasted_iota(jnp.int32, (L, 1), 0)
    for gi, win in enumerate(POOL_WINDOWS):
        u = u_ref[:, gi * POOL_G:(gi + 1) * POOL_G]
        s = u
        for off in range(-(win // 2), win - win // 2):
            if off != 0:
                s = s + _shift_rows(u, off, row)
        lo = jnp.maximum(row - win // 2, 0)
        hi = jnp.minimum(row - win // 2 + win, L)
        d = s / (hi - lo).astype(F32) - u
        o_ref[:, gi * POOL_G:(gi + 1) * POOL_G] = (
            jnp.dot(d.astype(BF16), w_ref[gi], preferred_element_type=F32)
            * sc_ref[:, gi * POOL_G:(gi + 1) * POOL_G])


def _pool_mix(ut, B, w, scale):
    L = ut.shape[0]
    nblk = UT_COLS // W_C
    return pl.pallas_call(
        functools.partial(_pool_kernel, L=L),
        grid=(B,),
        in_specs=[pl.BlockSpec((L, W_C), lambda b: (0, b * nblk + nblk - 1)),
                  pl.BlockSpec((len(POOL_WINDOWS), POOL_G, POOL_G), lambda b: (0, 0, 0)),
                  pl.BlockSpec((1, W_C), lambda b: (0, 0))],
        out_specs=pl.BlockSpec((L, W_C), lambda b: (0, b)),
        out_shape=jax.ShapeDtypeStruct((L, B * W_C), F32),
        compiler_params=_cparams(1),
        name="pool_mix",
    )(ut, w, scale.reshape(1, W_C))


def _silu(x):
    return x * jax.nn.sigmoid(x)


def _gelu_tanh(x):
    return x * (0.5 * (1.0 + jnp.tanh(math.sqrt(2.0 / math.pi) * (x + 0.044715 * (x * x * x)))))


def _out_kernel(*refs, has_pos, batch_major, final, B, bm):
    refs = list(refs)
    x_ref = refs.pop(0)
    pos_ref = refs.pop(0) if has_pos else None
    (mod_ref, h_ref, ua_ref, ya_ref, zb_ref, yc_ref, wga_ref, wgb_ref, wgc_ref, wma_ref, wmb_ref, wmc_ref,
     d_ref, wglu_ref, bglu_ref, wa_ref, wb_ref, wc_ref, wo_ref) = refs[:19]
    refs = refs[19:]
    fg_ref = refs.pop(0) if final else None
    xo_ref, zs_ref, ys_ref = refs[:3]
    xs_ref = refs[3] if (batch_major or final) else None
    for b in range(B):
        _put_batch(zs_ref, b, B, zb_ref[:, b * W_B:(b + 1) * W_B])
        _put_batch(ys_ref, b, B, yc_ref[:, b * W_C:(b + 1) * W_C])
    mod = mod_ref[...]
    h = h_ref[...]
    proj = lambda w_ref: jnp.dot(h, w_ref[0], preferred_element_type=F32)
    y = _gelu_tanh(ya_ref[...] + ua_ref[...] * d_ref[0])
    y = y * jax.nn.sigmoid(jnp.dot(y.astype(BF16), wglu_ref[0], preferred_element_type=F32) + bglu_ref[0])
    y_a = (y * _silu(proj(wga_ref))).astype(BF16)
    y_b = (_get_tile(zs_ref) * _silu(proj(wgb_ref))).astype(BF16)
    y_c = (_get_tile(ys_ref) * _silu(proj(wgc_ref))).astype(BF16)
    merged = (jax.nn.sigmoid(proj(wma_ref)) * jnp.dot(y_a, wa_ref[0], preferred_element_type=F32)
              + jax.nn.sigmoid(proj(wmb_ref)) * jnp.dot(y_b, wb_ref[0], preferred_element_type=F32)
              + jax.nn.sigmoid(proj(wmc_ref)) * jnp.dot(y_c, wc_ref[0], preferred_element_type=F32))
    out = jnp.dot(merged.astype(BF16), wo_ref[0], preferred_element_type=F32)
    x = _load_x(x_ref, xs_ref if batch_major else None, pos_ref, B, bm)
    x_new = x + _mod_part(mod, 2, bm) * _by_batch(out, bm)
    if final:
        r = lax.rsqrt(jnp.mean(x_new * x_new, axis=-1, keepdims=True) + EPS)
        _put_tile(xs_ref, (x_new * r * fg_ref[...]).reshape(-1, D_MODEL))
        for b in range(B):
            xo_ref[b] = _get_batch(xs_ref, b, B)
    else:
        xo_ref[...] = x_new.reshape(-1, D_MODEL)


def _out_proj(x, pos, mod, mod_blk, B, bm, h, ua, ya, zb, yc, w_in, l, d, w_glu, b_glu,
              w_a, w_b, w_c, w_o, final_g, tm):
    D = D_MODEL
    rows = x.size // D
    L = rows // B
    batch_major = x.ndim == 3
    final = final_g is not None
    tok = lambda w: pl.BlockSpec((tm, w), lambda i: (i, 0))
    seq = lambda w: pl.BlockSpec((tm // B, B * w), lambda i: (i, 0))
    full = lambda a: _const_spec(a.shape, (0,) * a.ndim)
    in_specs, args = _x_specs(x, pos, mod, mod_blk, B, bm, tm)
    in_specs += [tok(D), tok(W_A), tok(W_A), seq(W_B), seq(W_C)]
    args += [h, ua, ya, zb, yc]
    for cb in G_BLOCKS:
        in_specs.append(_const_spec((1, D, CB), (l, 0, cb)))
        args.append(w_in)
    for k in range(3):
        in_specs.append(_const_spec((1, D, D), (l, 0, M_BLOCK0 + k)))
        args.append(w_in)
    weights = [d, w_glu, b_glu, w_a, w_b, w_c, w_o]
    in_specs += [_const_spec((1,) + a.shape[1:], (l, 0, 0)) for a in weights]
    args += weights
    if final:
        in_specs.append(full(final_g))
        args.append(final_g)
    scratch = [_stage(tm, W_B), _stage(tm, W_C)]
    if batch_major or final:
        scratch.append(_stage(tm, D))
    if final:
        out_spec = pl.BlockSpec((B, tm // B, D), lambda i: (0, i, 0))
        out_shape = jax.ShapeDtypeStruct((B, L, D), F32)
    else:
        out_spec = tok(D)
        out_shape = jax.ShapeDtypeStruct((rows, D), F32)
    return pl.pallas_call(
        functools.partial(_out_kernel, has_pos=pos is not None, batch_major=batch_major, final=final,
                          B=B, bm=bm),
        grid=(rows // tm,),
        in_specs=in_specs,
        out_specs=out_spec,
        out_shape=out_shape,
        scratch_shapes=scratch,
        compiler_params=_cparams(1),
        name="out_proj",
    )(*args)


def _grid_pos_embed(L):
    rows = L // GRID_W
    r = jnp.broadcast_to(jnp.arange(rows, dtype=F32)[:, None], (rows, GRID_W)).reshape(-1)
    col = jnp.broadcast_to(jnp.arange(GRID_W, dtype=F32)[None, :], (rows, GRID_W)).reshape(-1)
    q = D_MODEL // 4
    omega = 1.0 / (10000.0 ** (jnp.arange(q, dtype=F32) / q))
    ar = r[:, None] * omega[None, :]
    ac = col[:, None] * omega[None, :]
    return jnp.concatenate([jnp.sin(ar), jnp.cos(ar), jnp.sin(ac), jnp.cos(ac)], axis=-1)


def _states_to_lanes(st_re, st_im):
    s = jnp.concatenate([st_re[:, 0], st_re[:, 1], st_im[:, 0], st_im[:, 1]], axis=-1)
    return s.transpose(1, 0, 2)


def _lanes_to_states(fin):
    f = fin.transpose(1, 0, 2).reshape(fin.shape[1], S5_G, 4, S5_P)
    return (jnp.stack([f[:, :, 0], f[:, :, 1]], axis=1), jnp.stack([f[:, :, 2], f[:, :, 3]], axis=1))


def kernel(x_prompt, x_sample, c, state_s5_re, state_s5_im, c_ctx, norm_g, w_mod, b_mod, w_in, s5_lam_re, s5_lam_im, s5_log_dt, s5_b_re, s5_b_im, s5_c_re, s5_c_im, s5_d, s5_w_glu, s5_b_glu, hy_conv_w, hy_conv_b, hy_f_w1, hy_f_b1, hy_f_w2, hy_f_b2, hy_f_freq, hy_f_w3, hy_bias, pool_w, pool_scale, w_br_a, w_br_b, w_br_c, w_out, final_g):
    Bc, Lc, D = x_prompt.shape
    Bl, Ll, _ = x_sample.shape
    assert Bl == MOD_ROWS - MOD_LAT0

    cond = jnp.zeros((MOD_ROWS, D), F32).at[0].set(c_ctx).at[MOD_LAT0:].set(c)
    mod = _modulation(cond, w_mod, b_mod).reshape(DEPTH * MOD_ROWS, 3 * D)

    pos = _grid_pos_embed(Ll)
    groups = {
        'ctx': dict(B=Bc, L=Lc, S=2, C=W_B, hy_dtype=F32, bm=1, mod_blk=0),
        'lat': dict(B=Bl, L=Ll, S=4, C=W_B // 2, hy_dtype=BF16, bm=Bl, mod_blk=1),
    }
    tables = {k: _dft_tables(v['L'] // v['S']) for k, v in groups.items()}
    w_in_b = w_in.astype(BF16)
    out_weights = [s5_d.reshape(DEPTH, 1, W_A), s5_w_glu.astype(BF16), s5_b_glu.reshape(DEPTH, 1, W_A),
                   w_br_a.astype(BF16), w_br_b.astype(BF16), w_br_c.astype(BF16), w_out.astype(BF16)]
    xs = {'ctx': x_prompt, 'lat': x_sample}
    new_re, new_im = [], []
    for l in range(DEPTH):
        s5_ops = _s5_operators(s5_lam_re[l], s5_lam_im[l], s5_log_dt[l], s5_b_re[l], s5_b_im[l],
                               s5_c_re[l], s5_c_im[l])
        g = norm_g[l].reshape(1, D)
        for name, cfg in groups.items():
            B, L, bm = cfg['B'], cfg['L'], cfg['bm']
            x = xs[name]
            p = pos if (name == 'lat' and l == 0) else None
            mod_blk = l * (MOD_ROWS // MOD_LAT0) + cfg['mod_blk']
            ua, ut, hn = _in_proj(x, p, mod, mod_blk, B, bm, g, w_in_b, l, min(TM_IN, L * B))
            if name == 'ctx':
                h0g = jnp.zeros((S5_G, B, 4 * S5_P), F32)
            else:
                h0g = _states_to_lanes(state_s5_re[:, l], state_s5_im[:, l])
            ya, fin = _s5_mix(ua, s5_ops, h0g, B)
            if name == 'ctx':
                fr, fi = _lanes_to_states(fin)
                new_re.append(fr)
                new_im.append(fi)
            fs, fst = tables[name]
            filt = _hyena_filters(L, hy_f_w1[l], hy_f_b1[l], hy_f_w2[l], hy_f_b2[l], hy_f_freq[l], hy_f_w3[l])
            hw = _filter_spectrum(fs, filt.reshape(L, 2 * W_B).astype(BF16), cfg['S'], cfg['C'], cfg['hy_dtype'])
            zb = _hyena_mix(ut, B, hy_conv_w[l], hy_conv_b[l], hy_bias[l], fs, fst, hw, cfg['C'])
            yc = _pool_mix(ut, B, pool_w[l].astype(BF16), pool_scale[l])
            fg = final_g.reshape(1, D) if l == DEPTH - 1 else None
            xs[name] = _out_proj(x, p, mod, mod_blk, B, bm, hn, ua, ya, zb, yc, w_in_b, l,
                                 *out_weights, fg, min(TM_OUT, L * B))
    return (xs['ctx'], xs['lat'], jnp.stack(new_re, axis=1), jnp.stack(new_im, axis=1))
```

```python
import functools
import math

import jax
import jax.numpy as jnp
from jax import lax
from jax.experimental import pallas as pl
from jax.experimental.pallas import tpu as pltpu

F32 = jnp.float32
BF16 = jnp.bfloat16
HIGHEST = lax.Precision.HIGHEST

D_MODEL = 1024
DEPTH = 2
GRID_W = 64
EPS = 1e-6
W_A = D_MODEL // 2
S5_H = 16
S5_G = W_A // S5_H
S5_P = 64
W_B = D_MODEL // 2
HY_BANDS = 16
HY_FAST_PCT = 0.3
HY_SLOW_PCT = 1.5
HY_TARGET = 1e-2
HY_SHIFT = 0.05
W_C = D_MODEL // 2
POOL_WINDOWS = (2, 4, 8, 16)
POOL_G = W_C // 4
LANES = 128
CB = 512
U_BLOCKS = (0, 2, 3, 4, 6)
G_BLOCKS = (1, 5, 7)
M_BLOCK0 = 4
UT_COLS = CB * (len(U_BLOCKS) - 1)
S5_CHUNK = 16
S5_W = S5_CHUNK * S5_H
S5_GB = LANES // S5_H
MOD_ROWS = 16
MOD_LAT0 = 8
VMEM_LIMIT = 56 * 1024 * 1024
TM_IN = 1024
TM_OUT = 256


def _const_spec(block_shape, index):
    return pl.BlockSpec(block_shape, lambda *_: index, pipeline_mode=pl.Buffered(1))


def _cparams(n_grid):
    return pltpu.CompilerParams(dimension_semantics=("arbitrary",) * n_grid,
                                vmem_limit_bytes=VMEM_LIMIT)


def _mod_kernel(c_ref, w_ref, b_ref, o_ref):
    c = c_ref[...]
    s = c * jax.nn.sigmoid(c)
    o_ref[0] = jnp.dot(s, w_ref[0], preferred_element_type=F32, precision=HIGHEST) + b_ref[0]


def _modulation(cond, w_mod, b_mod, tn=512):
    n = 3 * D_MODEL
    return pl.pallas_call(
        _mod_kernel,
        grid=(DEPTH, n // tn),
        in_specs=[pl.BlockSpec((MOD_ROWS, D_MODEL), lambda l, j: (0, 0)),
                  pl.BlockSpec((1, D_MODEL, tn), lambda l, j: (l, 0, j)),
                  pl.BlockSpec((1, 1, tn), lambda l, j: (l, 0, j))],
        out_specs=pl.BlockSpec((1, MOD_ROWS, tn), lambda l, j: (l, 0, j)),
        out_shape=jax.ShapeDtypeStruct((DEPTH, MOD_ROWS, n), F32),
        compiler_params=_cparams(2),
        name="modulation",
    )(cond, w_mod, b_mod.reshape(DEPTH, 1, n))


def _by_batch(x, bm):
    return x if bm == 1 else x.reshape(x.shape[0] // bm, bm, x.shape[1])


def _mod_part(mod, i, bm):
    m = mod[:bm, i * D_MODEL:(i + 1) * D_MODEL]
    return m if bm == 1 else m[None]


def _stage(rows, w):
    return pltpu.VMEM((w // LANES, rows, LANES), F32)


def _put_batch(ref, b, B, val):
    for c in range(ref.shape[0]):
        ref[c, pl.ds(b, val.shape[0], stride=B), :] = val[:, c * LANES:(c + 1) * LANES]


def _get_batch(ref, b, B):
    tt = ref.shape[1] // B
    return jnp.concatenate([ref[c, pl.ds(b, tt, stride=B), :] for c in range(ref.shape[0])], axis=1)


def _put_tile(ref, val):
    for c in range(ref.shape[0]):
        ref[c] = val[:, c * LANES:(c + 1) * LANES]


def _get_tile(ref):
    return jnp.concatenate([ref[c] for c in range(ref.shape[0])], axis=1)


def _load_x(x_ref, xs_ref, pos_ref, B, bm):
    if xs_ref is not None:
        for b in range(B):
            _put_batch(xs_ref, b, B, x_ref[b])
        x2 = _get_tile(xs_ref)
    else:
        x2 = x_ref[...]
    x = _by_batch(x2, bm)
    if pos_ref is not None:
        x = x + pos_ref[...][:, None, :]
    return x


def _normed(x, mod, g, bm):
    r = lax.rsqrt(jnp.mean(x * x, axis=-1, keepdims=True) + EPS)
    h = (x * r * g) * (1.0 + _mod_part(mod, 1, bm)) + _mod_part(mod, 0, bm)
    return h.reshape(-1, D_MODEL).astype(BF16)


def _in_kernel(*refs, has_pos, batch_major, B, bm):
    refs = list(refs)
    x_ref = refs.pop(0)
    pos_ref = refs.pop(0) if has_pos else None
    mod_ref, g_ref = refs[:2]
    w_refs = refs[2:2 + len(U_BLOCKS)]
    oa_ref, ot_ref, h_ref, scr_ref = refs[2 + len(U_BLOCKS):6 + len(U_BLOCKS)]
    xs_ref = refs[-1] if batch_major else None
    h = _normed(_load_x(x_ref, xs_ref, pos_ref, B, bm), mod_ref[...], g_ref[...], bm)
    h_ref[...] = h
    oa_ref[...] = jnp.dot(h, w_refs[0][0], preferred_element_type=F32)
    for i, w_ref in enumerate(w_refs[1:]):
        _put_tile(scr_ref, jnp.dot(h, w_ref[0], preferred_element_type=F32))
        for b in range(B):
            ot_ref[:, b * UT_COLS + i * CB:b * UT_COLS + (i + 1) * CB] = _get_batch(scr_ref, b, B)


def _x_specs(x, pos, mod, mod_blk, B, bm, tm):
    D = D_MODEL
    if x.ndim == 3:
        specs = [pl.BlockSpec((B, tm // B, D), lambda i: (0, i, 0))]
    else:
        specs = [pl.BlockSpec((tm, D), lambda i: (i, 0))]
    args = [x]
    if pos is not None:
        specs.append(pl.BlockSpec((tm // bm, D), lambda i: (i, 0)))
        args.append(pos)
    specs.append(_const_spec((MOD_LAT0, 3 * D), (mod_blk, 0)))
    args.append(mod)
    return specs, args


def _in_proj(x, pos, mod, mod_blk, B, bm, g, w_in, l, tm):
    D = D_MODEL
    rows = x.size // D
    L = rows // B
    batch_major = x.ndim == 3
    in_specs, args = _x_specs(x, pos, mod, mod_blk, B, bm, tm)
    in_specs.append(_const_spec((1, D), (0, 0)))
    args.append(g)
    for cb in U_BLOCKS:
        in_specs.append(_const_spec((1, D, CB), (l, 0, cb)))
        args.append(w_in)
    scratch = [_stage(tm, CB)] + ([_stage(tm, D)] if batch_major else [])
    return pl.pallas_call(
        functools.partial(_in_kernel, has_pos=pos is not None, batch_major=batch_major, B=B, bm=bm),
        grid=(rows // tm,),
        in_specs=in_specs,
        out_specs=[pl.BlockSpec((tm, W_A), lambda i: (i, 0)),
                   pl.BlockSpec((tm // B, B * UT_COLS), lambda i: (i, 0)),
                   pl.BlockSpec((tm, D), lambda i: (i, 0))],
        out_shape=[jax.ShapeDtypeStruct((rows, W_A), F32),
                   jax.ShapeDtypeStruct((L, B * UT_COLS), F32),
                   jax.ShapeDtypeStruct((rows, D), BF16)],
        scratch_shapes=scratch,
        compiler_params=_cparams(1),
        name="in_proj",
    )(*args)


S5_RB = 64
S5_GS = 4


def _s5_kernel(u_ref, m_ref, p_ref, qt_ref, a_ref, h0_ref, y_ref, fin_ref,
               x_ref, yall_ref, sloc_ref, sinf_ref, sinb_ref, *, nc, B):
    rows = nc * B
    cpb = S5_RB // B
    half = S5_W // 2
    lane_grp = lax.broadcasted_iota(jnp.int32, (S5_RB, LANES), 1) // S5_H

    lane = lax.broadcasted_iota(jnp.int32, (B, half), 1)
    is_fwd = lane < S5_P
    lane_full = lax.broadcasted_iota(jnp.int32, (rows, S5_W), 1)
    is_fwd_full = (lane_full % half) < S5_P

    def gather_step(rb, _, g0):
        c0 = pl.multiple_of(rb * cpb, cpb)
        r0 = pl.multiple_of(rb * S5_RB, S5_RB)
        slabs = [u_ref[pl.ds(c0, cpb), t * B:(t + 1) * B, :].reshape(S5_RB, LANES) for t in range(S5_CHUNK)]
        rolled = [s if t % S5_GB == 0 else pltpu.roll(s, (t % S5_GB) * S5_H, 1) for t, s in enumerate(slabs)]
        for gi in range(S5_GS):
            g8 = g0 + gi
            for hf in range(2):
                acc = rolled[hf * S5_GB]
                for k in range(1, S5_GB):
                    acc = jnp.where(lane_grp == (k + g8) % S5_GB, rolled[hf * S5_GB + k], acc)
                x_ref[gi, pl.ds(r0, S5_RB), hf * LANES:(hf + 1) * LANES] = acc.astype(BF16)
        return 0

    def group_step(pi, _, g0):
        gis = (2 * pi, 2 * pi + 1)
        ars, ais, init = [], [], []
        for k, gi in enumerate(gis):
            sloc_ref[k] = jnp.dot(x_ref[gi], p_ref[g0 + gi], preferred_element_type=F32)
            a = a_ref[g0 + gi]
            ars.append(a[:, :half])
            ais.append(a[:, half:])
            h0 = h0_ref[g0 + gi]
            init += [h0[:, :half], h0[:, half:]]

        def body(i, carry):
            jf = pl.multiple_of(i * B, B)
            jb = pl.multiple_of((nc - 1 - i) * B, B)
            out = []
            for k in range(2):
                cr, ci = carry[2 * k], carry[2 * k + 1]
                sinf_ref[k, pl.ds(jf, B), :half] = cr
                sinf_ref[k, pl.ds(jf, B), half:] = ci
                sinb_ref[k, pl.ds(jb, B), :half] = cr
                sinb_ref[k, pl.ds(jb, B), half:] = ci
                lf = sloc_ref[k, pl.ds(jf, B), :]
                lb = sloc_ref[k, pl.ds(jb, B), :]
                lr = jnp.where(is_fwd, lf[:, :half], lb[:, :half])
                li = jnp.where(is_fwd, lf[:, half:], lb[:, half:])
                out += [ars[k] * cr - ais[k] * ci + lr, ars[k] * ci + ais[k] * cr + li]
            return tuple(out)

        fin = lax.fori_loop(0, nc, body, tuple(init))
        for k, gi in enumerate(gis):
            g8 = g0 + gi
            fin_ref[g8, :, :half] = fin[2 * k]
            fin_ref[g8, :, half:] = fin[2 * k + 1]
            s_in = jnp.where(is_fwd_full, sinf_ref[k], sinb_ref[k]).astype(BF16)
            yall_ref[gi] = (jnp.dot(x_ref[gi], m_ref[g8], preferred_element_type=F32)
                            + lax.dot_general(s_in, qt_ref[g8], (((1,), (1,)), ((), ())),
                                              preferred_element_type=F32))
        return 0

    def scatter_step(rb, _, g0):
        c0 = pl.multiple_of(rb * cpb, cpb)
        r0 = pl.multiple_of(rb * S5_RB, S5_RB)
        for hf in range(2):
            ys = [yall_ref[gi, pl.ds(r0, S5_RB), hf * LANES:(hf + 1) * LANES] for gi in range(S5_GS)]
            for k in range(S5_GB):
                acc = ys[0]
                for gi in range(1, S5_GS):
                    acc = jnp.where(lane_grp == (k + g0 + gi) % S5_GB, ys[gi], acc)
                if k:
                    acc = pltpu.roll(acc, (S5_GB - k) * S5_H, 1)
                t = hf * S5_GB + k
                dst = (pl.ds(c0, cpb), slice(t * B, (t + 1) * B), slice(None))
                if g0 > 0:
                    acc = jnp.where(lane_grp >= g0, acc, y_ref[dst].reshape(S5_RB, LANES))
                y_ref[dst] = acc.reshape(cpb, B, LANES)
        return 0

    for g0 in range(0, S5_GB, S5_GS):
        lax.fori_loop(0, rows // S5_RB, functools.partial(gather_step, g0=g0), 0)
        lax.fori_loop(0, S5_GS // 2, functools.partial(group_step, g0=g0), 0)
        lax.fori_loop(0, rows // S5_RB, functools.partial(scatter_step, g0=g0), 0)


def _s5_mix(ua, ops, h0, B):
    m, p, q, a16 = ops
    rows_all = ua.shape[0]
    nc = rows_all // (S5_CHUNK * B)
    rows = nc * B
    up3 = ua.reshape(nc, S5_CHUNK * B, W_A)
    gblk = lambda r: pl.BlockSpec((S5_GB, r, S5_W), lambda j: (j, 0, 0))
    tok = pl.BlockSpec((nc, S5_CHUNK * B, LANES), lambda j: (0, 0, j))
    y, fin = pl.pallas_call(
        functools.partial(_s5_kernel, nc=nc, B=B),
        grid=(W_A // LANES,),
        in_specs=[tok, gblk(S5_W), gblk(S5_W), gblk(S5_W), gblk(1), gblk(B)],
        out_specs=[tok, gblk(B)],
        out_shape=[jax.ShapeDtypeStruct((nc, S5_CHUNK * B, W_A), F32),
                   jax.ShapeDtypeStruct((S5_G, B, S5_W), F32)],
        scratch_shapes=[pltpu.VMEM((S5_GS, rows, S5_W), BF16), pltpu.VMEM((S5_GS, rows, S5_W), F32),
                        pltpu.VMEM((2, rows, S5_W), F32), pltpu.VMEM((2, rows, S5_W), F32),
                        pltpu.VMEM((2, rows, S5_W), F32)],
        compiler_params=_cparams(1),
        name="s5_mix",
    )(up3, m, p, q, a16, h0)
    return y.reshape(rows_all, W_A), fin


def _s5_ops_kernel(lr_ref, li_ref, ldt_ref, br_ref, bi_ref, cr_ref, ci_ref, m_ref, p_ref, qt_ref, a_ref):
    T = S5_CHUNK
    H = S5_H
    lam_re = lr_ref[0]
    lam_im = li_ref[0]
    dt = jnp.exp(ldt_ref[0])
    mag = jnp.exp(lam_re * dt)
    ang = lam_im * dt
    a_re = mag * jnp.cos(ang)
    a_im = mag * jnp.sin(ang)
    n_re = a_re - 1.0
    n_im = a_im
    den = lam_re * lam_re + lam_im * lam_im
    k_re = (n_re * lam_re + n_im * lam_im) / den
    k_im = (n_im * lam_re - n_re * lam_im) / den
    b_re = br_ref[0]
    b_im = bi_ref[0]
    bb_re = k_re * b_re - k_im * b_im
    bb_im = k_re * b_im + k_im * b_re
    c_re = cr_ref[0]
    c_im = ci_ref[0]
    ap = [(jnp.ones_like(a_re), jnp.zeros_like(a_re))]
    for _ in range(T):
        pr, pi = ap[-1]
        ap.append((pr * a_re - pi * a_im, pr * a_im + pi * a_re))
    is_fwd = lax.broadcasted_iota(jnp.int32, (1, LANES), 1) < S5_P

    def powers(kf, kb):
        return jnp.where(is_fwd, ap[kf][0], ap[kb][0]), jnp.where(is_fwd, ap[kf][1], ap[kb][1])

    g8 = pl.program_id(0) % S5_GB

    def pos_rows(t):
        p = (t // S5_GB) * S5_GB + (t % S5_GB + g8) % S5_GB
        return pl.ds(pl.multiple_of(p * H, H), H)

    for t in range(T):
        rows = pos_rows(t)
        er, ei = powers(T - 1 - t, t)
        p_ref[0, rows, :LANES] = (er * bb_re - ei * bb_im).astype(BF16)
        p_ref[0, rows, LANES:] = (er * bb_im + ei * bb_re).astype(BF16)
        er, ei = powers(t + 1, T - t)
        qt_ref[0, rows, :LANES] = (c_re * er - c_im * ei).astype(BF16)
        qt_ref[0, rows, LANES:] = (-(c_re * ei + c_im * er)).astype(BF16)
    a_ref[0, :, :LANES] = ap[T][0]
    a_ref[0, :, LANES:] = ap[T][1]

    def ca(k):
        return c_re * ap[k][0] - c_im * ap[k][1], c_re * ap[k][1] + c_im * ap[k][0]

    fwd = [ca(k) for k in range(T)]
    bwd = fwd[::-1]
    nt = lambda a, b: lax.dot_general(a, b, (((1,), (1,)), ((), ())), precision=HIGHEST,
                                      preferred_element_type=F32)
    cat = lambda parts, i: jnp.concatenate([p[i] for p in parts], axis=0)
    zero = jnp.zeros_like(bb_re)
    wf = (nt(jnp.where(is_fwd, bb_re, zero), cat(fwd, 0)) - nt(jnp.where(is_fwd, bb_im, zero), cat(fwd, 1)))
    wb = (nt(jnp.where(is_fwd, zero, bb_re), cat(bwd, 0)) - nt(jnp.where(is_fwd, zero, bb_im), cat(bwd, 1)))
    pad = jnp.zeros((H, S5_W), F32)
    wf_pad = jnp.concatenate([pad, wf], axis=1)
    wb_pad = jnp.concatenate([wb, pad], axis=1)
    for t in range(T):
        row = (wf_pad[:, S5_W - H * t:2 * S5_W - H * t] + wb_pad[:, (T - 1 - t) * H:(T - 1 - t) * H + S5_W])
        row = jnp.concatenate([pltpu.roll(row[:, :LANES], g8 * H, 1), pltpu.roll(row[:, LANES:], g8 * H, 1)],
                              axis=1)
        m_ref[0, pos_rows(t), :] = row.astype(BF16)


def _s5_operators(lam_re, lam_im, log_dt, b_re, b_im, c_re, c_im):
    G = S5_G
    dirs = lambda x: jnp.concatenate([x[0], x[1]], axis=-1)
    lam = [dirs(x)[:, None, :] for x in (lam_re, lam_im)]
    ldt = dirs(jnp.broadcast_to(log_dt[..., None], (2, G, S5_P)))[:, None, :]
    bt = [dirs(x.transpose(0, 1, 3, 2)) for x in (b_re, b_im)]
    ct = [dirs(x) for x in (c_re, c_im)]
    vec = pl.BlockSpec((1, 1, LANES), lambda g: (g, 0, 0))
    mat = pl.BlockSpec((1, S5_H, LANES), lambda g: (g, 0, 0))
    op = pl.BlockSpec((1, S5_W, S5_W), lambda g: (g, 0, 0))
    return pl.pallas_call(
        _s5_ops_kernel,
        grid=(G,),
        in_specs=[vec, vec, vec, mat, mat, mat, mat],
        out_specs=[op, op, op, pl.BlockSpec((1, 1, S5_W), lambda g: (g, 0, 0))],
        out_shape=[jax.ShapeDtypeStruct((G, S5_W, S5_W), BF16)] * 3
        + [jax.ShapeDtypeStruct((G, 1, S5_W), F32)],
        compiler_params=_cparams(1),
        name="s5_operators",
    )(*lam, ldt, *bt, *ct)


def _phase_tables(u, va, vb, ncb, n):
    def trig(m):
        m = m & (2 * n - 1)
        m = jnp.where(m >= n, m - 2 * n, m)
        ang = m.astype(F32) * (math.pi / n)
        return jnp.cos(ang), jnp.sin(ang)

    lane = lax.broadcasted_iota(jnp.int32, (1, LANES), 1)
    cb, sb = trig(u * (va * lane + vb))
    ca_all, sa_all = trig(u * (va * LANES * lane))
    out = []
    for ch in range(ncb):
        ca = ca_all[:, ch:ch + 1]
        sa = sa_all[:, ch:ch + 1]
        out.append((ca * cb - sa * sb, sa * cb + ca * sb))
    return out


def _tables_kernel(fs_ref, fst_ref, blk_ref, *, Ls):
    u = 2 * lax.broadcasted_iota(jnp.int32, (Ls, 1), 0) + 1
    for ch, (c, s) in enumerate(_phase_tables(u, 1, 0, Ls // LANES, 2 * Ls)):
        blk_ref[:Ls, ch * LANES:(ch + 1) * LANES] = c
        blk_ref[Ls:, ch * LANES:(ch + 1) * LANES] = -s
    blk = blk_ref[...]
    fs_ref[...] = blk.astype(BF16)
    fst_ref[...] = blk.T.astype(BF16)


def _dft_tables(Ls):
    return pl.pallas_call(
        functools.partial(_tables_kernel, Ls=Ls),
        out_shape=[jax.ShapeDtypeStruct((2 * Ls, Ls), BF16), jax.ShapeDtypeStruct((Ls, 2 * Ls), BF16)],
        scratch_shapes=[pltpu.VMEM((2 * Ls, Ls), F32)],
        compiler_params=pltpu.CompilerParams(vmem_limit_bytes=VMEM_LIMIT),
        name="dft_tables",
    )()


def _spectrum_kernel(fs_ref, f_ref, w_ref, *, S, Ls):
    scale = 1.0 / Ls
    k = lax.broadcasted_iota(jnp.int32, (Ls, 1), 0)
    sgn = jnp.where(k % 2 == 0, 1.0, -1.0)
    prev = None
    for i in range(S + 1):
        re = im = None
        if i < S:
            g = jnp.dot(fs_ref[...], f_ref[i * Ls:(i + 1) * Ls, :], preferred_element_type=F32) * scale
            re, im = g[:Ls], g[Ls:]
        cur = (re, im)
        if prev is not None:
            jre, jim = -sgn * prev[1], sgn * prev[0]
            re = jre if re is None else re + jre
            im = jim if im is None else im + jim
        w_ref[0, i, :Ls] = re.astype(w_ref.dtype)
        w_ref[0, i, Ls:] = im.astype(w_ref.dtype)
        prev = cur


def _filter_spectrum(fs, filt, S, C, dtype):
    n2, Ls = fs.shape
    L = filt.shape[0]
    nct = W_B // C
    return pl.pallas_call(
        functools.partial(_spectrum_kernel, S=S, Ls=Ls),
        grid=(2, nct),
        in_specs=[pl.BlockSpec((n2, Ls), lambda o, ct: (0, 0)),
                  pl.BlockSpec((L, C), lambda o, ct: (0, o * nct + ct))],
        out_specs=pl.BlockSpec((1, S + 1, n2, C), lambda o, ct: (o, 0, 0, ct)),
        out_shape=jax.ShapeDtypeStruct((2, S + 1, n2, W_B), dtype),
        compiler_params=_cparams(2),
        name="filter_spectrum",
    )(fs, filt)


def _hyena_filters(L, w1, b1, w2, b2, freq, w3):
    t = jnp.arange(L, dtype=F32)
    tn = t / (L - 1)
    f = jnp.linspace(1e-4, HY_BANDS - 1, HY_BANDS, dtype=F32)
    ang = (2.0 * math.pi / L) * t[:, None] * f[None, :]
    z = jnp.concatenate([tn[:, None], jnp.cos(ang), -jnp.sin(ang)], axis=-1)
    h = jnp.sin(freq * (jnp.dot(z, w1, precision=HIGHEST) + b1))
    h = jnp.sin(freq * (jnp.dot(h, w2, precision=HIGHEST) + b2))
    h = jnp.dot(h, w3, precision=HIGHEST).reshape(L, 2, W_B)
    max_decay = math.log(HY_TARGET) / HY_FAST_PCT
    min_decay = math.log(HY_TARGET) / HY_SLOW_PCT
    deltas = jnp.abs(jnp.linspace(min_decay, max_decay, W_B, dtype=F32))
    half = L // 2
    off = jnp.abs(t - half) / half
    win = jnp.exp(-off[:, None] * deltas[None, :]) + HY_SHIFT
    return h * win[:, None, :]


HY_RF = 32
HY_R = 128
SUBLANES = 8


def _short_conv_rows(u_ref, w, bias, c, R, L):
    r0 = pl.multiple_of(c * R, R)
    cur = u_ref[pl.ds(r0, R), :]
    before = u_ref[pl.ds(pl.multiple_of(jnp.maximum(r0 - SUBLANES, 0), SUBLANES), SUBLANES), :]
    after = u_ref[pl.ds(pl.multiple_of(jnp.minimum(r0 + R, L - SUBLANES), SUBLANES), SUBLANES), :]
    before = jnp.where(c == 0, 0.0, before[SUBLANES - 1:])
    after = jnp.where(c == L // R - 1, 0.0, after[:1])
    row = lax.broadcasted_iota(jnp.int32, (SUBLANES, 1), 0)
    up = pltpu.roll(cur, 1, 0)
    up = jnp.concatenate([jnp.where(row == 0, before, up[:SUBLANES]), up[SUBLANES:]], axis=0)
    dn = pltpu.roll(cur, R - 1, 0)
    dn = jnp.concatenate([dn[:R - SUBLANES], jnp.where(row == SUBLANES - 1, after, dn[R - SUBLANES:])], axis=0)
    return bias + up * w[0:1] + cur * w[1:2] + dn * w[2:3]


def _hy_kernel(v_ref, x_ref, cwv_ref, cbv_ref, cwx_ref, cbx_ref, hb_ref, fs_ref, fst_ref, w_ref, o_ref,
               zf_ref, zb_ref, u_ref, d_ref, os_ref, *, L, S, R):
    Ls = L // S
    C = zf_ref.shape[1]
    o = pl.program_id(2)
    seg = lambda j: slice(j * C, (j + 1) * C)

    @pl.when(o == 0)
    def _():
        for j in range(S):
            def body(cc, _, j=j):
                c = j * (Ls // R) + cc
                z = _short_conv_rows(v_ref, cwv_ref[...], cbv_ref[...], c, R, L)
                zf_ref[pl.ds(pl.multiple_of(c * R, R), R), :] = z
                zb_ref[pl.ds(pl.multiple_of(cc * R, R), R), seg(j)] = z.astype(BF16)
                return 0
            lax.fori_loop(0, Ls // R, body, 0)

    u_ref[...] = jnp.dot(fs_ref[...], zb_ref[...], preferred_element_type=F32).astype(u_ref.dtype)

    def prod(fc, _):
        f0 = pl.multiple_of(fc * HY_RF, HY_RF)
        re_rows = pl.ds(f0, HY_RF)
        im_rows = pl.ds(Ls + f0, HY_RF)
        for r in range(S):
            dre = dim = None
            for j in range(S):
                d = r - j + S // 2
                if 0 <= d <= S:
                    ur, ui = u_ref[re_rows, seg(j)], u_ref[im_rows, seg(j)]
                    wr, wi = w_ref[0, d, re_rows, :], w_ref[0, d, im_rows, :]
                    pre, pim = ur * wr - ui * wi, ur * wi + ui * wr
                    dre = pre if dre is None else dre + pre
                    dim = pim if dim is None else dim + pim
            d_ref[re_rows, seg(r)] = dre.astype(BF16)
            d_ref[im_rows, seg(r)] = dim.astype(BF16)
        return 0
    lax.fori_loop(0, Ls // HY_RF, prod, 0)

    os_ref[...] = jnp.dot(fst_ref[...], d_ref[...], preferred_element_type=F32)

    for r in range(S):
        def gate(cc, _, r=r):
            c = r * (Ls // R) + cc
            rows = pl.ds(pl.multiple_of(c * R, R), R)
            local = pl.ds(pl.multiple_of(cc * R, R), R)
            x = _short_conv_rows(x_ref, cwx_ref[...], cbx_ref[...], c, R, L)
            z = x * (os_ref[local, seg(r)] + zf_ref[rows, :] * hb_ref[0])

            @pl.when(o == 0)
            def _():
                zf_ref[rows, :] = z
                zb_ref[local, seg(r)] = z.astype(BF16)

            @pl.when(o == 1)
            def _():
                o_ref[rows, :] = z
            return 0
        lax.fori_loop(0, Ls // R, gate, 0)


def _hyena_mix(ut, B, conv_w, conv_b, bias, fs, fst, w, C):
    L = ut.shape[0]
    n2, Ls = fs.shape
    S = L // Ls
    nct = W_B // C
    ncol = UT_COLS // C
    xcol = lambda o, ct: (1 + o) * nct + ct
    conv_b = conv_b.reshape(1, -1)
    R = min(HY_R, Ls)
    return pl.pallas_call(
        functools.partial(_hy_kernel, L=L, S=S, R=R),
        grid=(B, nct, 2),
        in_specs=[pl.BlockSpec((L, C), lambda b, ct, o: (0, b * ncol + ct)),
                  pl.BlockSpec((L, C), lambda b, ct, o: (0, b * ncol + xcol(o, ct))),
                  pl.BlockSpec((3, C), lambda b, ct, o: (0, ct)),
                  pl.BlockSpec((1, C), lambda b, ct, o: (0, ct)),
                  pl.BlockSpec((3, C), lambda b, ct, o: (0, xcol(o, ct))),
                  pl.BlockSpec((1, C), lambda b, ct, o: (0, xcol(o, ct))),
                  pl.BlockSpec((1, 1, C), lambda b, ct, o: (o, 0, ct)),
                  pl.BlockSpec((n2, Ls), lambda b, ct, o: (0, 0)),
                  pl.BlockSpec((Ls, n2), lambda b, ct, o: (0, 0)),
                  pl.BlockSpec((1, S + 1, n2, C), lambda b, ct, o: (o, 0, 0, ct))],
        out_specs=pl.BlockSpec((L, C), lambda b, ct, o: (0, b * nct + ct)),
        out_shape=jax.ShapeDtypeStruct((L, B * W_B), F32),
        scratch_shapes=[pltpu.VMEM((L, C), F32), pltpu.VMEM((Ls, S * C), BF16), pltpu.VMEM((n2, S * C), w.dtype),
                        pltpu.VMEM((n2, S * C), BF16), pltpu.VMEM((Ls, S * C), F32)],
        compiler_params=_cparams(3),
        name="hyena_mix",
    )(ut, ut, conv_w, conv_b, conv_w, conv_b, bias.reshape(2, 1, W_B), fs, fst, w)


POOL_PAD = 16


def _pool_kernel(u_ref, w_ref, sc_ref, o_ref, *, L):
    assert POOL_WINDOWS == (2, 4, 8, 16)
    n = L + 2 * POOL_PAD
    shifted = lambda x, s: pltpu.roll(x, (-s) % n, 0)
    for gi, win in enumerate(POOL_WINDOWS):
        u = u_ref[:, gi * POOL_G:(gi + 1) * POOL_G]
        pad = jnp.zeros((POOL_PAD, POOL_G), F32)
        ext = jnp.concatenate([pad, u, pad], axis=0)
        s = ext + shifted(ext, -1)
        half = 1
        while 2 * half < win:
            s = shifted(s, -half) + shifted(s, half)
            half *= 2
        s = s[POOL_PAD:POOL_PAD + L]

        def mean_minus_token(sl, t0):
            t = t0 + lax.broadcasted_iota(jnp.int32, (SUBLANES, 1), 0)
            cnt = jnp.minimum(t - win // 2 + win, L) - jnp.maximum(t - win // 2, 0)
            return s[sl] / cnt.astype(F32) - u[sl]

        edge = SUBLANES
        d = jnp.concatenate([mean_minus_token(slice(0, edge), 0),
                             s[edge:L - edge] * (1.0 / win) - u[edge:L - edge],
                             mean_minus_token(slice(L - edge, L), L - edge)], axis=0)
        o_ref[:, gi * POOL_G:(gi + 1) * POOL_G] = (
            jnp.dot(d.astype(BF16), w_ref[gi], preferred_element_type=F32)
            * sc_ref[:, gi * POOL_G:(gi + 1) * POOL_G])


def _pool_mix(ut, B, w, scale):
    L = ut.shape[0]
    nblk = UT_COLS // W_C
    return pl.pallas_call(
        functools.partial(_pool_kernel, L=L),
        grid=(B,),
        in_specs=[pl.BlockSpec((L, W_C), lambda b: (0, b * nblk + nblk - 1)),
                  pl.BlockSpec((len(POOL_WINDOWS), POOL_G, POOL_G), lambda b: (0, 0, 0)),
                  pl.BlockSpec((1, W_C), lambda b: (0, 0))],
        out_specs=pl.BlockSpec((L, W_C), lambda b: (0, b)),
        out_shape=jax.ShapeDtypeStruct((L, B * W_C), F32),
        compiler_params=_cparams(1),
        name="pool_mix",
    )(ut, w, scale.reshape(1, W_C))


def _silu(x):
    return x * jax.nn.sigmoid(x)


def _gelu_tanh(x):
    return x * (0.5 * (1.0 + jnp.tanh(math.sqrt(2.0 / math.pi) * (x + 0.044715 * (x * x * x)))))


def _out_kernel(*refs, has_pos, batch_major, final, B, bm):
    refs = list(refs)
    x_ref = refs.pop(0)
    pos_ref = refs.pop(0) if has_pos else None
    (mod_ref, h_ref, ua_ref, ya_ref, zb_ref, yc_ref, wga_ref, wgb_ref, wgc_ref, wma_ref, wmb_ref, wmc_ref,
     d_ref, wglu_ref, bglu_ref, wa_ref, wb_ref, wc_ref, wo_ref) = refs[:19]
    refs = refs[19:]
    fg_ref = refs.pop(0) if final else None
    xo_ref, zs_ref, ys_ref = refs[:3]
    xs_ref = refs[3] if (batch_major or final) else None
    for b in range(B):
        _put_batch(zs_ref, b, B, zb_ref[:, b * W_B:(b + 1) * W_B])
        _put_batch(ys_ref, b, B, yc_ref[:, b * W_C:(b + 1) * W_C])
    mod = mod_ref[...]
    h = h_ref[...]
    proj = lambda w_ref: jnp.dot(h, w_ref[0], preferred_element_type=F32)
    y = _gelu_tanh(ya_ref[...] + ua_ref[...] * d_ref[0])
    y = y * jax.nn.sigmoid(jnp.dot(y.astype(BF16), wglu_ref[0], preferred_element_type=F32) + bglu_ref[0])
    y_a = (y * _silu(proj(wga_ref))).astype(BF16)
    y_b = (_get_tile(zs_ref) * _silu(proj(wgb_ref))).astype(BF16)
    y_c = (_get_tile(ys_ref) * _silu(proj(wgc_ref))).astype(BF16)
    merged = (jax.nn.sigmoid(proj(wma_ref)) * jnp.dot(y_a, wa_ref[0], preferred_element_type=F32)
              + jax.nn.sigmoid(proj(wmb_ref)) * jnp.dot(y_b, wb_ref[0], preferred_element_type=F32)
              + jax.nn.sigmoid(proj(wmc_ref)) * jnp.dot(y_c, wc_ref[0], preferred_element_type=F32))
    out = jnp.dot(merged.astype(BF16), wo_ref[0], preferred_element_type=F32)
    x = _load_x(x_ref, xs_ref if batch_major else None, pos_ref, B, bm)
    x_new = x + _mod_part(mod, 2, bm) * _by_batch(out, bm)
    if final:
        r = lax.rsqrt(jnp.mean(x_new * x_new, axis=-1, keepdims=True) + EPS)
        _put_tile(xs_ref, (x_new * r * fg_ref[...]).reshape(-1, D_MODEL))
        for b in range(B):
            xo_ref[b] = _get_batch(xs_ref, b, B)
    else:
        xo_ref[...] = x_new.reshape(-1, D_MODEL)


def _out_proj(x, pos, mod, mod_blk, B, bm, h, ua, ya, zb, yc, w_in, l, d, w_glu, b_glu,
              w_a, w_b, w_c, w_o, final_g, tm):
    D = D_MODEL
    rows = x.size // D
    L = rows // B
    batch_major = x.ndim == 3
    final = final_g is not None
    tok = lambda w: pl.BlockSpec((tm, w), lambda i: (i, 0))
    seq = lambda w: pl.BlockSpec((tm // B, B * w), lambda i: (i, 0))
    full = lambda a: _const_spec(a.shape, (0,) * a.ndim)
    in_specs, args = _x_specs(x, pos, mod, mod_blk, B, bm, tm)
    in_specs += [tok(D), tok(W_A), tok(W_A), seq(W_B), seq(W_C)]
    args += [h, ua, ya, zb, yc]
    for cb in G_BLOCKS:
        in_specs.append(_const_spec((1, D, CB), (l, 0, cb)))
        args.append(w_in)
    for k in range(3):
        in_specs.append(_const_spec((1, D, D), (l, 0, M_BLOCK0 + k)))
        args.append(w_in)
    weights = [d, w_glu, b_glu, w_a, w_b, w_c, w_o]
    in_specs += [_const_spec((1,) + a.shape[1:], (l, 0, 0)) for a in weights]
    args += weights
    if final:
        in_specs.append(full(final_g))
        args.append(final_g)
    scratch = [_stage(tm, W_B), _stage(tm, W_C)]
    if batch_major or final:
        scratch.append(_stage(tm, D))
    if final:
        out_spec = pl.BlockSpec((B, tm // B, D), lambda i: (0, i, 0))
        out_shape = jax.ShapeDtypeStruct((B, L, D), F32)
    else:
        out_spec = tok(D)
        out_shape = jax.ShapeDtypeStruct((rows, D), F32)
    return pl.pallas_call(
        functools.partial(_out_kernel, has_pos=pos is not None, batch_major=batch_major, final=final,
                          B=B, bm=bm),
        grid=(rows // tm,),
        in_specs=in_specs,
        out_specs=out_spec,
        out_shape=out_shape,
        scratch_shapes=scratch,
        compiler_params=_cparams(1),
        name="out_proj",
    )(*args)


def _grid_pos_embed(L):
    rows = L // GRID_W
    r = jnp.broadcast_to(jnp.arange(rows, dtype=F32)[:, None], (rows, GRID_W)).reshape(-1)
    col = jnp.broadcast_to(jnp.arange(GRID_W, dtype=F32)[None, :], (rows, GRID_W)).reshape(-1)
    q = D_MODEL // 4
    omega = 1.0 / (10000.0 ** (jnp.arange(q, dtype=F32) / q))
    ar = r[:, None] * omega[None, :]
    ac = col[:, None] * omega[None, :]
    return jnp.concatenate([jnp.sin(ar), jnp.cos(ar), jnp.sin(ac), jnp.cos(ac)], axis=-1)


def _states_to_lanes(st_re, st_im):
    s = jnp.concatenate([st_re[:, 0], st_re[:, 1], st_im[:, 0], st_im[:, 1]], axis=-1)
    return s.transpose(1, 0, 2)


def _lanes_to_states(fin):
    f = fin.transpose(1, 0, 2).reshape(fin.shape[1], S5_G, 4, S5_P)
    return (jnp.stack([f[:, :, 0], f[:, :, 1]], axis=1), jnp.stack([f[:, :, 2], f[:, :, 3]], axis=1))


def kernel(x_prompt, x_sample, c, state_s5_re, state_s5_im, c_ctx, norm_g, w_mod, b_mod, w_in, s5_lam_re, s5_lam_im, s5_log_dt, s5_b_re, s5_b_im, s5_c_re, s5_c_im, s5_d, s5_w_glu, s5_b_glu, hy_conv_w, hy_conv_b, hy_f_w1, hy_f_b1, hy_f_w2, hy_f_b2, hy_f_freq, hy_f_w3, hy_bias, pool_w, pool_scale, w_br_a, w_br_b, w_br_c, w_out, final_g):
    Bc, Lc, D = x_prompt.shape
    Bl, Ll, _ = x_sample.shape
    assert Bl == MOD_ROWS - MOD_LAT0

    cond = jnp.zeros((MOD_ROWS, D), F32).at[0].set(c_ctx).at[MOD_LAT0:].set(c)
    mod = _modulation(cond, w_mod, b_mod).reshape(DEPTH * MOD_ROWS, 3 * D)

    pos = _grid_pos_embed(Ll)
    groups = {
        'ctx': dict(B=Bc, L=Lc, S=2, C=W_B, hy_dtype=F32, bm=1, mod_blk=0),
        'lat': dict(B=Bl, L=Ll, S=4, C=W_B // 2, hy_dtype=BF16, bm=Bl, mod_blk=1),
    }
    tables = {k: _dft_tables(v['L'] // v['S']) for k, v in groups.items()}
    w_in_b = w_in.astype(BF16)
    out_weights = [s5_d.reshape(DEPTH, 1, W_A), s5_w_glu.astype(BF16), s5_b_glu.reshape(DEPTH, 1, W_A),
                   w_br_a.astype(BF16), w_br_b.astype(BF16), w_br_c.astype(BF16), w_out.astype(BF16)]
    xs = {'ctx': x_prompt, 'lat': x_sample}
    new_re, new_im = [], []
    for l in range(DEPTH):
        s5_ops = _s5_operators(s5_lam_re[l], s5_lam_im[l], s5_log_dt[l], s5_b_re[l], s5_b_im[l],
                               s5_c_re[l], s5_c_im[l])
        g = norm_g[l].reshape(1, D)
        for name, cfg in groups.items():
            B, L, bm = cfg['B'], cfg['L'], cfg['bm']
            x = xs[name]
            p = pos if (name == 'lat' and l == 0) else None
            mod_blk = l * (MOD_ROWS // MOD_LAT0) + cfg['mod_blk']
            ua, ut, hn = _in_proj(x, p, mod, mod_blk, B, bm, g, w_in_b, l, min(TM_IN, L * B))
            if name == 'ctx':
                h0g = jnp.zeros((S5_G, B, 4 * S5_P), F32)
            else:
                h0g = _states_to_lanes(state_s5_re[:, l], state_s5_im[:, l])
            ya, fin = _s5_mix(ua, s5_ops, h0g, B)
            if name == 'ctx':
                fr, fi = _lanes_to_states(fin)
                new_re.append(fr)
                new_im.append(fi)
            fs, fst = tables[name]
            filt = _hyena_filters(L, hy_f_w1[l], hy_f_b1[l], hy_f_w2[l], hy_f_b2[l], hy_f_freq[l], hy_f_w3[l])
            hw = _filter_spectrum(fs, filt.reshape(L, 2 * W_B).astype(BF16), cfg['S'], cfg['C'], cfg['hy_dtype'])
            zb = _hyena_mix(ut, B, hy_conv_w[l], hy_conv_b[l], hy_bias[l], fs, fst, hw, cfg['C'])
            yc = _pool_mix(ut, B, pool_w[l].astype(BF16), pool_scale[l])
            fg = final_g.reshape(1, D) if l == DEPTH - 1 else None
            xs[name] = _out_proj(x, p, mod, mod_blk, B, bm, hn, ua, ya, zb, yc, w_in_b, l,
                                 *out_weights, fg, min(TM_OUT, L * B))
    return (xs['ctx'], xs['lat'], jnp.stack(new_re, axis=1), jnp.stack(new_im, axis=1))
```

```python
import functools
import math

import jax
import jax.numpy as jnp
from jax import lax
from jax.experimental import pallas as pl
from jax.experimental.pallas import tpu as pltpu

F32 = jnp.float32
BF16 = jnp.bfloat16
HIGHEST = lax.Precision.HIGHEST

D_MODEL = 1024
DEPTH = 2
GRID_W = 64
EPS = 1e-6
W_A = D_MODEL // 2
S5_H = 16
S5_G = W_A // S5_H
S5_P = 64
W_B = D_MODEL // 2
HY_BANDS = 16
HY_FAST_PCT = 0.3
HY_SLOW_PCT = 1.5
HY_TARGET = 1e-2
HY_SHIFT = 0.05
W_C = D_MODEL // 2
POOL_WINDOWS = (2, 4, 8, 16)
POOL_G = W_C // 4
LANES = 128
CB = 512
U_BLOCKS = (0, 2, 3, 4, 6)
G_BLOCKS = (1, 5, 7)
M_BLOCK0 = 4
UT_COLS = CB * (len(U_BLOCKS) - 1)
S5_CHUNK = 16
S5_W = S5_CHUNK * S5_H
S5_GB = LANES // S5_H
MOD_ROWS = 16
MOD_LAT0 = 8
VMEM_LIMIT = 56 * 1024 * 1024
TM_IN = 1024
TM_OUT = 256


def _const_spec(block_shape, index):
    return pl.BlockSpec(block_shape, lambda *_: index, pipeline_mode=pl.Buffered(1))


def _cparams(n_grid):
    return pltpu.CompilerParams(dimension_semantics=("arbitrary",) * n_grid,
                                vmem_limit_bytes=VMEM_LIMIT)


def _mod_kernel(c_ref, w_ref, b_ref, o_ref):
    c = c_ref[...]
    s = c * jax.nn.sigmoid(c)
    o_ref[0] = jnp.dot(s, w_ref[0], preferred_element_type=F32, precision=HIGHEST) + b_ref[0]


def _modulation(cond, w_mod, b_mod, tn=3 * D_MODEL // 2):
    n = 3 * D_MODEL
    return pl.pallas_call(
        _mod_kernel,
        grid=(DEPTH, n // tn),
        in_specs=[pl.BlockSpec((MOD_ROWS, D_MODEL), lambda l, j: (0, 0)),
                  pl.BlockSpec((1, D_MODEL, tn), lambda l, j: (l, 0, j)),
                  pl.BlockSpec((1, 1, tn), lambda l, j: (l, 0, j))],
        out_specs=pl.BlockSpec((1, MOD_ROWS, tn), lambda l, j: (l, 0, j)),
        out_shape=jax.ShapeDtypeStruct((DEPTH, MOD_ROWS, n), F32),
        compiler_params=_cparams(2),
        name="modulation",
    )(cond, w_mod, b_mod.reshape(DEPTH, 1, n))


def _by_batch(x, bm):
    return x if bm == 1 else x.reshape(x.shape[0] // bm, bm, x.shape[1])


def _mod_part(mod, i, bm):
    m = mod[:bm, i * D_MODEL:(i + 1) * D_MODEL]
    return m if bm == 1 else m[None]


def _stage(rows, w):
    return pltpu.VMEM((w // LANES, rows, LANES), F32)


def _put_batch(ref, b, B, val):
    for c in range(ref.shape[0]):
        ref[c, pl.ds(b, val.shape[0], stride=B), :] = val[:, c * LANES:(c + 1) * LANES]


def _get_batch(ref, b, B):
    tt = ref.shape[1] // B
    return jnp.concatenate([ref[c, pl.ds(b, tt, stride=B), :] for c in range(ref.shape[0])], axis=1)


def _put_tile(ref, val):
    for c in range(ref.shape[0]):
        ref[c] = val[:, c * LANES:(c + 1) * LANES]


def _get_tile(ref):
    return jnp.concatenate([ref[c] for c in range(ref.shape[0])], axis=1)


def _load_x(x_ref, xs_ref, pos_ref, B, bm):
    if xs_ref is not None:
        for b in range(B):
            _put_batch(xs_ref, b, B, x_ref[b])
        x2 = _get_tile(xs_ref)
    else:
        x2 = x_ref[...]
    x = _by_batch(x2, bm)
    if pos_ref is not None:
        x = x + pos_ref[...][:, None, :]
    return x


def _normed(x, mod, g, bm):
    r = lax.rsqrt(jnp.mean(x * x, axis=-1, keepdims=True) + EPS)
    h = (x * r * g) * (1.0 + _mod_part(mod, 1, bm)) + _mod_part(mod, 0, bm)
    return h.reshape(-1, D_MODEL).astype(BF16)


def _in_kernel(*refs, has_pos, batch_major, B, bm):
    refs = list(refs)
    x_ref = refs.pop(0)
    pos_ref = refs.pop(0) if has_pos else None
    mod_ref, g_ref = refs[:2]
    w_refs = refs[2:2 + len(U_BLOCKS)]
    oa_ref, ot_ref, h_ref, scr_ref = refs[2 + len(U_BLOCKS):6 + len(U_BLOCKS)]
    xs_ref = refs[-1] if batch_major else None
    h = _normed(_load_x(x_ref, xs_ref, pos_ref, B, bm), mod_ref[...], g_ref[...], bm)
    h_ref[...] = h
    oa_ref[...] = jnp.dot(h, w_refs[0][0], preferred_element_type=F32)
    for i, w_ref in enumerate(w_refs[1:]):
        _put_tile(scr_ref, jnp.dot(h, w_ref[0], preferred_element_type=F32))
        for b in range(B):
            ot_ref[:, b * UT_COLS + i * CB:b * UT_COLS + (i + 1) * CB] = _get_batch(scr_ref, b, B)


def _x_specs(x, pos, mod, mod_blk, B, bm, tm):
    D = D_MODEL
    if x.ndim == 3:
        specs = [pl.BlockSpec((B, tm // B, D), lambda i: (0, i, 0))]
    else:
        specs = [pl.BlockSpec((tm, D), lambda i: (i, 0))]
    args = [x]
    if pos is not None:
        specs.append(pl.BlockSpec((tm // bm, D), lambda i: (i, 0)))
        args.append(pos)
    specs.append(_const_spec((MOD_LAT0, 3 * D), (mod_blk, 0)))
    args.append(mod)
    return specs, args


def _in_proj(x, pos, mod, mod_blk, B, bm, g, w_in, l, tm):
    D = D_MODEL
    rows = x.size // D
    L = rows // B
    batch_major = x.ndim == 3
    in_specs, args = _x_specs(x, pos, mod, mod_blk, B, bm, tm)
    in_specs.append(_const_spec((1, D), (0, 0)))
    args.append(g)
    for cb in U_BLOCKS:
        in_specs.append(_const_spec((1, D, CB), (l, 0, cb)))
        args.append(w_in)
    scratch = [_stage(tm, CB)] + ([_stage(tm, D)] if batch_major else [])
    return pl.pallas_call(
        functools.partial(_in_kernel, has_pos=pos is not None, batch_major=batch_major, B=B, bm=bm),
        grid=(rows // tm,),
        in_specs=in_specs,
        out_specs=[pl.BlockSpec((tm, W_A), lambda i: (i, 0)),
                   pl.BlockSpec((tm // B, B * UT_COLS), lambda i: (i, 0)),
                   pl.BlockSpec((tm, D), lambda i: (i, 0))],
        out_shape=[jax.ShapeDtypeStruct((rows, W_A), F32),
                   jax.ShapeDtypeStruct((L, B * UT_COLS), F32),
                   jax.ShapeDtypeStruct((rows, D), BF16)],
        scratch_shapes=scratch,
        compiler_params=_cparams(1),
        name="in_proj",
    )(*args)


S5_RB = 128
S5_GS = 4


def _s5_kernel(u_ref, m_ref, p_ref, qt_ref, a_ref, h0_ref, y_ref, fin_ref,
               x_ref, yall_ref, sloc_ref, sinf_ref, sinb_ref, *, nc, B):
    rows = nc * B
    cpb = S5_RB // B
    half = S5_W // 2
    lane_grp = lax.broadcasted_iota(jnp.int32, (S5_RB, LANES), 1) // S5_H

    lane = lax.broadcasted_iota(jnp.int32, (B, half), 1)
    is_fwd = lane < S5_P
    lane_full = lax.broadcasted_iota(jnp.int32, (rows, S5_W), 1)
    is_fwd_full = (lane_full % half) < S5_P

    def gather_step(rb, _, g0):
        c0 = pl.multiple_of(rb * cpb, cpb)
        r0 = pl.multiple_of(rb * S5_RB, S5_RB)
        slabs = [u_ref[pl.ds(c0, cpb), t * B:(t + 1) * B, :].reshape(S5_RB, LANES) for t in range(S5_CHUNK)]
        rolled = [s if t % S5_GB == 0 else pltpu.roll(s, (t % S5_GB) * S5_H, 1) for t, s in enumerate(slabs)]
        for gi in range(S5_GS):
            g8 = g0 + gi
            for hf in range(2):
                acc = rolled[hf * S5_GB]
                for k in range(1, S5_GB):
                    acc = jnp.where(lane_grp == (k + g8) % S5_GB, rolled[hf * S5_GB + k], acc)
                x_ref[gi, pl.ds(r0, S5_RB), hf * LANES:(hf + 1) * LANES] = acc.astype(BF16)
        return 0

    def group_step(pi, _, g0):
        gis = (2 * pi, 2 * pi + 1)
        ars, ais, init = [], [], []
        for k, gi in enumerate(gis):
            sloc_ref[k] = jnp.dot(x_ref[gi], p_ref[g0 + gi], preferred_element_type=F32)
            a = a_ref[g0 + gi]
            ars.append(a[:, :half])
            ais.append(a[:, half:])
            h0 = h0_ref[g0 + gi]
            init += [h0[:, :half], h0[:, half:]]

        def body(i, carry):
            jf = pl.multiple_of(i * B, B)
            jb = pl.multiple_of((nc - 1 - i) * B, B)
            out = []
            for k in range(2):
                cr, ci = carry[2 * k], carry[2 * k + 1]
                sinf_ref[k, pl.ds(jf, B), :half] = cr
                sinf_ref[k, pl.ds(jf, B), half:] = ci
                sinb_ref[k, pl.ds(jb, B), :half] = cr
                sinb_ref[k, pl.ds(jb, B), half:] = ci
                lf = sloc_ref[k, pl.ds(jf, B), :]
                lb = sloc_ref[k, pl.ds(jb, B), :]
                lr = jnp.where(is_fwd, lf[:, :half], lb[:, :half])
                li = jnp.where(is_fwd, lf[:, half:], lb[:, half:])
                out += [ars[k] * cr - ais[k] * ci + lr, ars[k] * ci + ais[k] * cr + li]
            return tuple(out)

        fin = lax.fori_loop(0, nc, body, tuple(init))
        for k, gi in enumerate(gis):
            g8 = g0 + gi
            fin_ref[g8, :, :half] = fin[2 * k]
            fin_ref[g8, :, half:] = fin[2 * k + 1]
            s_in = jnp.where(is_fwd_full, sinf_ref[k], sinb_ref[k]).astype(BF16)
            yall_ref[gi] = (jnp.dot(x_ref[gi], m_ref[g8], preferred_element_type=F32)
                            + lax.dot_general(s_in, qt_ref[g8], (((1,), (1,)), ((), ())),
                                              preferred_element_type=F32))
        return 0

    def scatter_step(rb, _, g0):
        c0 = pl.multiple_of(rb * cpb, cpb)
        r0 = pl.multiple_of(rb * S5_RB, S5_RB)
        for hf in range(2):
            ys = [yall_ref[gi, pl.ds(r0, S5_RB), hf * LANES:(hf + 1) * LANES] for gi in range(S5_GS)]
            for k in range(S5_GB):
                acc = ys[0]
                for gi in range(1, S5_GS):
                    acc = jnp.where(lane_grp == (k + g0 + gi) % S5_GB, ys[gi], acc)
                if k:
                    acc = pltpu.roll(acc, (S5_GB - k) * S5_H, 1)
                t = hf * S5_GB + k
                dst = (pl.ds(c0, cpb), slice(t * B, (t + 1) * B), slice(None))
                if g0 > 0:
                    acc = jnp.where(lane_grp >= g0, acc, y_ref[dst].reshape(S5_RB, LANES))
                y_ref[dst] = acc.reshape(cpb, B, LANES)
        return 0

    for g0 in range(0, S5_GB, S5_GS):
        lax.fori_loop(0, rows // S5_RB, functools.partial(gather_step, g0=g0), 0)
        lax.fori_loop(0, S5_GS // 2, functools.partial(group_step, g0=g0), 0)
        lax.fori_loop(0, rows // S5_RB, functools.partial(scatter_step, g0=g0), 0)


def _s5_mix(ua, ops, h0, B):
    m, p, q, a16 = ops
    rows_all = ua.shape[0]
    nc = rows_all // (S5_CHUNK * B)
    rows = nc * B
    up3 = ua.reshape(nc, S5_CHUNK * B, W_A)
    gblk = lambda r: pl.BlockSpec((S5_GB, r, S5_W), lambda j: (j, 0, 0))
    tok = pl.BlockSpec((nc, S5_CHUNK * B, LANES), lambda j: (0, 0, j))
    y, fin = pl.pallas_call(
        functools.partial(_s5_kernel, nc=nc, B=B),
        grid=(W_A // LANES,),
        in_specs=[tok, gblk(S5_W), gblk(S5_W), gblk(S5_W), gblk(1), gblk(B)],
        out_specs=[tok, gblk(B)],
        out_shape=[jax.ShapeDtypeStruct((nc, S5_CHUNK * B, W_A), F32),
                   jax.ShapeDtypeStruct((S5_G, B, S5_W), F32)],
        scratch_shapes=[pltpu.VMEM((S5_GS, rows, S5_W), BF16), pltpu.VMEM((S5_GS, rows, S5_W), F32),
                        pltpu.VMEM((2, rows, S5_W), F32), pltpu.VMEM((2, rows, S5_W), F32),
                        pltpu.VMEM((2, rows, S5_W), F32)],
        compiler_params=_cparams(1),
        name="s5_mix",
    )(up3, m, p, q, a16, h0)
    return y.reshape(rows_all, W_A), fin


def _s5_ops_kernel(lr_ref, li_ref, ldt_ref, br_ref, bi_ref, cr_ref, ci_ref, m_ref, p_ref, qt_ref, a_ref):
    T = S5_CHUNK
    H = S5_H
    lam_re = lr_ref[0]
    lam_im = li_ref[0]
    dt = jnp.exp(ldt_ref[0])
    mag = jnp.exp(lam_re * dt)
    ang = lam_im * dt
    a_re = mag * jnp.cos(ang)
    a_im = mag * jnp.sin(ang)
    n_re = a_re - 1.0
    n_im = a_im
    den = lam_re * lam_re + lam_im * lam_im
    k_re = (n_re * lam_re + n_im * lam_im) / den
    k_im = (n_im * lam_re - n_re * lam_im) / den
    b_re = br_ref[0]
    b_im = bi_ref[0]
    bb_re = k_re * b_re - k_im * b_im
    bb_im = k_re * b_im + k_im * b_re
    c_re = cr_ref[0]
    c_im = ci_ref[0]
    ap = [(jnp.ones_like(a_re), jnp.zeros_like(a_re))]
    for _ in range(T):
        pr, pi = ap[-1]
        ap.append((pr * a_re - pi * a_im, pr * a_im + pi * a_re))
    is_fwd = lax.broadcasted_iota(jnp.int32, (1, LANES), 1) < S5_P

    def powers(kf, kb):
        return jnp.where(is_fwd, ap[kf][0], ap[kb][0]), jnp.where(is_fwd, ap[kf][1], ap[kb][1])

    g8 = pl.program_id(0) % S5_GB

    def pos_rows(t):
        p = (t // S5_GB) * S5_GB + (t % S5_GB + g8) % S5_GB
        return pl.ds(pl.multiple_of(p * H, H), H)

    for t in range(T):
        rows = pos_rows(t)
        er, ei = powers(T - 1 - t, t)
        p_ref[0, rows, :LANES] = (er * bb_re - ei * bb_im).astype(BF16)
        p_ref[0, rows, LANES:] = (er * bb_im + ei * bb_re).astype(BF16)
        er, ei = powers(t + 1, T - t)
        qt_ref[0, rows, :LANES] = (c_re * er - c_im * ei).astype(BF16)
        qt_ref[0, rows, LANES:] = (-(c_re * ei + c_im * er)).astype(BF16)
    a_ref[0, :, :LANES] = ap[T][0]
    a_ref[0, :, LANES:] = ap[T][1]

    def ca(k):
        return c_re * ap[k][0] - c_im * ap[k][1], c_re * ap[k][1] + c_im * ap[k][0]

    fwd = [ca(k) for k in range(T)]
    bwd = fwd[::-1]
    nt = lambda a, b: lax.dot_general(a, b, (((1,), (1,)), ((), ())), precision=HIGHEST,
                                      preferred_element_type=F32)
    cat = lambda parts, i: jnp.concatenate([p[i] for p in parts], axis=0)
    zero = jnp.zeros_like(bb_re)
    wf = (nt(jnp.where(is_fwd, bb_re, zero), cat(fwd, 0)) - nt(jnp.where(is_fwd, bb_im, zero), cat(fwd, 1)))
    wb = (nt(jnp.where(is_fwd, zero, bb_re), cat(bwd, 0)) - nt(jnp.where(is_fwd, zero, bb_im), cat(bwd, 1)))
    pad = jnp.zeros((H, S5_W), F32)
    wf_pad = jnp.concatenate([pad, wf], axis=1)
    wb_pad = jnp.concatenate([wb, pad], axis=1)
    for t in range(T):
        row = (wf_pad[:, S5_W - H * t:2 * S5_W - H * t] + wb_pad[:, (T - 1 - t) * H:(T - 1 - t) * H + S5_W])
        row = jnp.concatenate([pltpu.roll(row[:, :LANES], g8 * H, 1), pltpu.roll(row[:, LANES:], g8 * H, 1)],
                              axis=1)
        m_ref[0, pos_rows(t), :] = row.astype(BF16)


def _s5_operators(lam_re, lam_im, log_dt, b_re, b_im, c_re, c_im):
    G = S5_G
    dirs = lambda x: jnp.concatenate([x[0], x[1]], axis=-1)
    lam = [dirs(x)[:, None, :] for x in (lam_re, lam_im)]
    ldt = dirs(jnp.broadcast_to(log_dt[..., None], (2, G, S5_P)))[:, None, :]
    bt = [dirs(x.transpose(0, 1, 3, 2)) for x in (b_re, b_im)]
    ct = [dirs(x) for x in (c_re, c_im)]
    vec = pl.BlockSpec((1, 1, LANES), lambda g: (g, 0, 0))
    mat = pl.BlockSpec((1, S5_H, LANES), lambda g: (g, 0, 0))
    op = pl.BlockSpec((1, S5_W, S5_W), lambda g: (g, 0, 0))
    return pl.pallas_call(
        _s5_ops_kernel,
        grid=(G,),
        in_specs=[vec, vec, vec, mat, mat, mat, mat],
        out_specs=[op, op, op, pl.BlockSpec((1, 1, S5_W), lambda g: (g, 0, 0))],
        out_shape=[jax.ShapeDtypeStruct((G, S5_W, S5_W), BF16)] * 3
        + [jax.ShapeDtypeStruct((G, 1, S5_W), F32)],
        compiler_params=_cparams(1),
        name="s5_operators",
    )(*lam, ldt, *bt, *ct)


def _phase_tables(u, va, vb, ncb, n):
    def trig(m):
        m = m & (2 * n - 1)
        m = jnp.where(m >= n, m - 2 * n, m)
        ang = m.astype(F32) * (math.pi / n)
        return jnp.cos(ang), jnp.sin(ang)

    lane = lax.broadcasted_iota(jnp.int32, (1, LANES), 1)
    cb, sb = trig(u * (va * lane + vb))
    ca_all, sa_all = trig(u * (va * LANES * lane))
    out = []
    for ch in range(ncb):
        ca = ca_all[:, ch:ch + 1]
        sa = sa_all[:, ch:ch + 1]
        out.append((ca * cb - sa * sb, sa * cb + ca * sb))
    return out


def _tables_kernel(fs_ref, fst_ref, blk_ref, *, Ls):
    u = 2 * lax.broadcasted_iota(jnp.int32, (Ls, 1), 0) + 1
    for ch, (c, s) in enumerate(_phase_tables(u, 1, 0, Ls // LANES, 2 * Ls)):
        blk_ref[:Ls, ch * LANES:(ch + 1) * LANES] = c
        blk_ref[Ls:, ch * LANES:(ch + 1) * LANES] = -s
    blk = blk_ref[...]
    fs_ref[...] = blk.astype(BF16)
    fst_ref[...] = blk.T.astype(BF16)


def _dft_tables(Ls):
    return pl.pallas_call(
        functools.partial(_tables_kernel, Ls=Ls),
        out_shape=[jax.ShapeDtypeStruct((2 * Ls, Ls), BF16), jax.ShapeDtypeStruct((Ls, 2 * Ls), BF16)],
        scratch_shapes=[pltpu.VMEM((2 * Ls, Ls), F32)],
        compiler_params=pltpu.CompilerParams(vmem_limit_bytes=VMEM_LIMIT),
        name="dft_tables",
    )()


def _spectrum_kernel(fs_ref, f_ref, w_ref, *, S, Ls):
    scale = 1.0 / Ls
    k = lax.broadcasted_iota(jnp.int32, (Ls, 1), 0)
    sgn = jnp.where(k % 2 == 0, 1.0, -1.0)
    prev = None
    for i in range(S + 1):
        re = im = None
        if i < S:
            g = jnp.dot(fs_ref[...], f_ref[i * Ls:(i + 1) * Ls, :], preferred_element_type=F32) * scale
            re, im = g[:Ls], g[Ls:]
        cur = (re, im)
        if prev is not None:
            jre, jim = -sgn * prev[1], sgn * prev[0]
            re = jre if re is None else re + jre
            im = jim if im is None else im + jim
        w_ref[0, i, :Ls] = re.astype(w_ref.dtype)
        w_ref[0, i, Ls:] = im.astype(w_ref.dtype)
        prev = cur


def _filter_spectrum(fs, filt, S, C, dtype):
    n2, Ls = fs.shape
    L = filt.shape[0]
    nct = W_B // C
    return pl.pallas_call(
        functools.partial(_spectrum_kernel, S=S, Ls=Ls),
        grid=(2, nct),
        in_specs=[pl.BlockSpec((n2, Ls), lambda o, ct: (0, 0)),
                  pl.BlockSpec((L, C), lambda o, ct: (0, o * nct + ct))],
        out_specs=pl.BlockSpec((1, S + 1, n2, C), lambda o, ct: (o, 0, 0, ct)),
        out_shape=jax.ShapeDtypeStruct((2, S + 1, n2, W_B), dtype),
        compiler_params=_cparams(2),
        name="filter_spectrum",
    )(fs, filt)


def _hyena_filters(L, w1, b1, w2, b2, freq, w3):
    t = jnp.arange(L, dtype=F32)
    tn = t / (L - 1)
    f = jnp.linspace(1e-4, HY_BANDS - 1, HY_BANDS, dtype=F32)
    ang = (2.0 * math.pi / L) * t[:, None] * f[None, :]
    z = jnp.concatenate([tn[:, None], jnp.cos(ang), -jnp.sin(ang)], axis=-1)
    h = jnp.sin(freq * (jnp.dot(z, w1, precision=HIGHEST) + b1))
    h = jnp.sin(freq * (jnp.dot(h, w2, precision=HIGHEST) + b2))
    h = jnp.dot(h, w3, precision=HIGHEST).reshape(L, 2, W_B)
    max_decay = math.log(HY_TARGET) / HY_FAST_PCT
    min_decay = math.log(HY_TARGET) / HY_SLOW_PCT
    deltas = jnp.abs(jnp.linspace(min_decay, max_decay, W_B, dtype=F32))
    half = L // 2
    off = jnp.abs(t - half) / half
    win = jnp.exp(-off[:, None] * deltas[None, :]) + HY_SHIFT
    return h * win[:, None, :]


HY_RF = 64
HY_R = 128
SUBLANES = 8


def _short_conv_rows(u_ref, w, bias, c, R, L):
    r0 = pl.multiple_of(c * R, R)
    cur = u_ref[pl.ds(r0, R), :]
    before = u_ref[pl.ds(pl.multiple_of(jnp.maximum(r0 - SUBLANES, 0), SUBLANES), SUBLANES), :]
    after = u_ref[pl.ds(pl.multiple_of(jnp.minimum(r0 + R, L - SUBLANES), SUBLANES), SUBLANES), :]
    before = jnp.where(c == 0, 0.0, before[SUBLANES - 1:])
    after = jnp.where(c == L // R - 1, 0.0, after[:1])
    row = lax.broadcasted_iota(jnp.int32, (SUBLANES, 1), 0)
    up = pltpu.roll(cur, 1, 0)
    up = jnp.concatenate([jnp.where(row == 0, before, up[:SUBLANES]), up[SUBLANES:]], axis=0)
    dn = pltpu.roll(cur, R - 1, 0)
    dn = jnp.concatenate([dn[:R - SUBLANES], jnp.where(row == SUBLANES - 1, after, dn[R - SUBLANES:])], axis=0)
    return bias + up * w[0:1] + cur * w[1:2] + dn * w[2:3]


def _hy_kernel(v_ref, x_ref, cwv_ref, cbv_ref, cwx_ref, cbx_ref, hb_ref, fs_ref, fst_ref, w_ref, o_ref,
               zf_ref, zb_ref, u_ref, d_ref, os_ref, *, L, S, R):
    Ls = L // S
    C = zf_ref.shape[1]
    o = pl.program_id(2)
    seg = lambda j: slice(j * C, (j + 1) * C)

    @pl.when(o == 0)
    def _():
        for j in range(S):
            def body(cc, _, j=j):
                c = j * (Ls // R) + cc
                z = _short_conv_rows(v_ref, cwv_ref[...], cbv_ref[...], c, R, L)
                zf_ref[pl.ds(pl.multiple_of(c * R, R), R), :] = z
                zb_ref[pl.ds(pl.multiple_of(cc * R, R), R), seg(j)] = z.astype(BF16)
                return 0
            lax.fori_loop(0, Ls // R, body, 0)

    u_ref[...] = jnp.dot(fs_ref[...], zb_ref[...], preferred_element_type=F32).astype(u_ref.dtype)

    def prod(fc, _):
        f0 = pl.multiple_of(fc * HY_RF, HY_RF)
        re_rows = pl.ds(f0, HY_RF)
        im_rows = pl.ds(Ls + f0, HY_RF)
        for r in range(S):
            dre = dim = None
            for j in range(S):
                d = r - j + S // 2
                if 0 <= d <= S:
                    ur, ui = u_ref[re_rows, seg(j)], u_ref[im_rows, seg(j)]
                    wr, wi = w_ref[0, d, re_rows, :], w_ref[0, d, im_rows, :]
                    pre, pim = ur * wr - ui * wi, ur * wi + ui * wr
                    dre = pre if dre is None else dre + pre
                    dim = pim if dim is None else dim + pim
            d_ref[re_rows, seg(r)] = dre.astype(BF16)
            d_ref[im_rows, seg(r)] = dim.astype(BF16)
        return 0
    lax.fori_loop(0, Ls // HY_RF, prod, 0)

    os_ref[...] = jnp.dot(fst_ref[...], d_ref[...], preferred_element_type=F32)

    for r in range(S):
        def gate(cc, _, r=r):
            c = r * (Ls // R) + cc
            rows = pl.ds(pl.multiple_of(c * R, R), R)
            local = pl.ds(pl.multiple_of(cc * R, R), R)
            x = _short_conv_rows(x_ref, cwx_ref[...], cbx_ref[...], c, R, L)
            z = x * (os_ref[local, seg(r)] + zf_ref[rows, :] * hb_ref[0])

            @pl.when(o == 0)
            def _():
                zf_ref[rows, :] = z
                zb_ref[local, seg(r)] = z.astype(BF16)

            @pl.when(o == 1)
            def _():
                o_ref[rows, :] = z
            return 0
        lax.fori_loop(0, Ls // R, gate, 0)


def _hyena_mix(ut, B, conv_w, conv_b, bias, fs, fst, w, C):
    L = ut.shape[0]
    n2, Ls = fs.shape
    S = L // Ls
    nct = W_B // C
    ncol = UT_COLS // C
    xcol = lambda o, ct: (1 + o) * nct + ct
    conv_b = conv_b.reshape(1, -1)
    R = min(HY_R, Ls)
    return pl.pallas_call(
        functools.partial(_hy_kernel, L=L, S=S, R=R),
        grid=(B, nct, 2),
        in_specs=[pl.BlockSpec((L, C), lambda b, ct, o: (0, b * ncol + ct)),
                  pl.BlockSpec((L, C), lambda b, ct, o: (0, b * ncol + xcol(o, ct))),
                  pl.BlockSpec((3, C), lambda b, ct, o: (0, ct)),
                  pl.BlockSpec((1, C), lambda b, ct, o: (0, ct)),
                  pl.BlockSpec((3, C), lambda b, ct, o: (0, xcol(o, ct))),
                  pl.BlockSpec((1, C), lambda b, ct, o: (0, xcol(o, ct))),
                  pl.BlockSpec((1, 1, C), lambda b, ct, o: (o, 0, ct)),
                  pl.BlockSpec((n2, Ls), lambda b, ct, o: (0, 0)),
                  pl.BlockSpec((Ls, n2), lambda b, ct, o: (0, 0)),
                  pl.BlockSpec((1, S + 1, n2, C), lambda b, ct, o: (o, 0, 0, ct))],
        out_specs=pl.BlockSpec((L, C), lambda b, ct, o: (0, b * nct + ct)),
        out_shape=jax.ShapeDtypeStruct((L, B * W_B), F32),
        scratch_shapes=[pltpu.VMEM((L, C), F32), pltpu.VMEM((Ls, S * C), BF16), pltpu.VMEM((n2, S * C), w.dtype),
                        pltpu.VMEM((n2, S * C), BF16), pltpu.VMEM((Ls, S * C), F32)],
        compiler_params=_cparams(3),
        name="hyena_mix",
    )(ut, ut, conv_w, conv_b, conv_w, conv_b, bias.reshape(2, 1, W_B), fs, fst, w)


POOL_PAD = 16


def _pool_kernel(u_ref, w_ref, sc_ref, o_ref, *, L):
    assert POOL_WINDOWS == (2, 4, 8, 16)
    n = L + 2 * POOL_PAD
    shifted = lambda x, s: pltpu.roll(x, (-s) % n, 0)
    for gi, win in enumerate(POOL_WINDOWS):
        u = u_ref[:, gi * POOL_G:(gi + 1) * POOL_G]
        pad = jnp.zeros((POOL_PAD, POOL_G), F32)
        ext = jnp.concatenate([pad, u, pad], axis=0)
        s = ext + shifted(ext, -1)
        half = 1
        while 2 * half < win:
            s = shifted(s, -half) + shifted(s, half)
            half *= 2
        s = s[POOL_PAD:POOL_PAD + L]

        def mean_minus_token(sl, t0):
            t = t0 + lax.broadcasted_iota(jnp.int32, (SUBLANES, 1), 0)
            cnt = jnp.minimum(t - win // 2 + win, L) - jnp.maximum(t - win // 2, 0)
            return s[sl] / cnt.astype(F32) - u[sl]

        edge = SUBLANES
        d = jnp.concatenate([mean_minus_token(slice(0, edge), 0),
                             s[edge:L - edge] * (1.0 / win) - u[edge:L - edge],
                             mean_minus_token(slice(L - edge, L), L - edge)], axis=0)
        o_ref[:, gi * POOL_G:(gi + 1) * POOL_G] = (
            jnp.dot(d.astype(BF16), w_ref[gi], preferred_element_type=F32)
            * sc_ref[:, gi * POOL_G:(gi + 1) * POOL_G])


def _pool_mix(ut, B, w, scale):
    L = ut.shape[0]
    nblk = UT_COLS // W_C
    return pl.pallas_call(
        functools.partial(_pool_kernel, L=L),
        grid=(B,),
        in_specs=[pl.BlockSpec((L, W_C), lambda b: (0, b * nblk + nblk - 1)),
                  pl.BlockSpec((len(POOL_WINDOWS), POOL_G, POOL_G), lambda b: (0, 0, 0)),
                  pl.BlockSpec((1, W_C), lambda b: (0, 0))],
        out_specs=pl.BlockSpec((L, W_C), lambda b: (0, b)),
        out_shape=jax.ShapeDtypeStruct((L, B * W_C), F32),
        compiler_params=_cparams(1),
        name="pool_mix",
    )(ut, w, scale.reshape(1, W_C))


def _silu(x):
    return x * jax.nn.sigmoid(x)


def _gelu_tanh(x):
    return x * (0.5 * (1.0 + jnp.tanh(math.sqrt(2.0 / math.pi) * (x + 0.044715 * (x * x * x)))))


def _out_kernel(*refs, has_pos, batch_major, final, B, bm):
    refs = list(refs)
    x_ref = refs.pop(0)
    pos_ref = refs.pop(0) if has_pos else None
    (mod_ref, h_ref, ua_ref, ya_ref, zb_ref, yc_ref, wga_ref, wgb_ref, wgc_ref, wma_ref, wmb_ref, wmc_ref,
     d_ref, wglu_ref, bglu_ref, wa_ref, wb_ref, wc_ref, wo_ref) = refs[:19]
    refs = refs[19:]
    fg_ref = refs.pop(0) if final else None
    xo_ref, zs_ref, ys_ref = refs[:3]
    xs_ref = refs[3] if (batch_major or final) else None
    for b in range(B):
        _put_batch(zs_ref, b, B, zb_ref[:, b * W_B:(b + 1) * W_B])
        _put_batch(ys_ref, b, B, yc_ref[:, b * W_C:(b + 1) * W_C])
    mod = mod_ref[...]
    h = h_ref[...]
    proj = lambda w_ref: jnp.dot(h, w_ref[0], preferred_element_type=F32)
    y = _gelu_tanh(ya_ref[...] + ua_ref[...] * d_ref[0])
    y = y * jax.nn.sigmoid(jnp.dot(y.astype(BF16), wglu_ref[0], preferred_element_type=F32) + bglu_ref[0])
    y_a = (y * _silu(proj(wga_ref))).astype(BF16)
    y_b = (_get_tile(zs_ref) * _silu(proj(wgb_ref))).astype(BF16)
    y_c = (_get_tile(ys_ref) * _silu(proj(wgc_ref))).astype(BF16)
    merged = (jax.nn.sigmoid(proj(wma_ref)) * jnp.dot(y_a, wa_ref[0], preferred_element_type=F32)
              + jax.nn.sigmoid(proj(wmb_ref)) * jnp.dot(y_b, wb_ref[0], preferred_element_type=F32)
              + jax.nn.sigmoid(proj(wmc_ref)) * jnp.dot(y_c, wc_ref[0], preferred_element_type=F32))
    out = jnp.dot(merged.astype(BF16), wo_ref[0], preferred_element_type=F32)
    x = _load_x(x_ref, xs_ref if batch_major else None, pos_ref, B, bm)
    x_new = x + _mod_part(mod, 2, bm) * _by_batch(out, bm)
    if final:
        r = lax.rsqrt(jnp.mean(x_new * x_new, axis=-1, keepdims=True) + EPS)
        _put_tile(xs_ref, (x_new * r * fg_ref[...]).reshape(-1, D_MODEL))
        for b in range(B):
            xo_ref[b] = _get_batch(xs_ref, b, B)
    else:
        xo_ref[...] = x_new.reshape(-1, D_MODEL)


def _out_proj(x, pos, mod, mod_blk, B, bm, h, ua, ya, zb, yc, w_in, l, d, w_glu, b_glu,
              w_a, w_b, w_c, w_o, final_g, tm):
    D = D_MODEL
    rows = x.size // D
    L = rows // B
    batch_major = x.ndim == 3
    final = final_g is not None
    tok = lambda w: pl.BlockSpec((tm, w), lambda i: (i, 0))
    seq = lambda w: pl.BlockSpec((tm // B, B * w), lambda i: (i, 0))
    full = lambda a: _const_spec(a.shape, (0,) * a.ndim)
    in_specs, args = _x_specs(x, pos, mod, mod_blk, B, bm, tm)
    in_specs += [tok(D), tok(W_A), tok(W_A), seq(W_B), seq(W_C)]
    args += [h, ua, ya, zb, yc]
    for cb in G_BLOCKS:
        in_specs.append(_const_spec((1, D, CB), (l, 0, cb)))
        args.append(w_in)
    for k in range(3):
        in_specs.append(_const_spec((1, D, D), (l, 0, M_BLOCK0 + k)))
        args.append(w_in)
    weights = [d, w_glu, b_glu, w_a, w_b, w_c, w_o]
    in_specs += [_const_spec((1,) + a.shape[1:], (l, 0, 0)) for a in weights]
    args += weights
    if final:
        in_specs.append(full(final_g))
        args.append(final_g)
    scratch = [_stage(tm, W_B), _stage(tm, W_C)]
    if batch_major or final:
        scratch.append(_stage(tm, D))
    if final:
        out_spec = pl.BlockSpec((B, tm // B, D), lambda i: (0, i, 0))
        out_shape = jax.ShapeDtypeStruct((B, L, D), F32)
    else:
        out_spec = tok(D)
        out_shape = jax.ShapeDtypeStruct((rows, D), F32)
    return pl.pallas_call(
        functools.partial(_out_kernel, has_pos=pos is not None, batch_major=batch_major, final=final,
                          B=B, bm=bm),
        grid=(rows // tm,),
        in_specs=in_specs,
        out_specs=out_spec,
        out_shape=out_shape,
        scratch_shapes=scratch,
        compiler_params=_cparams(1),
        name="out_proj",
    )(*args)


def _grid_pos_embed(L):
    rows = L // GRID_W
    r = jnp.broadcast_to(jnp.arange(rows, dtype=F32)[:, None], (rows, GRID_W)).reshape(-1)
    col = jnp.broadcast_to(jnp.arange(GRID_W, dtype=F32)[None, :], (rows, GRID_W)).reshape(-1)
    q = D_MODEL // 4
    omega = 1.0 / (10000.0 ** (jnp.arange(q, dtype=F32) / q))
    ar = r[:, None] * omega[None, :]
    ac = col[:, None] * omega[None, :]
    return jnp.concatenate([jnp.sin(ar), jnp.cos(ar), jnp.sin(ac), jnp.cos(ac)], axis=-1)


def _states_to_lanes(st_re, st_im):
    s = jnp.concatenate([st_re[:, 0], st_re[:, 1], st_im[:, 0], st_im[:, 1]], axis=-1)
    return s.transpose(1, 0, 2)


def _lanes_to_states(fin):
    f = fin.transpose(1, 0, 2).reshape(fin.shape[1], S5_G, 4, S5_P)
    return (jnp.stack([f[:, :, 0], f[:, :, 1]], axis=1), jnp.stack([f[:, :, 2], f[:, :, 3]], axis=1))


def kernel(x_prompt, x_sample, c, state_s5_re, state_s5_im, c_ctx, norm_g, w_mod, b_mod, w_in, s5_lam_re, s5_lam_im, s5_log_dt, s5_b_re, s5_b_im, s5_c_re, s5_c_im, s5_d, s5_w_glu, s5_b_glu, hy_conv_w, hy_conv_b, hy_f_w1, hy_f_b1, hy_f_w2, hy_f_b2, hy_f_freq, hy_f_w3, hy_bias, pool_w, pool_scale, w_br_a, w_br_b, w_br_c, w_out, final_g):
    Bc, Lc, D = x_prompt.shape
    Bl, Ll, _ = x_sample.shape
    assert Bl == MOD_ROWS - MOD_LAT0

    cond = jnp.zeros((MOD_ROWS, D), F32).at[0].set(c_ctx).at[MOD_LAT0:].set(c)
    mod = _modulation(cond, w_mod, b_mod).reshape(DEPTH * MOD_ROWS, 3 * D)

    pos = _grid_pos_embed(Ll)
    groups = {
        'ctx': dict(B=Bc, L=Lc, S=2, C=W_B, hy_dtype=F32, bm=1, mod_blk=0),
        'lat': dict(B=Bl, L=Ll, S=4, C=W_B // 2, hy_dtype=BF16, bm=Bl, mod_blk=1),
    }
    tables = {k: _dft_tables(v['L'] // v['S']) for k, v in groups.items()}
    w_in_b = w_in.astype(BF16)
    out_weights = [s5_d.reshape(DEPTH, 1, W_A), s5_w_glu.astype(BF16), s5_b_glu.reshape(DEPTH, 1, W_A),
                   w_br_a.astype(BF16), w_br_b.astype(BF16), w_br_c.astype(BF16), w_out.astype(BF16)]
    xs = {'ctx': x_prompt, 'lat': x_sample}
    new_re, new_im = [], []
    for l in range(DEPTH):
        s5_ops = _s5_operators(s5_lam_re[l], s5_lam_im[l], s5_log_dt[l], s5_b_re[l], s5_b_im[l],
                               s5_c_re[l], s5_c_im[l])
        g = norm_g[l].reshape(1, D)
        for name, cfg in groups.items():
            B, L, bm = cfg['B'], cfg['L'], cfg['bm']
            x = xs[name]
            p = pos if (name == 'lat' and l == 0) else None
            mod_blk = l * (MOD_ROWS // MOD_LAT0) + cfg['mod_blk']
            ua, ut, hn = _in_proj(x, p, mod, mod_blk, B, bm, g, w_in_b, l, min(TM_IN, L * B))
            if name == 'ctx':
                h0g = jnp.zeros((S5_G, B, 4 * S5_P), F32)
            else:
                h0g = _states_to_lanes(state_s5_re[:, l], state_s5_im[:, l])
            ya, fin = _s5_mix(ua, s5_ops, h0g, B)
            if name == 'ctx':
                fr, fi = _lanes_to_states(fin)
                new_re.append(fr)
                new_im.append(fi)
            fs, fst = tables[name]
            filt = _hyena_filters(L, hy_f_w1[l], hy_f_b1[l], hy_f_w2[l], hy_f_b2[l], hy_f_freq[l], hy_f_w3[l])
            hw = _filter_spectrum(fs, filt.reshape(L, 2 * W_B).astype(BF16), cfg['S'], cfg['C'], cfg['hy_dtype'])
            zb = _hyena_mix(ut, B, hy_conv_w[l], hy_conv_b[l], hy_bias[l], fs, fst, hw, cfg['C'])
            yc = _pool_mix(ut, B, pool_w[l].astype(BF16), pool_scale[l])
            fg = final_g.reshape(1, D) if l == DEPTH - 1 else None
            xs[name] = _out_proj(x, p, mod, mod_blk, B, bm, hn, ua, ya, zb, yc, w_in_b, l,
                                 *out_weights, fg, min(TM_OUT, L * B))
    return (xs['ctx'], xs['lat'], jnp.stack(new_re, axis=1), jnp.stack(new_im, axis=1))
```

```python
import functools
import math

import jax
import jax.numpy as jnp
from jax import lax
from jax.experimental import pallas as pl
from jax.experimental.pallas import tpu as pltpu

F32 = jnp.float32
BF16 = jnp.bfloat16
HIGHEST = lax.Precision.HIGHEST

D_MODEL = 1024
DEPTH = 2
GRID_W = 64
EPS = 1e-6
W_A = D_MODEL // 2
S5_H = 16
S5_G = W_A // S5_H
S5_P = 64
W_B = D_MODEL // 2
HY_BANDS = 16
HY_FAST_PCT = 0.3
HY_SLOW_PCT = 1.5
HY_TARGET = 1e-2
HY_SHIFT = 0.05
W_C = D_MODEL // 2
POOL_WINDOWS = (2, 4, 8, 16)
POOL_G = W_C // 4
LANES = 128
CB = 512
U_BLOCKS = (0, 2, 3, 4, 6)
G_BLOCKS = (1, 5, 7)
M_BLOCK0 = 4
UT_COLS = CB * (len(U_BLOCKS) - 1)
S5_CHUNK = 16
S5_W = S5_CHUNK * S5_H
S5_GB = LANES // S5_H
MOD_ROWS = 16
MOD_LAT0 = 8
VMEM_LIMIT = 56 * 1024 * 1024
TM_IN = 1024
TM_OUT = 256


def _const_spec(block_shape, index):
    return pl.BlockSpec(block_shape, lambda *_: index, pipeline_mode=pl.Buffered(1))


def _cparams(n_grid):
    return pltpu.CompilerParams(dimension_semantics=("arbitrary",) * n_grid,
                                vmem_limit_bytes=VMEM_LIMIT)


def _mod_kernel(c_ref, w_ref, b_ref, o_ref):
    c = c_ref[...]
    s = c * jax.nn.sigmoid(c)
    o_ref[0] = jnp.dot(s, w_ref[0], preferred_element_type=F32, precision=HIGHEST) + b_ref[0]


def _modulation(cond, w_mod, b_mod, tn=3 * D_MODEL // 2):
    n = 3 * D_MODEL
    return pl.pallas_call(
        _mod_kernel,
        grid=(DEPTH, n // tn),
        in_specs=[pl.BlockSpec((MOD_ROWS, D_MODEL), lambda l, j: (0, 0)),
                  pl.BlockSpec((1, D_MODEL, tn), lambda l, j: (l, 0, j)),
                  pl.BlockSpec((1, 1, tn), lambda l, j: (l, 0, j))],
        out_specs=pl.BlockSpec((1, MOD_ROWS, tn), lambda l, j: (l, 0, j)),
        out_shape=jax.ShapeDtypeStruct((DEPTH, MOD_ROWS, n), F32),
        compiler_params=_cparams(2),
        name="modulation",
    )(cond, w_mod, b_mod.reshape(DEPTH, 1, n))


def _by_batch(x, bm):
    return x if bm == 1 else x.reshape(x.shape[0] // bm, bm, x.shape[1])


def _mod_part(mod, i, bm):
    m = mod[:bm, i * D_MODEL:(i + 1) * D_MODEL]
    return m if bm == 1 else m[None]


def _stage(rows, w):
    return pltpu.VMEM((w // LANES, rows, LANES), F32)


def _put_batch(ref, b, B, val):
    for c in range(ref.shape[0]):
        ref[c, pl.ds(b, val.shape[0], stride=B), :] = val[:, c * LANES:(c + 1) * LANES]


def _get_batch(ref, b, B):
    tt = ref.shape[1] // B
    return jnp.concatenate([ref[c, pl.ds(b, tt, stride=B), :] for c in range(ref.shape[0])], axis=1)


def _put_tile(ref, val):
    for c in range(ref.shape[0]):
        ref[c] = val[:, c * LANES:(c + 1) * LANES]


def _get_tile(ref):
    return jnp.concatenate([ref[c] for c in range(ref.shape[0])], axis=1)


def _load_x(x_ref, xs_ref, pos_ref, B, bm):
    if xs_ref is not None:
        for b in range(B):
            _put_batch(xs_ref, b, B, x_ref[b])
        x2 = _get_tile(xs_ref)
    else:
        x2 = x_ref[...]
    x = _by_batch(x2, bm)
    if pos_ref is not None:
        x = x + pos_ref[...][:, None, :]
    return x


def _normed(x, mod, g, bm):
    r = lax.rsqrt(jnp.mean(x * x, axis=-1, keepdims=True) + EPS)
    h = (x * r * g) * (1.0 + _mod_part(mod, 1, bm)) + _mod_part(mod, 0, bm)
    return h.reshape(-1, D_MODEL).astype(BF16)


def _in_kernel(*refs, has_pos, batch_major, B, bm):
    refs = list(refs)
    x_ref = refs.pop(0)
    pos_ref = refs.pop(0) if has_pos else None
    mod_ref, g_ref = refs[:2]
    w_refs = refs[2:2 + len(U_BLOCKS)]
    oa_ref, ot_ref, h_ref, scr_ref = refs[2 + len(U_BLOCKS):6 + len(U_BLOCKS)]
    xs_ref = refs[-1] if batch_major else None
    h = _normed(_load_x(x_ref, xs_ref, pos_ref, B, bm), mod_ref[...], g_ref[...], bm)
    h_ref[...] = h
    oa_ref[...] = jnp.dot(h, w_refs[0][0], preferred_element_type=F32)
    for i, w_ref in enumerate(w_refs[1:]):
        _put_tile(scr_ref, jnp.dot(h, w_ref[0], preferred_element_type=F32))
        for b in range(B):
            ot_ref[:, b * UT_COLS + i * CB:b * UT_COLS + (i + 1) * CB] = _get_batch(scr_ref, b, B)


def _x_specs(x, pos, mod, mod_blk, B, bm, tm):
    D = D_MODEL
    if x.ndim == 3:
        specs = [pl.BlockSpec((B, tm // B, D), lambda i: (0, i, 0))]
    else:
        specs = [pl.BlockSpec((tm, D), lambda i: (i, 0))]
    args = [x]
    if pos is not None:
        specs.append(pl.BlockSpec((tm // bm, D), lambda i: (i, 0)))
        args.append(pos)
    specs.append(_const_spec((MOD_LAT0, 3 * D), (mod_blk, 0)))
    args.append(mod)
    return specs, args


def _in_proj(x, pos, mod, mod_blk, B, bm, g, w_in, l, tm):
    D = D_MODEL
    rows = x.size // D
    L = rows // B
    batch_major = x.ndim == 3
    in_specs, args = _x_specs(x, pos, mod, mod_blk, B, bm, tm)
    in_specs.append(_const_spec((1, D), (0, 0)))
    args.append(g)
    for cb in U_BLOCKS:
        in_specs.append(_const_spec((1, D, CB), (l, 0, cb)))
        args.append(w_in)
    scratch = [_stage(tm, CB)] + ([_stage(tm, D)] if batch_major else [])
    return pl.pallas_call(
        functools.partial(_in_kernel, has_pos=pos is not None, batch_major=batch_major, B=B, bm=bm),
        grid=(rows // tm,),
        in_specs=in_specs,
        out_specs=[pl.BlockSpec((tm, W_A), lambda i: (i, 0)),
                   pl.BlockSpec((tm // B, B * UT_COLS), lambda i: (i, 0)),
                   pl.BlockSpec((tm, D), lambda i: (i, 0))],
        out_shape=[jax.ShapeDtypeStruct((rows, W_A), F32),
                   jax.ShapeDtypeStruct((L, B * UT_COLS), F32),
                   jax.ShapeDtypeStruct((rows, D), BF16)],
        scratch_shapes=scratch,
        compiler_params=_cparams(1),
        name="in_proj",
    )(*args)


S5_RB = 128
S5_GS = 4


def _s5_kernel(u_ref, m_ref, p_ref, qt_ref, a_ref, h0_ref, y_ref, fin_ref,
               x_ref, yall_ref, sloc_ref, sinf_ref, sinb_ref, *, nc, B):
    rows = nc * B
    cpb = S5_RB // B
    half = S5_W // 2
    lane_grp = lax.broadcasted_iota(jnp.int32, (S5_RB, LANES), 1) // S5_H

    lane = lax.broadcasted_iota(jnp.int32, (B, half), 1)
    is_fwd = lane < S5_P
    lane_full = lax.broadcasted_iota(jnp.int32, (rows, S5_W), 1)
    is_fwd_full = (lane_full % half) < S5_P

    def gather_step(rb, _, g0):
        c0 = pl.multiple_of(rb * cpb, cpb)
        r0 = pl.multiple_of(rb * S5_RB, S5_RB)
        slabs = [u_ref[pl.ds(c0, cpb), t * B:(t + 1) * B, :].reshape(S5_RB, LANES) for t in range(S5_CHUNK)]
        rolled = [s if t % S5_GB == 0 else pltpu.roll(s, (t % S5_GB) * S5_H, 1) for t, s in enumerate(slabs)]
        for gi in range(S5_GS):
            g8 = g0 + gi
            for hf in range(2):
                acc = rolled[hf * S5_GB]
                for k in range(1, S5_GB):
                    acc = jnp.where(lane_grp == (k + g8) % S5_GB, rolled[hf * S5_GB + k], acc)
                x_ref[gi, pl.ds(r0, S5_RB), hf * LANES:(hf + 1) * LANES] = acc.astype(BF16)
        return 0

    def group_step(pi, _, g0):
        gis = (2 * pi, 2 * pi + 1)
        ars, ais, init = [], [], []
        for k, gi in enumerate(gis):
            sloc_ref[k] = jnp.dot(x_ref[gi], p_ref[g0 + gi], preferred_element_type=F32)
            a = a_ref[g0 + gi]
            ars.append(a[:, :half])
            ais.append(a[:, half:])
            h0 = h0_ref[g0 + gi]
            init += [h0[:, :half], h0[:, half:]]

        def body(i, carry):
            jf = pl.multiple_of(i * B, B)
            jb = pl.multiple_of((nc - 1 - i) * B, B)
            out = []
            for k in range(2):
                cr, ci = carry[2 * k], carry[2 * k + 1]
                sinf_ref[k, pl.ds(jf, B), :half] = cr
                sinf_ref[k, pl.ds(jf, B), half:] = ci
                sinb_ref[k, pl.ds(jb, B), :half] = cr
                sinb_ref[k, pl.ds(jb, B), half:] = ci
                lf = sloc_ref[k, pl.ds(jf, B), :]
                lb = sloc_ref[k, pl.ds(jb, B), :]
                lr = jnp.where(is_fwd, lf[:, :half], lb[:, :half])
                li = jnp.where(is_fwd, lf[:, half:], lb[:, half:])
                out += [ars[k] * cr - ais[k] * ci + lr, ars[k] * ci + ais[k] * cr + li]
            return tuple(out)

        fin = lax.fori_loop(0, nc, body, tuple(init))
        for k, gi in enumerate(gis):
            g8 = g0 + gi
            fin_ref[g8, :, :half] = fin[2 * k]
            fin_ref[g8, :, half:] = fin[2 * k + 1]
            s_in = jnp.where(is_fwd_full, sinf_ref[k], sinb_ref[k]).astype(BF16)
            yall_ref[gi] = (jnp.dot(x_ref[gi], m_ref[g8], preferred_element_type=F32)
                            + lax.dot_general(s_in, qt_ref[g8], (((1,), (1,)), ((), ())),
                                              preferred_element_type=F32))
        return 0

    def scatter_step(rb, _, g0):
        c0 = pl.multiple_of(rb * cpb, cpb)
        r0 = pl.multiple_of(rb * S5_RB, S5_RB)
        for hf in range(2):
            ys = [yall_ref[gi, pl.ds(r0, S5_RB), hf * LANES:(hf + 1) * LANES] for gi in range(S5_GS)]
            for k in range(S5_GB):
                acc = ys[0]
                for gi in range(1, S5_GS):
                    acc = jnp.where(lane_grp == (k + g0 + gi) % S5_GB, ys[gi], acc)
                if k:
                    acc = pltpu.roll(acc, (S5_GB - k) * S5_H, 1)
                t = hf * S5_GB + k
                dst = (pl.ds(c0, cpb), slice(t * B, (t + 1) * B), slice(None))
                if g0 > 0:
                    acc = jnp.where(lane_grp >= g0, acc, y_ref[dst].reshape(S5_RB, LANES))
                y_ref[dst] = acc.reshape(cpb, B, LANES)
        return 0

    for g0 in range(0, S5_GB, S5_GS):
        lax.fori_loop(0, rows // S5_RB, functools.partial(gather_step, g0=g0), 0)
        lax.fori_loop(0, S5_GS // 2, functools.partial(group_step, g0=g0), 0)
        lax.fori_loop(0, rows // S5_RB, functools.partial(scatter_step, g0=g0), 0)


def _s5_mix(ua, ops, h0, B):
    m, p, q, a16 = ops
    rows_all = ua.shape[0]
    nc = rows_all // (S5_CHUNK * B)
    rows = nc * B
    up3 = ua.reshape(nc, S5_CHUNK * B, W_A)
    gblk = lambda r: pl.BlockSpec((S5_GB, r, S5_W), lambda j: (j, 0, 0))
    tok = pl.BlockSpec((nc, S5_CHUNK * B, LANES), lambda j: (0, 0, j))
    y, fin = pl.pallas_call(
        functools.partial(_s5_kernel, nc=nc, B=B),
        grid=(W_A // LANES,),
        in_specs=[tok, gblk(S5_W), gblk(S5_W), gblk(S5_W), gblk(1), gblk(B)],
        out_specs=[tok, gblk(B)],
        out_shape=[jax.ShapeDtypeStruct((nc, S5_CHUNK * B, W_A), F32),
                   jax.ShapeDtypeStruct((S5_G, B, S5_W), F32)],
        scratch_shapes=[pltpu.VMEM((S5_GS, rows, S5_W), BF16), pltpu.VMEM((S5_GS, rows, S5_W), F32),
                        pltpu.VMEM((2, rows, S5_W), F32), pltpu.VMEM((2, rows, S5_W), F32),
                        pltpu.VMEM((2, rows, S5_W), F32)],
        compiler_params=_cparams(1),
        name="s5_mix",
    )(up3, m, p, q, a16, h0)
    return y.reshape(rows_all, W_A), fin


def _s5_ops_kernel(lr_ref, li_ref, ldt_ref, br_ref, bi_ref, cr_ref, ci_ref, m_ref, p_ref, qt_ref, a_ref):
    T = S5_CHUNK
    H = S5_H
    lam_re = lr_ref[0]
    lam_im = li_ref[0]
    dt = jnp.exp(ldt_ref[0])
    mag = jnp.exp(lam_re * dt)
    ang = lam_im * dt
    a_re = mag * jnp.cos(ang)
    a_im = mag * jnp.sin(ang)
    n_re = a_re - 1.0
    n_im = a_im
    den = lam_re * lam_re + lam_im * lam_im
    k_re = (n_re * lam_re + n_im * lam_im) / den
    k_im = (n_im * lam_re - n_re * lam_im) / den
    b_re = br_ref[0]
    b_im = bi_ref[0]
    bb_re = k_re * b_re - k_im * b_im
    bb_im = k_re * b_im + k_im * b_re
    c_re = cr_ref[0]
    c_im = ci_ref[0]
    ap = [(jnp.ones_like(a_re), jnp.zeros_like(a_re))]
    for _ in range(T):
        pr, pi = ap[-1]
        ap.append((pr * a_re - pi * a_im, pr * a_im + pi * a_re))
    is_fwd = lax.broadcasted_iota(jnp.int32, (1, LANES), 1) < S5_P

    def powers(kf, kb):
        return jnp.where(is_fwd, ap[kf][0], ap[kb][0]), jnp.where(is_fwd, ap[kf][1], ap[kb][1])

    g8 = pl.program_id(0) % S5_GB

    def pos_rows(t):
        p = (t // S5_GB) * S5_GB + (t % S5_GB + g8) % S5_GB
        return pl.ds(pl.multiple_of(p * H, H), H)

    for t in range(T):
        rows = pos_rows(t)
        er, ei = powers(T - 1 - t, t)
        p_ref[0, rows, :LANES] = (er * bb_re - ei * bb_im).astype(BF16)
        p_ref[0, rows, LANES:] = (er * bb_im + ei * bb_re).astype(BF16)
        er, ei = powers(t + 1, T - t)
        qt_ref[0, rows, :LANES] = (c_re * er - c_im * ei).astype(BF16)
        qt_ref[0, rows, LANES:] = (-(c_re * ei + c_im * er)).astype(BF16)
    a_ref[0, :, :LANES] = ap[T][0]
    a_ref[0, :, LANES:] = ap[T][1]

    def ca(k):
        return c_re * ap[k][0] - c_im * ap[k][1], c_re * ap[k][1] + c_im * ap[k][0]

    fwd = [ca(k) for k in range(T)]
    bwd = fwd[::-1]
    nt = lambda a, b: lax.dot_general(a, b, (((1,), (1,)), ((), ())), precision=HIGHEST,
                                      preferred_element_type=F32)
    cat = lambda parts, i: jnp.concatenate([p[i] for p in parts], axis=0)
    zero = jnp.zeros_like(bb_re)
    wf = (nt(jnp.where(is_fwd, bb_re, zero), cat(fwd, 0)) - nt(jnp.where(is_fwd, bb_im, zero), cat(fwd, 1)))
    wb = (nt(jnp.where(is_fwd, zero, bb_re), cat(bwd, 0)) - nt(jnp.where(is_fwd, zero, bb_im), cat(bwd, 1)))
    pad = jnp.zeros((H, S5_W), F32)
    wf_pad = jnp.concatenate([pad, wf], axis=1)
    wb_pad = jnp.concatenate([wb, pad], axis=1)
    for t in range(T):
        row = (wf_pad[:, S5_W - H * t:2 * S5_W - H * t] + wb_pad[:, (T - 1 - t) * H:(T - 1 - t) * H + S5_W])
        row = jnp.concatenate([pltpu.roll(row[:, :LANES], g8 * H, 1), pltpu.roll(row[:, LANES:], g8 * H, 1)],
                              axis=1)
        m_ref[0, pos_rows(t), :] = row.astype(BF16)


def _s5_operators(lam_re, lam_im, log_dt, b_re, b_im, c_re, c_im):
    G = S5_G
    dirs = lambda x: jnp.concatenate([x[0], x[1]], axis=-1)
    lam = [dirs(x)[:, None, :] for x in (lam_re, lam_im)]
    ldt = dirs(jnp.broadcast_to(log_dt[..., None], (2, G, S5_P)))[:, None, :]
    bt = [dirs(x.transpose(0, 1, 3, 2)) for x in (b_re, b_im)]
    ct = [dirs(x) for x in (c_re, c_im)]
    vec = pl.BlockSpec((1, 1, LANES), lambda g: (g, 0, 0))
    mat = pl.BlockSpec((1, S5_H, LANES), lambda g: (g, 0, 0))
    op = pl.BlockSpec((1, S5_W, S5_W), lambda g: (g, 0, 0))
    return pl.pallas_call(
        _s5_ops_kernel,
        grid=(G,),
        in_specs=[vec, vec, vec, mat, mat, mat, mat],
        out_specs=[op, op, op, pl.BlockSpec((1, 1, S5_W), lambda g: (g, 0, 0))],
        out_shape=[jax.ShapeDtypeStruct((G, S5_W, S5_W), BF16)] * 3
        + [jax.ShapeDtypeStruct((G, 1, S5_W), F32)],
        compiler_params=_cparams(1),
        name="s5_operators",
    )(*lam, ldt, *bt, *ct)


def _phase_tables(u, va, vb, ncb, n):
    def trig(m):
        m = m & (2 * n - 1)
        m = jnp.where(m >= n, m - 2 * n, m)
        ang = m.astype(F32) * (math.pi / n)
        return jnp.cos(ang), jnp.sin(ang)

    lane = lax.broadcasted_iota(jnp.int32, (1, LANES), 1)
    cb, sb = trig(u * (va * lane + vb))
    ca_all, sa_all = trig(u * (va * LANES * lane))
    out = []
    for ch in range(ncb):
        ca = ca_all[:, ch:ch + 1]
        sa = sa_all[:, ch:ch + 1]
        out.append((ca * cb - sa * sb, sa * cb + ca * sb))
    return out


def _tables_kernel(fs_ref, fst_ref, blk_ref, *, Ls):
    u = 2 * lax.broadcasted_iota(jnp.int32, (Ls, 1), 0) + 1
    for ch, (c, s) in enumerate(_phase_tables(u, 1, 0, Ls // LANES, 2 * Ls)):
        blk_ref[:Ls, ch * LANES:(ch + 1) * LANES] = c
        blk_ref[Ls:, ch * LANES:(ch + 1) * LANES] = -s
    blk = blk_ref[...]
    fs_ref[...] = blk.astype(BF16)
    fst_ref[...] = blk.T.astype(BF16)


def _dft_tables(Ls):
    return pl.pallas_call(
        functools.partial(_tables_kernel, Ls=Ls),
        out_shape=[jax.ShapeDtypeStruct((2 * Ls, Ls), BF16), jax.ShapeDtypeStruct((Ls, 2 * Ls), BF16)],
        scratch_shapes=[pltpu.VMEM((2 * Ls, Ls), F32)],
        compiler_params=pltpu.CompilerParams(vmem_limit_bytes=VMEM_LIMIT),
        name="dft_tables",
    )()


def _spectrum_kernel(fs_ref, f_ref, w_ref, *, S, Ls):
    scale = 1.0 / Ls
    k = lax.broadcasted_iota(jnp.int32, (Ls, 1), 0)
    sgn = jnp.where(k % 2 == 0, 1.0, -1.0)
    prev = None
    for i in range(S + 1):
        re = im = None
        if i < S:
            g = jnp.dot(fs_ref[...], f_ref[i * Ls:(i + 1) * Ls, :], preferred_element_type=F32) * scale
            re, im = g[:Ls], g[Ls:]
        cur = (re, im)
        if prev is not None:
            jre, jim = -sgn * prev[1], sgn * prev[0]
            re = jre if re is None else re + jre
            im = jim if im is None else im + jim
        w_ref[0, i, :Ls] = re.astype(w_ref.dtype)
        w_ref[0, i, Ls:] = im.astype(w_ref.dtype)
        prev = cur


def _filter_spectrum(fs, filt, S, C, dtype):
    n2, Ls = fs.shape
    L = filt.shape[0]
    nct = W_B // C
    return pl.pallas_call(
        functools.partial(_spectrum_kernel, S=S, Ls=Ls),
        grid=(2, nct),
        in_specs=[pl.BlockSpec((n2, Ls), lambda o, ct: (0, 0)),
                  pl.BlockSpec((L, C), lambda o, ct: (0, o * nct + ct))],
        out_specs=pl.BlockSpec((1, S + 1, n2, C), lambda o, ct: (o, 0, 0, ct)),
        out_shape=jax.ShapeDtypeStruct((2, S + 1, n2, W_B), dtype),
        compiler_params=_cparams(2),
        name="filter_spectrum",
    )(fs, filt)


def _hyena_filters(L, w1, b1, w2, b2, freq, w3):
    t = jnp.arange(L, dtype=F32)
    tn = t / (L - 1)
    f = jnp.linspace(1e-4, HY_BANDS - 1, HY_BANDS, dtype=F32)
    ang = (2.0 * math.pi / L) * t[:, None] * f[None, :]
    z = jnp.concatenate([tn[:, None], jnp.cos(ang), -jnp.sin(ang)], axis=-1)
    h = jnp.sin(freq * (jnp.dot(z, w1, precision=HIGHEST) + b1))
    h = jnp.sin(freq * (jnp.dot(h, w2, precision=HIGHEST) + b2))
    h = jnp.dot(h, w3, precision=HIGHEST).reshape(L, 2, W_B)
    max_decay = math.log(HY_TARGET) / HY_FAST_PCT
    min_decay = math.log(HY_TARGET) / HY_SLOW_PCT
    deltas = jnp.abs(jnp.linspace(min_decay, max_decay, W_B, dtype=F32))
    half = L // 2
    off = jnp.abs(t - half) / half
    win = jnp.exp(-off[:, None] * deltas[None, :]) + HY_SHIFT
    return h * win[:, None, :]


HY_RF = 32
HY_R = 128
SUBLANES = 8


def _short_conv_rows(u_ref, w, bias, c, R, L):
    r0 = pl.multiple_of(c * R, R)
    cur = u_ref[pl.ds(r0, R), :]
    before = u_ref[pl.ds(pl.multiple_of(jnp.maximum(r0 - SUBLANES, 0), SUBLANES), SUBLANES), :]
    after = u_ref[pl.ds(pl.multiple_of(jnp.minimum(r0 + R, L - SUBLANES), SUBLANES), SUBLANES), :]
    before = jnp.where(c == 0, 0.0, before[SUBLANES - 1:])
    after = jnp.where(c == L // R - 1, 0.0, after[:1])
    row = lax.broadcasted_iota(jnp.int32, (SUBLANES, 1), 0)
    up = pltpu.roll(cur, 1, 0)
    up = jnp.concatenate([jnp.where(row == 0, before, up[:SUBLANES]), up[SUBLANES:]], axis=0)
    dn = pltpu.roll(cur, R - 1, 0)
    dn = jnp.concatenate([dn[:R - SUBLANES], jnp.where(row == SUBLANES - 1, after, dn[R - SUBLANES:])], axis=0)
    return bias + up * w[0:1] + cur * w[1:2] + dn * w[2:3]


def _hy_kernel(v_ref, x_ref, cwv_ref, cbv_ref, cwx_ref, cbx_ref, hb_ref, fs_ref, fst_ref, w_ref, o_ref,
               zf_ref, zb_ref, u_ref, d_ref, os_ref, *, L, S, R):
    Ls = L // S
    C = zf_ref.shape[1]
    o = pl.program_id(2)
    seg = lambda j: slice(j * C, (j + 1) * C)

    @pl.when(o == 0)
    def _():
        for j in range(S):
            def body(cc, _, j=j):
                c = j * (Ls // R) + cc
                z = _short_conv_rows(v_ref, cwv_ref[...], cbv_ref[...], c, R, L)
                zf_ref[pl.ds(pl.multiple_of(c * R, R), R), :] = z
                zb_ref[pl.ds(pl.multiple_of(cc * R, R), R), seg(j)] = z.astype(BF16)
                return 0
            lax.fori_loop(0, Ls // R, body, 0)

    u_ref[...] = jnp.dot(fs_ref[...], zb_ref[...], preferred_element_type=F32).astype(u_ref.dtype)

    def prod(fc, _):
        f0 = pl.multiple_of(fc * HY_RF, HY_RF)
        re_rows = pl.ds(f0, HY_RF)
        im_rows = pl.ds(Ls + f0, HY_RF)
        for r in range(S):
            dre = dim = None
            for j in range(S):
                d = r - j + S // 2
                if 0 <= d <= S:
                    ur, ui = u_ref[re_rows, seg(j)], u_ref[im_rows, seg(j)]
                    wr, wi = w_ref[o, d, re_rows, :], w_ref[o, d, im_rows, :]
                    pre, pim = ur * wr - ui * wi, ur * wi + ui * wr
                    dre = pre if dre is None else dre + pre
                    dim = pim if dim is None else dim + pim
            d_ref[re_rows, seg(r)] = dre.astype(BF16)
            d_ref[im_rows, seg(r)] = dim.astype(BF16)
        return 0
    lax.fori_loop(0, Ls // HY_RF, prod, 0)

    os_ref[...] = jnp.dot(fst_ref[...], d_ref[...], preferred_element_type=F32)

    for r in range(S):
        def gate(cc, _, r=r):
            c = r * (Ls // R) + cc
            rows = pl.ds(pl.multiple_of(c * R, R), R)
            local = pl.ds(pl.multiple_of(cc * R, R), R)
            x = _short_conv_rows(x_ref, cwx_ref[...], cbx_ref[...], c, R, L)
            z = x * (os_ref[local, seg(r)] + zf_ref[rows, :] * hb_ref[0])

            @pl.when(o == 0)
            def _():
                zf_ref[rows, :] = z
                zb_ref[local, seg(r)] = z.astype(BF16)

            @pl.when(o == 1)
            def _():
                o_ref[rows, :] = z
            return 0
        lax.fori_loop(0, Ls // R, gate, 0)


def _hyena_mix(ut, B, conv_w, conv_b, bias, fs, fst, w, C):
    L = ut.shape[0]
    n2, Ls = fs.shape
    S = L // Ls
    nct = W_B // C
    ncol = UT_COLS // C
    xcol = lambda o, ct: (1 + o) * nct + ct
    conv_b = conv_b.reshape(1, -1)
    R = min(HY_R, Ls)
    return pl.pallas_call(
        functools.partial(_hy_kernel, L=L, S=S, R=R),
        grid=(B, nct, 2),
        in_specs=[pl.BlockSpec((L, C), lambda b, ct, o: (0, b * ncol + ct)),
                  pl.BlockSpec((L, C), lambda b, ct, o: (0, b * ncol + xcol(o, ct))),
                  pl.BlockSpec((3, C), lambda b, ct, o: (0, ct)),
                  pl.BlockSpec((1, C), lambda b, ct, o: (0, ct)),
                  pl.BlockSpec((3, C), lambda b, ct, o: (0, xcol(o, ct))),
                  pl.BlockSpec((1, C), lambda b, ct, o: (0, xcol(o, ct))),
                  pl.BlockSpec((1, 1, C), lambda b, ct, o: (o, 0, ct)),
                  pl.BlockSpec((n2, Ls), lambda b, ct, o: (0, 0)),
                  pl.BlockSpec((Ls, n2), lambda b, ct, o: (0, 0)),
                  pl.BlockSpec((2, S + 1, n2, C), lambda b, ct, o: (0, 0, 0, ct))],
        out_specs=pl.BlockSpec((L, C), lambda b, ct, o: (0, b * nct + ct)),
        out_shape=jax.ShapeDtypeStruct((L, B * W_B), F32),
        scratch_shapes=[pltpu.VMEM((L, C), F32), pltpu.VMEM((Ls, S * C), BF16), pltpu.VMEM((n2, S * C), w.dtype),
                        pltpu.VMEM((n2, S * C), BF16), pltpu.VMEM((Ls, S * C), F32)],
        compiler_params=_cparams(3),
        name="hyena_mix",
    )(ut, ut, conv_w, conv_b, conv_w, conv_b, bias.reshape(2, 1, W_B), fs, fst, w)


POOL_PAD = 16


def _pool_kernel(u_ref, w_ref, sc_ref, o_ref, *, L):
    assert POOL_WINDOWS == (2, 4, 8, 16)
    n = L + 2 * POOL_PAD
    shifted = lambda x, s: pltpu.roll(x, (-s) % n, 0)
    for gi, win in enumerate(POOL_WINDOWS):
        u = u_ref[:, gi * POOL_G:(gi + 1) * POOL_G]
        pad = jnp.zeros((POOL_PAD, POOL_G), F32)
        ext = jnp.concatenate([pad, u, pad], axis=0)
        s = ext + shifted(ext, -1)
        half = 1
        while 2 * half < win:
            s = shifted(s, -half) + shifted(s, half)
            half *= 2
        s = s[POOL_PAD:POOL_PAD + L]

        def mean_minus_token(sl, t0):
            t = t0 + lax.broadcasted_iota(jnp.int32, (SUBLANES, 1), 0)
            cnt = jnp.minimum(t - win // 2 + win, L) - jnp.maximum(t - win // 2, 0)
            return s[sl] / cnt.astype(F32) - u[sl]

        edge = SUBLANES
        d = jnp.concatenate([mean_minus_token(slice(0, edge), 0),
                             s[edge:L - edge] * (1.0 / win) - u[edge:L - edge],
                             mean_minus_token(slice(L - edge, L), L - edge)], axis=0)
        o_ref[:, gi * POOL_G:(gi + 1) * POOL_G] = (
            jnp.dot(d.astype(BF16), w_ref[gi], preferred_element_type=F32)
            * sc_ref[:, gi * POOL_G:(gi + 1) * POOL_G])


def _pool_mix(ut, B, w, scale):
    L = ut.shape[0]
    nblk = UT_COLS // W_C
    return pl.pallas_call(
        functools.partial(_pool_kernel, L=L),
        grid=(B,),
        in_specs=[pl.BlockSpec((L, W_C), lambda b: (0, b * nblk + nblk - 1)),
                  pl.BlockSpec((len(POOL_WINDOWS), POOL_G, POOL_G), lambda b: (0, 0, 0)),
                  pl.BlockSpec((1, W_C), lambda b: (0, 0))],
        out_specs=pl.BlockSpec((L, W_C), lambda b: (0, b)),
        out_shape=jax.ShapeDtypeStruct((L, B * W_C), F32),
        compiler_params=_cparams(1),
        name="pool_mix",
    )(ut, w, scale.reshape(1, W_C))


def _silu(x):
    return x * jax.nn.sigmoid(x)


def _gelu_tanh(x):
    return x * (0.5 * (1.0 + jnp.tanh(math.sqrt(2.0 / math.pi) * (x + 0.044715 * (x * x * x)))))


def _out_kernel(*refs, has_pos, batch_major, final, B, bm):
    refs = list(refs)
    x_ref = refs.pop(0)
    pos_ref = refs.pop(0) if has_pos else None
    (mod_ref, h_ref, ua_ref, ya_ref, zb_ref, yc_ref, wga_ref, wgb_ref, wgc_ref, wma_ref, wmb_ref, wmc_ref,
     d_ref, wglu_ref, bglu_ref, wa_ref, wb_ref, wc_ref, wo_ref) = refs[:19]
    refs = refs[19:]
    fg_ref = refs.pop(0) if final else None
    xo_ref, zs_ref, ys_ref = refs[:3]
    xs_ref = refs[3] if (batch_major or final) else None
    for b in range(B):
        _put_batch(zs_ref, b, B, zb_ref[:, b * W_B:(b + 1) * W_B])
        _put_batch(ys_ref, b, B, yc_ref[:, b * W_C:(b + 1) * W_C])
    mod = mod_ref[...]
    h = h_ref[...]
    proj = lambda w_ref: jnp.dot(h, w_ref[0], preferred_element_type=F32)
    y = _gelu_tanh(ya_ref[...] + ua_ref[...] * d_ref[0])
    y = y * jax.nn.sigmoid(jnp.dot(y.astype(BF16), wglu_ref[0], preferred_element_type=F32) + bglu_ref[0])
    y_a = (y * _silu(proj(wga_ref))).astype(BF16)
    y_b = (_get_tile(zs_ref) * _silu(proj(wgb_ref))).astype(BF16)
    y_c = (_get_tile(ys_ref) * _silu(proj(wgc_ref))).astype(BF16)
    merged = (jax.nn.sigmoid(proj(wma_ref)) * jnp.dot(y_a, wa_ref[0], preferred_element_type=F32)
              + jax.nn.sigmoid(proj(wmb_ref)) * jnp.dot(y_b, wb_ref[0], preferred_element_type=F32)
              + jax.nn.sigmoid(proj(wmc_ref)) * jnp.dot(y_c, wc_ref[0], preferred_element_type=F32))
    out = jnp.dot(merged.astype(BF16), wo_ref[0], preferred_element_type=F32)
    x = _load_x(x_ref, xs_ref if batch_major else None, pos_ref, B, bm)
    x_new = x + _mod_part(mod, 2, bm) * _by_batch(out, bm)
    if final:
        r = lax.rsqrt(jnp.mean(x_new * x_new, axis=-1, keepdims=True) + EPS)
        _put_tile(xs_ref, (x_new * r * fg_ref[...]).reshape(-1, D_MODEL))
        for b in range(B):
            xo_ref[b] = _get_batch(xs_ref, b, B)
    else:
        xo_ref[...] = x_new.reshape(-1, D_MODEL)


def _out_proj(x, pos, mod, mod_blk, B, bm, h, ua, ya, zb, yc, w_in, l, d, w_glu, b_glu,
              w_a, w_b, w_c, w_o, final_g, tm):
    D = D_MODEL
    rows = x.size // D
    L = rows // B
    batch_major = x.ndim == 3
    final = final_g is not None
    tok = lambda w: pl.BlockSpec((tm, w), lambda i: (i, 0))
    seq = lambda w: pl.BlockSpec((tm // B, B * w), lambda i: (i, 0))
    full = lambda a: _const_spec(a.shape, (0,) * a.ndim)
    in_specs, args = _x_specs(x, pos, mod, mod_blk, B, bm, tm)
    in_specs += [tok(D), tok(W_A), tok(W_A), seq(W_B), seq(W_C)]
    args += [h, ua, ya, zb, yc]
    for cb in G_BLOCKS:
        in_specs.append(_const_spec((1, D, CB), (l, 0, cb)))
        args.append(w_in)
    for k in range(3):
        in_specs.append(_const_spec((1, D, D), (l, 0, M_BLOCK0 + k)))
        args.append(w_in)
    weights = [d, w_glu, b_glu, w_a, w_b, w_c, w_o]
    in_specs += [_const_spec((1,) + a.shape[1:], (l, 0, 0)) for a in weights]
    args += weights
    if final:
        in_specs.append(full(final_g))
        args.append(final_g)
    scratch = [_stage(tm, W_B), _stage(tm, W_C)]
    if batch_major or final:
        scratch.append(_stage(tm, D))
    if final:
        out_spec = pl.BlockSpec((B, tm // B, D), lambda i: (0, i, 0))
        out_shape = jax.ShapeDtypeStruct((B, L, D), F32)
    else:
        out_spec = tok(D)
        out_shape = jax.ShapeDtypeStruct((rows, D), F32)
    return pl.pallas_call(
        functools.partial(_out_kernel, has_pos=pos is not None, batch_major=batch_major, final=final,
                          B=B, bm=bm),
        grid=(rows // tm,),
        in_specs=in_specs,
        out_specs=out_spec,
        out_shape=out_shape,
        scratch_shapes=scratch,
        compiler_params=_cparams(1),
        name="out_proj",
    )(*args)


def _grid_pos_embed(L):
    rows = L // GRID_W
    r = jnp.broadcast_to(jnp.arange(rows, dtype=F32)[:, None], (rows, GRID_W)).reshape(-1)
    col = jnp.broadcast_to(jnp.arange(GRID_W, dtype=F32)[None, :], (rows, GRID_W)).reshape(-1)
    q = D_MODEL // 4
    omega = 1.0 / (10000.0 ** (jnp.arange(q, dtype=F32) / q))
    ar = r[:, None] * omega[None, :]
    ac = col[:, None] * omega[None, :]
    return jnp.concatenate([jnp.sin(ar), jnp.cos(ar), jnp.sin(ac), jnp.cos(ac)], axis=-1)


def _states_to_lanes(st_re, st_im):
    s = jnp.concatenate([st_re[:, 0], st_re[:, 1], st_im[:, 0], st_im[:, 1]], axis=-1)
    return s.transpose(1, 0, 2)


def _lanes_to_states(fin):
    f = fin.transpose(1, 0, 2).reshape(fin.shape[1], S5_G, 4, S5_P)
    return (jnp.stack([f[:, :, 0], f[:, :, 1]], axis=1), jnp.stack([f[:, :, 2], f[:, :, 3]], axis=1))


def kernel(x_prompt, x_sample, c, state_s5_re, state_s5_im, c_ctx, norm_g, w_mod, b_mod, w_in, s5_lam_re, s5_lam_im, s5_log_dt, s5_b_re, s5_b_im, s5_c_re, s5_c_im, s5_d, s5_w_glu, s5_b_glu, hy_conv_w, hy_conv_b, hy_f_w1, hy_f_b1, hy_f_w2, hy_f_b2, hy_f_freq, hy_f_w3, hy_bias, pool_w, pool_scale, w_br_a, w_br_b, w_br_c, w_out, final_g):
    Bc, Lc, D = x_prompt.shape
    Bl, Ll, _ = x_sample.shape
    assert Bl == MOD_ROWS - MOD_LAT0

    cond = jnp.zeros((MOD_ROWS, D), F32).at[0].set(c_ctx).at[MOD_LAT0:].set(c)
    mod = _modulation(cond, w_mod, b_mod).reshape(DEPTH * MOD_ROWS, 3 * D)

    pos = _grid_pos_embed(Ll)
    groups = {
        'ctx': dict(B=Bc, L=Lc, S=2, C=W_B, hy_dtype=F32, bm=1, mod_blk=0),
        'lat': dict(B=Bl, L=Ll, S=4, C=W_B // 2, hy_dtype=BF16, bm=Bl, mod_blk=1),
    }
    tables = {k: _dft_tables(v['L'] // v['S']) for k, v in groups.items()}
    w_in_b = w_in.astype(BF16)
    out_weights = [s5_d.reshape(DEPTH, 1, W_A), s5_w_glu.astype(BF16), s5_b_glu.reshape(DEPTH, 1, W_A),
                   w_br_a.astype(BF16), w_br_b.astype(BF16), w_br_c.astype(BF16), w_out.astype(BF16)]
    xs = {'ctx': x_prompt, 'lat': x_sample}
    new_re, new_im = [], []
    for l in range(DEPTH):
        s5_ops = _s5_operators(s5_lam_re[l], s5_lam_im[l], s5_log_dt[l], s5_b_re[l], s5_b_im[l],
                               s5_c_re[l], s5_c_im[l])
        g = norm_g[l].reshape(1, D)
        for name, cfg in groups.items():
            B, L, bm = cfg['B'], cfg['L'], cfg['bm']
            x = xs[name]
            p = pos if (name == 'lat' and l == 0) else None
            mod_blk = l * (MOD_ROWS // MOD_LAT0) + cfg['mod_blk']
            ua, ut, hn = _in_proj(x, p, mod, mod_blk, B, bm, g, w_in_b, l, min(TM_IN, L * B))
            if name == 'ctx':
                h0g = jnp.zeros((S5_G, B, 4 * S5_P), F32)
            else:
                h0g = _states_to_lanes(state_s5_re[:, l], state_s5_im[:, l])
            ya, fin = _s5_mix(ua, s5_ops, h0g, B)
            if name == 'ctx':
                fr, fi = _lanes_to_states(fin)
                new_re.append(fr)
                new_im.append(fi)
            fs, fst = tables[name]
            filt = _hyena_filters(L, hy_f_w1[l], hy_f_b1[l], hy_f_w2[l], hy_f_b2[l], hy_f_freq[l], hy_f_w3[l])
            hw = _filter_spectrum(fs, filt.reshape(L, 2 * W_B).astype(BF16), cfg['S'], cfg['C'], cfg['hy_dtype'])
            zb = _hyena_mix(ut, B, hy_conv_w[l], hy_conv_b[l], hy_bias[l], fs, fst, hw, cfg['C'])
            yc = _pool_mix(ut, B, pool_w[l].astype(BF16), pool_scale[l])
            fg = final_g.reshape(1, D) if l == DEPTH - 1 else None
            xs[name] = _out_proj(x, p, mod, mod_blk, B, bm, hn, ua, ya, zb, yc, w_in_b, l,
                                 *out_weights, fg, min(TM_OUT, L * B))
    return (xs['ctx'], xs['lat'], jnp.stack(new_re, axis=1), jnp.stack(new_im, axis=1))
```

```python
import functools
import math

import jax
import jax.numpy as jnp
from jax import lax
from jax.experimental import pallas as pl
from jax.experimental.pallas import tpu as pltpu

F32 = jnp.float32
BF16 = jnp.bfloat16
HIGHEST = lax.Precision.HIGHEST

D_MODEL = 1024
DEPTH = 2
GRID_W = 64
EPS = 1e-6
W_A = D_MODEL // 2
S5_H = 16
S5_G = W_A // S5_H
S5_P = 64
W_B = D_MODEL // 2
HY_BANDS = 16
HY_FAST_PCT = 0.3
HY_SLOW_PCT = 1.5
HY_TARGET = 1e-2
HY_SHIFT = 0.05
W_C = D_MODEL // 2
POOL_WINDOWS = (2, 4, 8, 16)
POOL_G = W_C // 4
LANES = 128
CB = 512
U_BLOCKS = (0, 2, 3, 4, 6)
G_BLOCKS = (1, 5, 7)
M_BLOCK0 = 4
UT_COLS = CB * (len(U_BLOCKS) - 1)
S5_CHUNK = 16
S5_W = S5_CHUNK * S5_H
S5_GB = LANES // S5_H
MOD_ROWS = 16
MOD_LAT0 = 8
VMEM_LIMIT = 56 * 1024 * 1024
TM_IN = 1024
TM_OUT = 256


def _const_spec(block_shape, index):
    return pl.BlockSpec(block_shape, lambda *_: index, pipeline_mode=pl.Buffered(1))


def _cparams(n_grid):
    return pltpu.CompilerParams(dimension_semantics=("arbitrary",) * n_grid,
                                vmem_limit_bytes=VMEM_LIMIT)


def _mod_kernel(c_ref, w_ref, b_ref, o_ref):
    c = c_ref[...]
    s = c * jax.nn.sigmoid(c)
    o_ref[0] = jnp.dot(s, w_ref[0], preferred_element_type=F32, precision=HIGHEST) + b_ref[0]


def _modulation(cond, w_mod, b_mod, tn=3 * D_MODEL // 2):
    n = 3 * D_MODEL
    return pl.pallas_call(
        _mod_kernel,
        grid=(DEPTH, n // tn),
        in_specs=[pl.BlockSpec((MOD_ROWS, D_MODEL), lambda l, j: (0, 0)),
                  pl.BlockSpec((1, D_MODEL, tn), lambda l, j: (l, 0, j)),
                  pl.BlockSpec((1, 1, tn), lambda l, j: (l, 0, j))],
        out_specs=pl.BlockSpec((1, MOD_ROWS, tn), lambda l, j: (l, 0, j)),
        out_shape=jax.ShapeDtypeStruct((DEPTH, MOD_ROWS, n), F32),
        compiler_params=_cparams(2),
        name="modulation",
    )(cond, w_mod, b_mod.reshape(DEPTH, 1, n))


def _by_batch(x, bm):
    return x if bm == 1 else x.reshape(x.shape[0] // bm, bm, x.shape[1])


def _mod_part(mod, i, bm):
    m = mod[:bm, i * D_MODEL:(i + 1) * D_MODEL]
    return m if bm == 1 else m[None]


def _stage(rows, w):
    return pltpu.VMEM((w // LANES, rows, LANES), F32)


def _put_batch(ref, b, B, val):
    for c in range(ref.shape[0]):
        ref[c, pl.ds(b, val.shape[0], stride=B), :] = val[:, c * LANES:(c + 1) * LANES]


def _get_batch(ref, b, B):
    tt = ref.shape[1] // B
    return jnp.concatenate([ref[c, pl.ds(b, tt, stride=B), :] for c in range(ref.shape[0])], axis=1)


def _put_tile(ref, val):
    for c in range(ref.shape[0]):
        ref[c] = val[:, c * LANES:(c + 1) * LANES]


def _get_tile(ref):
    return jnp.concatenate([ref[c] for c in range(ref.shape[0])], axis=1)


def _load_x(x_ref, xs_ref, pos_ref, B, bm):
    if xs_ref is not None:
        for b in range(B):
            _put_batch(xs_ref, b, B, x_ref[b])
        x2 = _get_tile(xs_ref)
    else:
        x2 = x_ref[...]
    x = _by_batch(x2, bm)
    if pos_ref is not None:
        x = x + pos_ref[...][:, None, :]
    return x


def _normed(x, mod, g, bm):
    r = lax.rsqrt(jnp.mean(x * x, axis=-1, keepdims=True) + EPS)
    h = (x * r * g) * (1.0 + _mod_part(mod, 1, bm)) + _mod_part(mod, 0, bm)
    return h.reshape(-1, D_MODEL).astype(BF16)


def _in_kernel(*refs, has_pos, batch_major, B, bm):
    refs = list(refs)
    x_ref = refs.pop(0)
    pos_ref = refs.pop(0) if has_pos else None
    mod_ref, g_ref = refs[:2]
    w_refs = refs[2:2 + len(U_BLOCKS)]
    oa_ref, ot_ref, h_ref, scr_ref = refs[2 + len(U_BLOCKS):6 + len(U_BLOCKS)]
    xs_ref = refs[-1] if batch_major else None
    h = _normed(_load_x(x_ref, xs_ref, pos_ref, B, bm), mod_ref[...], g_ref[...], bm)
    h_ref[...] = h
    oa_ref[...] = jnp.dot(h, w_refs[0][0], preferred_element_type=F32)
    for i, w_ref in enumerate(w_refs[1:]):
        _put_tile(scr_ref, jnp.dot(h, w_ref[0], preferred_element_type=F32))
        for b in range(B):
            ot_ref[:, b * UT_COLS + i * CB:b * UT_COLS + (i + 1) * CB] = _get_batch(scr_ref, b, B)


def _x_specs(x, pos, mod, mod_blk, B, bm, tm):
    D = D_MODEL
    if x.ndim == 3:
        specs = [pl.BlockSpec((B, tm // B, D), lambda i: (0, i, 0))]
    else:
        specs = [pl.BlockSpec((tm, D), lambda i: (i, 0))]
    args = [x]
    if pos is not None:
        specs.append(pl.BlockSpec((tm // bm, D), lambda i: (i, 0)))
        args.append(pos)
    specs.append(_const_spec((MOD_LAT0, 3 * D), (mod_blk, 0)))
    args.append(mod)
    return specs, args


def _in_proj(x, pos, mod, mod_blk, B, bm, g, w_in, l, tm):
    D = D_MODEL
    rows = x.size // D
    L = rows // B
    batch_major = x.ndim == 3
    in_specs, args = _x_specs(x, pos, mod, mod_blk, B, bm, tm)
    in_specs.append(_const_spec((1, D), (0, 0)))
    args.append(g)
    for cb in U_BLOCKS:
        in_specs.append(_const_spec((1, D, CB), (l, 0, cb)))
        args.append(w_in)
    scratch = [_stage(tm, CB)] + ([_stage(tm, D)] if batch_major else [])
    return pl.pallas_call(
        functools.partial(_in_kernel, has_pos=pos is not None, batch_major=batch_major, B=B, bm=bm),
        grid=(rows // tm,),
        in_specs=in_specs,
        out_specs=[pl.BlockSpec((tm, W_A), lambda i: (i, 0)),
                   pl.BlockSpec((tm // B, B * UT_COLS), lambda i: (i, 0)),
                   pl.BlockSpec((tm, D), lambda i: (i, 0))],
        out_shape=[jax.ShapeDtypeStruct((rows, W_A), F32),
                   jax.ShapeDtypeStruct((L, B * UT_COLS), F32),
                   jax.ShapeDtypeStruct((rows, D), BF16)],
        scratch_shapes=scratch,
        compiler_params=_cparams(1),
        name="in_proj",
    )(*args)


S5_RB = 128
S5_GS = 4


def _s5_kernel(u_ref, m_ref, p_ref, qt_ref, a_ref, h0_ref, y_ref, fin_ref,
               x_ref, yall_ref, sloc_ref, sinf_ref, sinb_ref, *, nc, B):
    rows = nc * B
    cpb = S5_RB // B
    half = S5_W // 2
    lane_grp = lax.broadcasted_iota(jnp.int32, (S5_RB, LANES), 1) // S5_H

    lane = lax.broadcasted_iota(jnp.int32, (B, half), 1)
    is_fwd = lane < S5_P
    lane_full = lax.broadcasted_iota(jnp.int32, (rows, S5_W), 1)
    is_fwd_full = (lane_full % half) < S5_P

    def gather_step(rb, _, g0):
        c0 = pl.multiple_of(rb * cpb, cpb)
        r0 = pl.multiple_of(rb * S5_RB, S5_RB)
        slabs = [u_ref[pl.ds(c0, cpb), t * B:(t + 1) * B, :].reshape(S5_RB, LANES) for t in range(S5_CHUNK)]
        rolled = [s if t % S5_GB == 0 else pltpu.roll(s, (t % S5_GB) * S5_H, 1) for t, s in enumerate(slabs)]
        for gi in range(S5_GS):
            g8 = g0 + gi
            for hf in range(2):
                acc = rolled[hf * S5_GB]
                for k in range(1, S5_GB):
                    acc = jnp.where(lane_grp == (k + g8) % S5_GB, rolled[hf * S5_GB + k], acc)
                x_ref[gi, pl.ds(r0, S5_RB), hf * LANES:(hf + 1) * LANES] = acc.astype(BF16)
        return 0

    def group_step(pi, _, g0):
        gis = (2 * pi, 2 * pi + 1)
        ars, ais, init = [], [], []
        for k, gi in enumerate(gis):
            sloc_ref[k] = jnp.dot(x_ref[gi], p_ref[g0 + gi], preferred_element_type=F32)
            a = a_ref[g0 + gi]
            ars.append(a[:, :half])
            ais.append(a[:, half:])
            h0 = h0_ref[g0 + gi]
            init += [h0[:, :half], h0[:, half:]]

        def body(i, carry):
            jf = pl.multiple_of(i * B, B)
            jb = pl.multiple_of((nc - 1 - i) * B, B)
            out = []
            for k in range(2):
                cr, ci = carry[2 * k], carry[2 * k + 1]
                sinf_ref[k, pl.ds(jf, B), :half] = cr
                sinf_ref[k, pl.ds(jf, B), half:] = ci
                sinb_ref[k, pl.ds(jb, B), :half] = cr
                sinb_ref[k, pl.ds(jb, B), half:] = ci
                lf = sloc_ref[k, pl.ds(jf, B), :]
                lb = sloc_ref[k, pl.ds(jb, B), :]
                lr = jnp.where(is_fwd, lf[:, :half], lb[:, :half])
                li = jnp.where(is_fwd, lf[:, half:], lb[:, half:])
                out += [ars[k] * cr - ais[k] * ci + lr, ars[k] * ci + ais[k] * cr + li]
            return tuple(out)

        fin = lax.fori_loop(0, nc, body, tuple(init))
        for k, gi in enumerate(gis):
            g8 = g0 + gi
            fin_ref[g8, :, :half] = fin[2 * k]
            fin_ref[g8, :, half:] = fin[2 * k + 1]
            s_in = jnp.where(is_fwd_full, sinf_ref[k], sinb_ref[k]).astype(BF16)
            yall_ref[gi] = (jnp.dot(x_ref[gi], m_ref[g8], preferred_element_type=F32)
                            + lax.dot_general(s_in, qt_ref[g8], (((1,), (1,)), ((), ())),
                                              preferred_element_type=F32))
        return 0

    def scatter_step(rb, _, g0):
        c0 = pl.multiple_of(rb * cpb, cpb)
        r0 = pl.multiple_of(rb * S5_RB, S5_RB)
        for hf in range(2):
            ys = [yall_ref[gi, pl.ds(r0, S5_RB), hf * LANES:(hf + 1) * LANES] for gi in range(S5_GS)]
            for k in range(S5_GB):
                acc = ys[0]
                for gi in range(1, S5_GS):
                    acc = jnp.where(lane_grp == (k + g0 + gi) % S5_GB, ys[gi], acc)
                if k:
                    acc = pltpu.roll(acc, (S5_GB - k) * S5_H, 1)
                t = hf * S5_GB + k
                dst = (pl.ds(c0, cpb), slice(t * B, (t + 1) * B), slice(None))
                if g0 > 0:
                    acc = jnp.where(lane_grp >= g0, acc, y_ref[dst].reshape(S5_RB, LANES))
                y_ref[dst] = acc.reshape(cpb, B, LANES)
        return 0

    for g0 in range(0, S5_GB, S5_GS):
        lax.fori_loop(0, rows // S5_RB, functools.partial(gather_step, g0=g0), 0)
        lax.fori_loop(0, S5_GS // 2, functools.partial(group_step, g0=g0), 0)
        lax.fori_loop(0, rows // S5_RB, functools.partial(scatter_step, g0=g0), 0)


def _s5_mix(ua, ops, h0, B):
    m, p, q, a16 = ops
    rows_all = ua.shape[0]
    nc = rows_all // (S5_CHUNK * B)
    rows = nc * B
    up3 = ua.reshape(nc, S5_CHUNK * B, W_A)
    gblk = lambda r: pl.BlockSpec((S5_GB, r, S5_W), lambda j: (j, 0, 0))
    tok = pl.BlockSpec((nc, S5_CHUNK * B, LANES), lambda j: (0, 0, j))
    y, fin = pl.pallas_call(
        functools.partial(_s5_kernel, nc=nc, B=B),
        grid=(W_A // LANES,),
        in_specs=[tok, gblk(S5_W), gblk(S5_W), gblk(S5_W), gblk(1), gblk(B)],
        out_specs=[tok, gblk(B)],
        out_shape=[jax.ShapeDtypeStruct((nc, S5_CHUNK * B, W_A), F32),
                   jax.ShapeDtypeStruct((S5_G, B, S5_W), F32)],
        scratch_shapes=[pltpu.VMEM((S5_GS, rows, S5_W), BF16), pltpu.VMEM((S5_GS, rows, S5_W), F32),
                        pltpu.VMEM((2, rows, S5_W), F32), pltpu.VMEM((2, rows, S5_W), F32),
                        pltpu.VMEM((2, rows, S5_W), F32)],
        compiler_params=_cparams(1),
        name="s5_mix",
    )(up3, m, p, q, a16, h0)
    return y.reshape(rows_all, W_A), fin


def _s5_ops_kernel(lr_ref, li_ref, ldt_ref, br_ref, bi_ref, cr_ref, ci_ref, m_ref, p_ref, qt_ref, a_ref):
    T = S5_CHUNK
    H = S5_H
    lam_re = lr_ref[0]
    lam_im = li_ref[0]
    dt = jnp.exp(ldt_ref[0])
    mag = jnp.exp(lam_re * dt)
    ang = lam_im * dt
    a_re = mag * jnp.cos(ang)
    a_im = mag * jnp.sin(ang)
    n_re = a_re - 1.0
    n_im = a_im
    den = lam_re * lam_re + lam_im * lam_im
    k_re = (n_re * lam_re + n_im * lam_im) / den
    k_im = (n_im * lam_re - n_re * lam_im) / den
    b_re = br_ref[0]
    b_im = bi_ref[0]
    bb_re = k_re * b_re - k_im * b_im
    bb_im = k_re * b_im + k_im * b_re
    c_re = cr_ref[0]
    c_im = ci_ref[0]
    ap = [(jnp.ones_like(a_re), jnp.zeros_like(a_re))]
    for _ in range(T):
        pr, pi = ap[-1]
        ap.append((pr * a_re - pi * a_im, pr * a_im + pi * a_re))
    is_fwd = lax.broadcasted_iota(jnp.int32, (1, LANES), 1) < S5_P

    def powers(kf, kb):
        return jnp.where(is_fwd, ap[kf][0], ap[kb][0]), jnp.where(is_fwd, ap[kf][1], ap[kb][1])

    g8 = pl.program_id(0) % S5_GB

    def pos_rows(t):
        p = (t // S5_GB) * S5_GB + (t % S5_GB + g8) % S5_GB
        return pl.ds(pl.multiple_of(p * H, H), H)

    for t in range(T):
        rows = pos_rows(t)
        er, ei = powers(T - 1 - t, t)
        p_ref[0, rows, :LANES] = (er * bb_re - ei * bb_im).astype(BF16)
        p_ref[0, rows, LANES:] = (er * bb_im + ei * bb_re).astype(BF16)
        er, ei = powers(t + 1, T - t)
        qt_ref[0, rows, :LANES] = (c_re * er - c_im * ei).astype(BF16)
        qt_ref[0, rows, LANES:] = (-(c_re * ei + c_im * er)).astype(BF16)
    a_ref[0, :, :LANES] = ap[T][0]
    a_ref[0, :, LANES:] = ap[T][1]

    def ca(k):
        return c_re * ap[k][0] - c_im * ap[k][1], c_re * ap[k][1] + c_im * ap[k][0]

    fwd = [ca(k) for k in range(T)]
    bwd = fwd[::-1]
    nt = lambda a, b: lax.dot_general(a, b, (((1,), (1,)), ((), ())), precision=HIGHEST,
                                      preferred_element_type=F32)
    cat = lambda parts, i: jnp.concatenate([p[i] for p in parts], axis=0)
    zero = jnp.zeros_like(bb_re)
    wf = (nt(jnp.where(is_fwd, bb_re, zero), cat(fwd, 0)) - nt(jnp.where(is_fwd, bb_im, zero), cat(fwd, 1)))
    wb = (nt(jnp.where(is_fwd, zero, bb_re), cat(bwd, 0)) - nt(jnp.where(is_fwd, zero, bb_im), cat(bwd, 1)))
    pad = jnp.zeros((H, S5_W), F32)
    wf_pad = jnp.concatenate([pad, wf], axis=1)
    wb_pad = jnp.concatenate([wb, pad], axis=1)
    for t in range(T):
        row = (wf_pad[:, S5_W - H * t:2 * S5_W - H * t] + wb_pad[:, (T - 1 - t) * H:(T - 1 - t) * H + S5_W])
        row = jnp.concatenate([pltpu.roll(row[:, :LANES], g8 * H, 1), pltpu.roll(row[:, LANES:], g8 * H, 1)],
                              axis=1)
        m_ref[0, pos_rows(t), :] = row.astype(BF16)


def _s5_operators(lam_re, lam_im, log_dt, b_re, b_im, c_re, c_im):
    G = S5_G
    dirs = lambda x: jnp.concatenate([x[0], x[1]], axis=-1)
    lam = [dirs(x)[:, None, :] for x in (lam_re, lam_im)]
    ldt = dirs(jnp.broadcast_to(log_dt[..., None], (2, G, S5_P)))[:, None, :]
    bt = [dirs(x.transpose(0, 1, 3, 2)) for x in (b_re, b_im)]
    ct = [dirs(x) for x in (c_re, c_im)]
    vec = pl.BlockSpec((1, 1, LANES), lambda g: (g, 0, 0))
    mat = pl.BlockSpec((1, S5_H, LANES), lambda g: (g, 0, 0))
    op = pl.BlockSpec((1, S5_W, S5_W), lambda g: (g, 0, 0))
    return pl.pallas_call(
        _s5_ops_kernel,
        grid=(G,),
        in_specs=[vec, vec, vec, mat, mat, mat, mat],
        out_specs=[op, op, op, pl.BlockSpec((1, 1, S5_W), lambda g: (g, 0, 0))],
        out_shape=[jax.ShapeDtypeStruct((G, S5_W, S5_W), BF16)] * 3
        + [jax.ShapeDtypeStruct((G, 1, S5_W), F32)],
        compiler_params=_cparams(1),
        name="s5_operators",
    )(*lam, ldt, *bt, *ct)


def _phase_tables(u, va, vb, ncb, n):
    def trig(m):
        m = m & (2 * n - 1)
        m = jnp.where(m >= n, m - 2 * n, m)
        ang = m.astype(F32) * (math.pi / n)
        return jnp.cos(ang), jnp.sin(ang)

    lane = lax.broadcasted_iota(jnp.int32, (1, LANES), 1)
    cb, sb = trig(u * (va * lane + vb))
    ca_all, sa_all = trig(u * (va * LANES * lane))
    out = []
    for ch in range(ncb):
        ca = ca_all[:, ch:ch + 1]
        sa = sa_all[:, ch:ch + 1]
        out.append((ca * cb - sa * sb, sa * cb + ca * sb))
    return out


def _tables_kernel(fs_ref, fst_ref, blk_ref, *, Ls):
    u = 2 * lax.broadcasted_iota(jnp.int32, (Ls, 1), 0) + 1
    for ch, (c, s) in enumerate(_phase_tables(u, 1, 0, Ls // LANES, 2 * Ls)):
        blk_ref[:Ls, ch * LANES:(ch + 1) * LANES] = c
        blk_ref[Ls:, ch * LANES:(ch + 1) * LANES] = -s
    blk = blk_ref[...]
    fs_ref[...] = blk.astype(BF16)
    fst_ref[...] = blk.T.astype(BF16)


def _dft_tables(Ls):
    return pl.pallas_call(
        functools.partial(_tables_kernel, Ls=Ls),
        out_shape=[jax.ShapeDtypeStruct((2 * Ls, Ls), BF16), jax.ShapeDtypeStruct((Ls, 2 * Ls), BF16)],
        scratch_shapes=[pltpu.VMEM((2 * Ls, Ls), F32)],
        compiler_params=pltpu.CompilerParams(vmem_limit_bytes=VMEM_LIMIT),
        name="dft_tables",
    )()


def _spectrum_kernel(fs_ref, f_ref, w_ref, *, S, Ls):
    scale = 1.0 / Ls
    k = lax.broadcasted_iota(jnp.int32, (Ls, 1), 0)
    sgn = jnp.where(k % 2 == 0, 1.0, -1.0)
    prev = None
    for i in range(S + 1):
        re = im = None
        if i < S:
            g = jnp.dot(fs_ref[...], f_ref[i * Ls:(i + 1) * Ls, :], preferred_element_type=F32) * scale
            re, im = g[:Ls], g[Ls:]
        cur = (re, im)
        if prev is not None:
            jre, jim = -sgn * prev[1], sgn * prev[0]
            re = jre if re is None else re + jre
            im = jim if im is None else im + jim
        w_ref[0, i, :Ls] = re.astype(w_ref.dtype)
        w_ref[0, i, Ls:] = im.astype(w_ref.dtype)
        prev = cur


def _filter_spectrum(fs, filt, S, C, dtype):
    n2, Ls = fs.shape
    L = filt.shape[0]
    nct = W_B // C
    return pl.pallas_call(
        functools.partial(_spectrum_kernel, S=S, Ls=Ls),
        grid=(2, nct),
        in_specs=[pl.BlockSpec((n2, Ls), lambda o, ct: (0, 0)),
                  pl.BlockSpec((L, C), lambda o, ct: (0, o * nct + ct))],
        out_specs=pl.BlockSpec((1, S + 1, n2, C), lambda o, ct: (o, 0, 0, ct)),
        out_shape=jax.ShapeDtypeStruct((2, S + 1, n2, W_B), dtype),
        compiler_params=_cparams(2),
        name="filter_spectrum",
    )(fs, filt)


def _hyena_filters(L, w1, b1, w2, b2, freq, w3):
    t = jnp.arange(L, dtype=F32)
    tn = t / (L - 1)
    f = jnp.linspace(1e-4, HY_BANDS - 1, HY_BANDS, dtype=F32)
    ang = (2.0 * math.pi / L) * t[:, None] * f[None, :]
    z = jnp.concatenate([tn[:, None], jnp.cos(ang), -jnp.sin(ang)], axis=-1)
    h = jnp.sin(freq * (jnp.dot(z, w1, precision=HIGHEST) + b1))
    h = jnp.sin(freq * (jnp.dot(h, w2, precision=HIGHEST) + b2))
    h = jnp.dot(h, w3, precision=HIGHEST).reshape(L, 2, W_B)
    max_decay = math.log(HY_TARGET) / HY_FAST_PCT
    min_decay = math.log(HY_TARGET) / HY_SLOW_PCT
    deltas = jnp.abs(jnp.linspace(min_decay, max_decay, W_B, dtype=F32))
    half = L // 2
    off = jnp.abs(t - half) / half
    win = jnp.exp(-off[:, None] * deltas[None, :]) + HY_SHIFT
    return h * win[:, None, :]


HY_RF = 32
HY_R = 128
SUBLANES = 8


def _short_conv_rows(u_ref, w, bias, c, R, L):
    r0 = pl.multiple_of(c * R, R)
    cur = u_ref[pl.ds(r0, R), :]
    before = u_ref[pl.ds(pl.multiple_of(jnp.maximum(r0 - SUBLANES, 0), SUBLANES), SUBLANES), :]
    after = u_ref[pl.ds(pl.multiple_of(jnp.minimum(r0 + R, L - SUBLANES), SUBLANES), SUBLANES), :]
    before = jnp.where(c == 0, 0.0, before[SUBLANES - 1:])
    after = jnp.where(c == L // R - 1, 0.0, after[:1])
    row = lax.broadcasted_iota(jnp.int32, (SUBLANES, 1), 0)
    up = pltpu.roll(cur, 1, 0)
    up = jnp.concatenate([jnp.where(row == 0, before, up[:SUBLANES]), up[SUBLANES:]], axis=0)
    dn = pltpu.roll(cur, R - 1, 0)
    dn = jnp.concatenate([dn[:R - SUBLANES], jnp.where(row == SUBLANES - 1, after, dn[R - SUBLANES:])], axis=0)
    return bias + up * w[0:1] + cur * w[1:2] + dn * w[2:3]


def _hy_kernel(v_ref, x_ref, cwv_ref, cbv_ref, cwx_ref, cbx_ref, hb_ref, fs_ref, fst_ref, w_ref, o_ref,
               zf_ref, zb_ref, u_ref, d_ref, os_ref, *, L, S, R):
    Ls = L // S
    C = zf_ref.shape[1]
    o = pl.program_id(2)
    seg = lambda j: slice(j * C, (j + 1) * C)

    @pl.when(o == 0)
    def _():
        for j in range(S):
            def body(cc, _, j=j):
                c = j * (Ls // R) + cc
                z = _short_conv_rows(v_ref, cwv_ref[...], cbv_ref[...], c, R, L)
                zf_ref[pl.ds(pl.multiple_of(c * R, R), R), :] = z
                zb_ref[pl.ds(pl.multiple_of(cc * R, R), R), seg(j)] = z.astype(BF16)
                return 0
            lax.fori_loop(0, Ls // R, body, 0)

    u_ref[...] = jnp.dot(fs_ref[...], zb_ref[...], preferred_element_type=F32).astype(u_ref.dtype)

    def prod(fc, _):
        f0 = pl.multiple_of(fc * HY_RF, HY_RF)
        re_rows = pl.ds(f0, HY_RF)
        im_rows = pl.ds(Ls + f0, HY_RF)
        for r in range(S):
            dre = dim = None
            for j in range(S):
                d = r - j + S // 2
                if 0 <= d <= S:
                    ur, ui = u_ref[re_rows, seg(j)], u_ref[im_rows, seg(j)]
                    wr, wi = w_ref[o, d, re_rows, :], w_ref[o, d, im_rows, :]
                    pre, pim = ur * wr - ui * wi, ur * wi + ui * wr
                    dre = pre if dre is None else dre + pre
                    dim = pim if dim is None else dim + pim
            d_ref[re_rows, seg(r)] = dre.astype(BF16)
            d_ref[im_rows, seg(r)] = dim.astype(BF16)
        return 0
    lax.fori_loop(0, Ls // HY_RF, prod, 0)

    os_ref[...] = jnp.dot(fst_ref[...], d_ref[...], preferred_element_type=F32)

    for r in range(S):
        def gate(cc, _, r=r):
            c = r * (Ls // R) + cc
            rows = pl.ds(pl.multiple_of(c * R, R), R)
            local = pl.ds(pl.multiple_of(cc * R, R), R)
            x = _short_conv_rows(x_ref, cwx_ref[...], cbx_ref[...], c, R, L)
            z = x * (os_ref[local, seg(r)] + zf_ref[rows, :] * hb_ref[0])

            zf_ref[rows, :] = z
            zb_ref[local, seg(r)] = z.astype(BF16)
            o_ref[rows, :] = z
            return 0
        lax.fori_loop(0, Ls // R, gate, 0)


def _hyena_mix(ut, B, conv_w, conv_b, bias, fs, fst, w, C):
    L = ut.shape[0]
    n2, Ls = fs.shape
    S = L // Ls
    nct = W_B // C
    ncol = UT_COLS // C
    xcol = lambda o, ct: (1 + o) * nct + ct
    conv_b = conv_b.reshape(1, -1)
    R = min(HY_R, Ls)
    return pl.pallas_call(
        functools.partial(_hy_kernel, L=L, S=S, R=R),
        grid=(B, nct, 2),
        in_specs=[pl.BlockSpec((L, C), lambda b, ct, o: (0, b * ncol + ct)),
                  pl.BlockSpec((L, C), lambda b, ct, o: (0, b * ncol + xcol(o, ct))),
                  pl.BlockSpec((3, C), lambda b, ct, o: (0, ct)),
                  pl.BlockSpec((1, C), lambda b, ct, o: (0, ct)),
                  pl.BlockSpec((3, C), lambda b, ct, o: (0, xcol(o, ct))),
                  pl.BlockSpec((1, C), lambda b, ct, o: (0, xcol(o, ct))),
                  pl.BlockSpec((1, 1, C), lambda b, ct, o: (o, 0, ct)),
                  pl.BlockSpec((n2, Ls), lambda b, ct, o: (0, 0)),
                  pl.BlockSpec((Ls, n2), lambda b, ct, o: (0, 0)),
                  pl.BlockSpec((2, S + 1, n2, C), lambda b, ct, o: (0, 0, 0, ct))],
        out_specs=pl.BlockSpec((L, C), lambda b, ct, o: (0, b * nct + ct)),
        out_shape=jax.ShapeDtypeStruct((L, B * W_B), F32),
        scratch_shapes=[pltpu.VMEM((L, C), F32), pltpu.VMEM((Ls, S * C), BF16), pltpu.VMEM((n2, S * C), w.dtype),
                        pltpu.VMEM((n2, S * C), BF16), pltpu.VMEM((Ls, S * C), F32)],
        compiler_params=_cparams(3),
        name="hyena_mix",
    )(ut, ut, conv_w, conv_b, conv_w, conv_b, bias.reshape(2, 1, W_B), fs, fst, w)


POOL_PAD = 16


def _pool_kernel(u_ref, w_ref, sc_ref, o_ref, *, L):
    assert POOL_WINDOWS == (2, 4, 8, 16)
    n = L + 2 * POOL_PAD
    shifted = lambda x, s: pltpu.roll(x, (-s) % n, 0)
    for gi, win in enumerate(POOL_WINDOWS):
        u = u_ref[:, gi * POOL_G:(gi + 1) * POOL_G]
        pad = jnp.zeros((POOL_PAD, POOL_G), F32)
        ext = jnp.concatenate([pad, u, pad], axis=0)
        s = ext + shifted(ext, -1)
        half = 1
        while 2 * half < win:
            s = shifted(s, -half) + shifted(s, half)
            half *= 2
        s = s[POOL_PAD:POOL_PAD + L]

        def mean_minus_token(sl, t0):
            t = t0 + lax.broadcasted_iota(jnp.int32, (SUBLANES, 1), 0)
            cnt = jnp.minimum(t - win // 2 + win, L) - jnp.maximum(t - win // 2, 0)
            return s[sl] / cnt.astype(F32) - u[sl]

        edge = SUBLANES
        d = jnp.concatenate([mean_minus_token(slice(0, edge), 0),
                             s[edge:L - edge] * (1.0 / win) - u[edge:L - edge],
                             mean_minus_token(slice(L - edge, L), L - edge)], axis=0)
        o_ref[:, gi * POOL_G:(gi + 1) * POOL_G] = (
            jnp.dot(d.astype(BF16), w_ref[gi], preferred_element_type=F32)
            * sc_ref[:, gi * POOL_G:(gi + 1) * POOL_G])


def _pool_mix(ut, B, w, scale):
    L = ut.shape[0]
    nblk = UT_COLS // W_C
    return pl.pallas_call(
        functools.partial(_pool_kernel, L=L),
        grid=(B,),
        in_specs=[pl.BlockSpec((L, W_C), lambda b: (0, b * nblk + nblk - 1)),
                  pl.BlockSpec((len(POOL_WINDOWS), POOL_G, POOL_G), lambda b: (0, 0, 0)),
                  pl.BlockSpec((1, W_C), lambda b: (0, 0))],
        out_specs=pl.BlockSpec((L, W_C), lambda b: (0, b)),
        out_shape=jax.ShapeDtypeStruct((L, B * W_C), F32),
        compiler_params=_cparams(1),
        name="pool_mix",
    )(ut, w, scale.reshape(1, W_C))


def _silu(x):
    return x * jax.nn.sigmoid(x)


def _gelu_tanh(x):
    return x * (0.5 * (1.0 + jnp.tanh(math.sqrt(2.0 / math.pi) * (x + 0.044715 * (x * x * x)))))


def _out_kernel(*refs, has_pos, batch_major, final, B, bm):
    refs = list(refs)
    x_ref = refs.pop(0)
    pos_ref = refs.pop(0) if has_pos else None
    (mod_ref, h_ref, ua_ref, ya_ref, zb_ref, yc_ref, wga_ref, wgb_ref, wgc_ref, wma_ref, wmb_ref, wmc_ref,
     d_ref, wglu_ref, bglu_ref, wa_ref, wb_ref, wc_ref, wo_ref) = refs[:19]
    refs = refs[19:]
    fg_ref = refs.pop(0) if final else None
    xo_ref, zs_ref, ys_ref = refs[:3]
    xs_ref = refs[3] if (batch_major or final) else None
    for b in range(B):
        _put_batch(zs_ref, b, B, zb_ref[:, b * W_B:(b + 1) * W_B])
        _put_batch(ys_ref, b, B, yc_ref[:, b * W_C:(b + 1) * W_C])
    mod = mod_ref[...]
    h = h_ref[...]
    proj = lambda w_ref: jnp.dot(h, w_ref[0], preferred_element_type=F32)
    y = _gelu_tanh(ya_ref[...] + ua_ref[...] * d_ref[0])
    y = y * jax.nn.sigmoid(jnp.dot(y.astype(BF16), wglu_ref[0], preferred_element_type=F32) + bglu_ref[0])
    y_a = (y * _silu(proj(wga_ref))).astype(BF16)
    y_b = (_get_tile(zs_ref) * _silu(proj(wgb_ref))).astype(BF16)
    y_c = (_get_tile(ys_ref) * _silu(proj(wgc_ref))).astype(BF16)
    merged = (jax.nn.sigmoid(proj(wma_ref)) * jnp.dot(y_a, wa_ref[0], preferred_element_type=F32)
              + jax.nn.sigmoid(proj(wmb_ref)) * jnp.dot(y_b, wb_ref[0], preferred_element_type=F32)
              + jax.nn.sigmoid(proj(wmc_ref)) * jnp.dot(y_c, wc_ref[0], preferred_element_type=F32))
    out = jnp.dot(merged.astype(BF16), wo_ref[0], preferred_element_type=F32)
    x = _load_x(x_ref, xs_ref if batch_major else None, pos_ref, B, bm)
    x_new = x + _mod_part(mod, 2, bm) * _by_batch(out, bm)
    if final:
        r = lax.rsqrt(jnp.mean(x_new * x_new, axis=-1, keepdims=True) + EPS)
        _put_tile(xs_ref, (x_new * r * fg_ref[...]).reshape(-1, D_MODEL))
        for b in range(B):
            xo_ref[b] = _get_batch(xs_ref, b, B)
    else:
        xo_ref[...] = x_new.reshape(-1, D_MODEL)


def _out_proj(x, pos, mod, mod_blk, B, bm, h, ua, ya, zb, yc, w_in, l, d, w_glu, b_glu,
              w_a, w_b, w_c, w_o, final_g, tm):
    D = D_MODEL
    rows = x.size // D
    L = rows // B
    batch_major = x.ndim == 3
    final = final_g is not None
    tok = lambda w: pl.BlockSpec((tm, w), lambda i: (i, 0))
    seq = lambda w: pl.BlockSpec((tm // B, B * w), lambda i: (i, 0))
    full = lambda a: _const_spec(a.shape, (0,) * a.ndim)
    in_specs, args = _x_specs(x, pos, mod, mod_blk, B, bm, tm)
    in_specs += [tok(D), tok(W_A), tok(W_A), seq(W_B), seq(W_C)]
    args += [h, ua, ya, zb, yc]
    for cb in G_BLOCKS:
        in_specs.append(_const_spec((1, D, CB), (l, 0, cb)))
        args.append(w_in)
    for k in range(3):
        in_specs.append(_const_spec((1, D, D), (l, 0, M_BLOCK0 + k)))
        args.append(w_in)
    weights = [d, w_glu, b_glu, w_a, w_b, w_c, w_o]
    in_specs += [_const_spec((1,) + a.shape[1:], (l, 0, 0)) for a in weights]
    args += weights
    if final:
        in_specs.append(full(final_g))
        args.append(final_g)
    scratch = [_stage(tm, W_B), _stage(tm, W_C)]
    if batch_major or final:
        scratch.append(_stage(tm, D))
    if final:
        out_spec = pl.BlockSpec((B, tm // B, D), lambda i: (0, i, 0))
        out_shape = jax.ShapeDtypeStruct((B, L, D), F32)
    else:
        out_spec = tok(D)
        out_shape = jax.ShapeDtypeStruct((rows, D), F32)
    return pl.pallas_call(
        functools.partial(_out_kernel, has_pos=pos is not None, batch_major=batch_major, final=final,
                          B=B, bm=bm),
        grid=(rows // tm,),
        in_specs=in_specs,
        out_specs=out_spec,
        out_shape=out_shape,
        scratch_shapes=scratch,
        compiler_params=_cparams(1),
        name="out_proj",
    )(*args)


def _grid_pos_embed(L):
    rows = L // GRID_W
    r = jnp.broadcast_to(jnp.arange(rows, dtype=F32)[:, None], (rows, GRID_W)).reshape(-1)
    col = jnp.broadcast_to(jnp.arange(GRID_W, dtype=F32)[None, :], (rows, GRID_W)).reshape(-1)
    q = D_MODEL // 4
    omega = 1.0 / (10000.0 ** (jnp.arange(q, dtype=F32) / q))
    ar = r[:, None] * omega[None, :]
    ac = col[:, None] * omega[None, :]
    return jnp.concatenate([jnp.sin(ar), jnp.cos(ar), jnp.sin(ac), jnp.cos(ac)], axis=-1)


def _states_to_lanes(st_re, st_im):
    s = jnp.concatenate([st_re[:, 0], st_re[:, 1], st_im[:, 0], st_im[:, 1]], axis=-1)
    return s.transpose(1, 0, 2)


def _lanes_to_states(fin):
    f = fin.transpose(1, 0, 2).reshape(fin.shape[1], S5_G, 4, S5_P)
    return (jnp.stack([f[:, :, 0], f[:, :, 1]], axis=1), jnp.stack([f[:, :, 2], f[:, :, 3]], axis=1))


def kernel(x_prompt, x_sample, c, state_s5_re, state_s5_im, c_ctx, norm_g, w_mod, b_mod, w_in, s5_lam_re, s5_lam_im, s5_log_dt, s5_b_re, s5_b_im, s5_c_re, s5_c_im, s5_d, s5_w_glu, s5_b_glu, hy_conv_w, hy_conv_b, hy_f_w1, hy_f_b1, hy_f_w2, hy_f_b2, hy_f_freq, hy_f_w3, hy_bias, pool_w, pool_scale, w_br_a, w_br_b, w_br_c, w_out, final_g):
    Bc, Lc, D = x_prompt.shape
    Bl, Ll, _ = x_sample.shape
    assert Bl == MOD_ROWS - MOD_LAT0

    cond = jnp.zeros((MOD_ROWS, D), F32).at[0].set(c_ctx).at[MOD_LAT0:].set(c)
    mod = _modulation(cond, w_mod, b_mod).reshape(DEPTH * MOD_ROWS, 3 * D)

    pos = _grid_pos_embed(Ll)
    groups = {
        'ctx': dict(B=Bc, L=Lc, S=2, C=W_B, hy_dtype=F32, bm=1, mod_blk=0),
        'lat': dict(B=Bl, L=Ll, S=4, C=W_B // 2, hy_dtype=BF16, bm=Bl, mod_blk=1),
    }
    tables = {k: _dft_tables(v['L'] // v['S']) for k, v in groups.items()}
    w_in_b = w_in.astype(BF16)
    out_weights = [s5_d.reshape(DEPTH, 1, W_A), s5_w_glu.astype(BF16), s5_b_glu.reshape(DEPTH, 1, W_A),
                   w_br_a.astype(BF16), w_br_b.astype(BF16), w_br_c.astype(BF16), w_out.astype(BF16)]
    xs = {'ctx': x_prompt, 'lat': x_sample}
    new_re, new_im = [], []
    for l in range(DEPTH):
        s5_ops = _s5_operators(s5_lam_re[l], s5_lam_im[l], s5_log_dt[l], s5_b_re[l], s5_b_im[l],
                               s5_c_re[l], s5_c_im[l])
        g = norm_g[l].reshape(1, D)
        for name, cfg in groups.items():
            B, L, bm = cfg['B'], cfg['L'], cfg['bm']
            x = xs[name]
            p = pos if (name == 'lat' and l == 0) else None
            mod_blk = l * (MOD_ROWS // MOD_LAT0) + cfg['mod_blk']
            ua, ut, hn = _in_proj(x, p, mod, mod_blk, B, bm, g, w_in_b, l, min(TM_IN, L * B))
            if name == 'ctx':
                h0g = jnp.zeros((S5_G, B, 4 * S5_P), F32)
            else:
                h0g = _states_to_lanes(state_s5_re[:, l], state_s5_im[:, l])
            ya, fin = _s5_mix(ua, s5_ops, h0g, B)
            if name == 'ctx':
                fr, fi = _lanes_to_states(fin)
                new_re.append(fr)
                new_im.append(fi)
            fs, fst = tables[name]
            filt = _hyena_filters(L, hy_f_w1[l], hy_f_b1[l], hy_f_w2[l], hy_f_b2[l], hy_f_freq[l], hy_f_w3[l])
            hw = _filter_spectrum(fs, filt.reshape(L, 2 * W_B).astype(BF16), cfg['S'], cfg['C'], cfg['hy_dtype'])
            zb = _hyena_mix(ut, B, hy_conv_w[l], hy_conv_b[l], hy_bias[l], fs, fst, hw, cfg['C'])
            yc = _pool_mix(ut, B, pool_w[l].astype(BF16), pool_scale[l])
            fg = final_g.reshape(1, D) if l == DEPTH - 1 else None
            xs[name] = _out_proj(x, p, mod, mod_blk, B, bm, hn, ua, ya, zb, yc, w_in_b, l,
                                 *out_weights, fg, min(TM_OUT, L * B))
    return (xs['ctx'], xs['lat'], jnp.stack(new_re, axis=1), jnp.stack(new_im, axis=1))
```
